```python
import jax
import jax.numpy as jnp
from jax import lax
import numpy as np

D_MODEL = 1024
BATCH = 16
SEQ = 4096
DEPTH = 4

GRID_W = 64
CTX_LEN = 256
N_MIXERS = 2
HEAD_SIZE = 64
N_HEADS = D_MODEL // HEAD_SIZE
DECAY_LORA = 64
AAA_LORA = 64
MV_LORA = 32
GATE_LORA = 128
N_DIRS = 2
POOL_WINDOWS = (2, 4, 8, 16)
N_POOL_GROUPS = len(POOL_WINDOWS)
POOL_GROUP = D_MODEL // N_POOL_GROUPS
N_EXPERTS = 16
EXPERT_FF = 2 * D_MODEL
CAPACITY_FACTOR = 2
N_RWKV = (DEPTH + N_MIXERS - 1) // N_MIXERS
N_POOL = DEPTH // N_MIXERS
RMS_EPS = 1e-6
GN_EPS = 64e-5

kernel_name = 'hybrid_rwkv7_pool_ecmoe_dit'


def rmsnorm(x, g):
    xf = x.astype(jnp.float32)
    y = xf * lax.rsqrt(jnp.mean(xf * xf, axis=-1, keepdims=True) + RMS_EPS)
    return (y * g.astype(jnp.float32)).astype(x.dtype)


def ada_mod(cond, w, b):
    m = (jax.nn.silu(cond) @ w + b)[..., None, :]
    return jnp.split(m, 6, axis=-1)


def modulate(h, shift, scale):
    return h * (1 + scale) + shift


def grid_shift(x):
    B, T, D = x.shape
    rows = T // GRID_W
    q = D // 4
    g = x.reshape(B, rows, GRID_W, D)
    left = jnp.pad(g[:, :, :-1, :q], ((0, 0), (0, 0), (1, 0), (0, 0)))
    right = jnp.pad(g[:, :, 1:, q:2 * q], ((0, 0), (0, 0), (0, 1), (0, 0)))
    up = jnp.pad(g[:, :-1, :, 2 * q:3 * q], ((0, 0), (1, 0), (0, 0), (0, 0)))
    down = jnp.pad(g[:, 1:, :, 3 * q:], ((0, 0), (0, 1), (0, 0), (0, 0)))
    return jnp.concatenate([left, right, up, down], axis=-1).reshape(B, T, D)


def seq_shift(x):
    h = x.shape[-1] // 2
    prev = jnp.pad(x[:, :-1, :h], ((0, 0), (1, 0), (0, 0)))
    nxt = jnp.pad(x[:, 1:, h:], ((0, 0), (0, 1), (0, 0)))
    return jnp.concatenate([prev, nxt], axis=-1)


def rwkv_project(h, shifted, p, vres, v_first):
    B, T, D = h.shape
    f32 = jnp.float32

    def heads(t):
        return t.astype(f32).reshape(B, T, N_HEADS, HEAD_SIZE)

    xx = shifted - h
    xr, xw, xk, xv, xa, xg = [h + xx * p['mix'][n] for n in range(6)]
    r = xr @ p['wrkv'][0]
    k = xk @ p['wrkv'][1]
    v = xv @ p['wrkv'][2]
    if vres is not None:
        v0, v1, v2 = vres
        v = v + (v_first - v) * jax.nn.sigmoid(v0 + (xv @ v1) @ v2)
    g = jax.nn.sigmoid(xg @ p['g1']) @ p['g2']
    kh = heads(k)
    kk = heads(k * p['k_k'])
    kk = kk / jnp.maximum(jnp.sqrt(jnp.sum(kk * kk, axis=-1, keepdims=True)), 1e-12)
    k_a = p['k_a'].astype(f32).reshape(N_HEADS, HEAD_SIZE)
    dirs = []
    for d in range(N_DIRS):
        w_pre = (p['w0'][d] + jnp.tanh(xw @ p['w1'][d]) @ p['w2'][d]).astype(f32)
        w_log = -jax.nn.softplus(-w_pre) - 0.5
        decay = heads(jnp.exp(-jnp.exp(w_log)))
        a = heads(jax.nn.sigmoid((p['a0'][d] + (xa @ p['a1'][d]) @ p['a2'][d]).astype(f32)))
        k_d = kh * (1 + (a - 1) * k_a)
        dirs.append((decay, kk * a, k_d))
    return heads(r), heads(v), v, g, kk, dirs


def wkv_scan(S0, r, decay, kk, b, k, v, reverse):
    def step(S, inp):
        r_t, w_t, kk_t, b_t, k_t, v_t = inp
        S = (S * w_t[:, :, None, :]
             + jnp.einsum('bhi,bhj->bhij', jnp.einsum('bhij,bhj->bhi', S, -kk_t), b_t)
             + jnp.einsum('bhi,bhj->bhij', v_t, k_t))
        return S, jnp.einsum('bhij,bhj->bhi', S, r_t)

    xs = tuple(jnp.moveaxis(t, 1, 0) for t in (r, decay, kk, b, k, v))
    S, y = lax.scan(step, S0, xs, reverse=reverse)
    return S, jnp.moveaxis(y, 0, 1)


def rwkv_output(o, r, kb, v, g, p, dtype):
    B, T, H, N = o.shape
    mu = jnp.mean(o, axis=-1, keepdims=True)
    var = jnp.mean(jnp.square(o - mu), axis=-1, keepdims=True)
    on = ((o - mu) * lax.rsqrt(var + GN_EPS)).reshape(B, T, H * N)
    on = on * p['ln_w'].astype(jnp.float32) + p['ln_b'].astype(jnp.float32)
    bonus = (jnp.sum(r * kb * p['r_k'].astype(jnp.float32), axis=-1, keepdims=True) * v).reshape(B, T, H * N)
    y = (on + bonus).astype(dtype) * g
    return y @ p['wo']


def rwkv_time_mix(h_ctx, h_lat, p, vres, v_first):
    vf_ctx = None if v_first is None else v_first[0]
    vf_lat = None if v_first is None else v_first[1]
    r_c, vh_c, v_c, g_c, kk_c, dirs_c = rwkv_project(h_ctx, seq_shift(h_ctx), p, vres, vf_ctx)
    r_l, vh_l, v_l, g_l, kk_l, dirs_l = rwkv_project(h_lat, grid_shift(h_lat), p, vres, vf_lat)
    new_v_first = (v_c, v_l) if v_first is None else v_first
    B = h_lat.shape[0]
    S_zero = jnp.zeros((B, N_HEADS, HEAD_SIZE, HEAD_SIZE), jnp.float32)
    o_c = jnp.zeros_like(r_c)
    o_l = jnp.zeros_like(r_l)
    kb_c = jnp.zeros_like(r_c)
    kb_l = jnp.zeros_like(r_l)
    for d, rev in enumerate((False, True)):
        dec_c, b_c, k_c = dirs_c[d]
        dec_l, b_l, k_l = dirs_l[d]
        S_ctx, y_c = wkv_scan(S_zero, r_c, dec_c, kk_c, b_c, k_c, vh_c, rev)
        _, y_l = wkv_scan(S_ctx, r_l, dec_l, kk_l, b_l, k_l, vh_l, rev)
        o_c = o_c + y_c
        o_l = o_l + y_l
        kb_c = kb_c + 0.5 * k_c
        kb_l = kb_l + 0.5 * k_l
    y_ctx = rwkv_output(o_c, r_c, kb_c, vh_c, g_c, p, h_ctx.dtype)
    y_lat = rwkv_output(o_l, r_l, kb_l, vh_l, g_l, p, h_lat.dtype)
    return y_ctx, y_lat, new_v_first


def pool_mix(h, w, scale):
    B, T, D = h.shape
    t = jnp.arange(T)
    parts = []
    for gi, win in enumerate(POOL_WINDOWS):
        hg = h[..., gi * POOL_GROUP:(gi + 1) * POOL_GROUP].astype(jnp.float32)
        P = jnp.pad(jnp.cumsum(hg, axis=1), ((0, 0), (1, 0), (0, 0)))
        half = win // 2
        hi = jnp.minimum(t + half, T)
        lo = jnp.maximum(t - half, 0)
        cnt = (hi - lo).astype(jnp.float32)[None, :, None]
        mean = (jnp.take(P, hi, axis=1) - jnp.take(P, lo, axis=1)) / cnt
        parts.append(mean - hg)
    dlt = jnp.stack(parts, axis=2).astype(h.dtype)
    y = jnp.einsum('btgc,gce->btge', dlt, w).reshape(B, T, D)
    return y * scale


def ec_moe(h, router, w1, w3, w2):
    B, T, D = h.shape
    cap = CAPACITY_FACTOR * T // N_EXPERTS
    aff = jax.nn.softmax(jnp.einsum('btd,de->bte', h, router).astype(jnp.float32), axis=-1)
    gates, idx = lax.top_k(jnp.swapaxes(aff, 1, 2), cap)
    bi = jnp.arange(B)[:, None, None]
    xin = h[bi, idx]
    hid = jax.nn.silu(jnp.einsum('becd,edf->becf', xin, w1)) * jnp.einsum('becd,edf->becf', xin, w3)
    y = jnp.einsum('becf,efd->becd', hid, w2) * gates[..., None].astype(h.dtype)
    return jnp.zeros_like(h).at[bi, idx].add(y)


def setup_inputs(seed: int = 0) -> dict:
    key = jax.random.key(seed)
    ks = iter(list(jax.random.split(key, 40)))

    def nrm(shape, scale):
        return jax.random.normal(next(ks), shape, jnp.float32) * scale

    def unif(shape, lo, hi):
        return jax.random.uniform(next(ks), shape, jnp.float32, lo, hi)

    D = D_MODEL
    NR = N_RWKV
    NP = N_POOL
    sd = D ** -0.5
    return {
        'x': nrm((BATCH, SEQ, D), 1.0),
        'c': nrm((BATCH, D), 1.0),
        'ctx': nrm((BATCH, CTX_LEN, D), 1.0),
        'c_ctx': nrm((D,), 1.0),
        'ada_w': nrm((DEPTH, D, 6 * D), 0.5 * sd),
        'ada_b': nrm((DEPTH, 6 * D), 0.02),
        'norm1_g': 1.0 + nrm((DEPTH, D), 0.05),
        'norm2_g': 1.0 + nrm((DEPTH, D), 0.05),
        'rwkv_mix': unif((NR, 6, D), 0.0, 1.0),
        'rwkv_wrkv': nrm((NR, 3, D, D), sd),
        'rwkv_w0': unif((NR, N_DIRS, D), -6.0, 1.0),
        'rwkv_w1': nrm((NR, N_DIRS, D, DECAY_LORA), sd),
        'rwkv_w2': nrm((NR, N_DIRS, DECAY_LORA, D), 0.5 * DECAY_LORA ** -0.5),
        'rwkv_a0': nrm((NR, N_DIRS, D), 0.1),
        'rwkv_a1': nrm((NR, N_DIRS, D, AAA_LORA), sd),
        'rwkv_a2': nrm((NR, N_DIRS, AAA_LORA, D), 0.5 * AAA_LORA ** -0.5),
        'rwkv_v0': 1.0 + nrm((NR - 1, D), 0.1),
        'rwkv_v1': nrm((NR - 1, D, MV_LORA), sd),
        'rwkv_v2': nrm((NR - 1, MV_LORA, D), 0.5 * MV_LORA ** -0.5),
        'rwkv_g1': nrm((NR, D, GATE_LORA), sd),
        'rwkv_g2': nrm((NR, GATE_LORA, D), GATE_LORA ** -0.5),
        'rwkv_kk': 0.85 + nrm((NR, D), 0.05),
        'rwkv_ka': 1.0 + nrm((NR, D), 0.05),
        'rwkv_rk': nrm((NR, N_HEADS, HEAD_SIZE), 0.1),
        'rwkv_lnw': 1.0 + nrm((NR, D), 0.05),
        'rwkv_lnb': nrm((NR, D), 0.02),
        'rwkv_wo': nrm((NR, D, D), sd),
        'pool_w': nrm((NP, N_POOL_GROUPS, POOL_GROUP, POOL_GROUP), POOL_GROUP ** -0.5),
        'pool_scale': 1.0 + nrm((NP, D), 0.1),
        'moe_router': nrm((DEPTH, D, N_EXPERTS), sd),
        'moe_w1': nrm((DEPTH, N_EXPERTS, D, EXPERT_FF), sd),
        'moe_w3': nrm((DEPTH, N_EXPERTS, D, EXPERT_FF), sd),
        'moe_w2': nrm((DEPTH, N_EXPERTS, EXPERT_FF, D), EXPERT_FF ** -0.5),
        'final_g': 1.0 + nrm((D,), 0.05),
    }


def reference(x, c, ctx, c_ctx, ada_w, ada_b, norm1_g, norm2_g, rwkv_mix, rwkv_wrkv,
              rwkv_w0, rwkv_w1, rwkv_w2, rwkv_a0, rwkv_a1, rwkv_a2, rwkv_v0, rwkv_v1, rwkv_v2,
              rwkv_g1, rwkv_g2, rwkv_kk, rwkv_ka, rwkv_rk, rwkv_lnw, rwkv_lnb, rwkv_wo,
              pool_w, pool_scale, moe_router, moe_w1, moe_w3, moe_w2, final_g):
    v_first = None
    for i in range(DEPTH):
        last = i == DEPTH - 1
        is_rwkv = i % N_MIXERS == 0
        j = i // N_MIXERS
        m_lat = ada_mod(c, ada_w[i], ada_b[i])
        need_ctx = (not last) or is_rwkv
        m_ctx = ada_mod(c_ctx, ada_w[i], ada_b[i]) if need_ctx else None

        h_lat = modulate(rmsnorm(x, norm1_g[i]), m_lat[0], m_lat[1])
        if is_rwkv:
            h_ctx = modulate(rmsnorm(ctx, norm1_g[i]), m_ctx[0], m_ctx[1])
            p = {'mix': rwkv_mix[j], 'wrkv': rwkv_wrkv[j],
                 'w0': rwkv_w0[j], 'w1': rwkv_w1[j], 'w2': rwkv_w2[j],
                 'a0': rwkv_a0[j], 'a1': rwkv_a1[j], 'a2': rwkv_a2[j],
                 'g1': rwkv_g1[j], 'g2': rwkv_g2[j], 'k_k': rwkv_kk[j], 'k_a': rwkv_ka[j],
                 'r_k': rwkv_rk[j], 'ln_w': rwkv_lnw[j], 'ln_b': rwkv_lnb[j], 'wo': rwkv_wo[j]}
            vres = None if j == 0 else (rwkv_v0[j - 1], rwkv_v1[j - 1], rwkv_v2[j - 1])
            y_ctx, y_lat, v_first = rwkv_time_mix(h_ctx, h_lat, p, vres, v_first)
            x = x + m_lat[2] * y_lat
            if not last:
                ctx = ctx + m_ctx[2] * y_ctx
        else:
            x = x + m_lat[2] * pool_mix(h_lat, pool_w[j], pool_scale[j])
            if not last:
                h_ctx = modulate(rmsnorm(ctx, norm1_g[i]), m_ctx[0], m_ctx[1])
                ctx = ctx + m_ctx[2] * pool_mix(h_ctx, pool_w[j], pool_scale[j])

        h_lat = modulate(rmsnorm(x, norm2_g[i]), m_lat[3], m_lat[4])
        x = x + m_lat[5] * ec_moe(h_lat, moe_router[i], moe_w1[i], moe_w3[i], moe_w2[i])
        if not last:
            h_ctx = modulate(rmsnorm(ctx, norm2_g[i]), m_ctx[3], m_ctx[4])
            ctx = ctx + m_ctx[5] * ec_moe(h_ctx, moe_router[i], moe_w1[i], moe_w3[i], moe_w2[i])
    return rmsnorm(x, final_g)
```

```python
import functools
import math

import jax
import jax.numpy as jnp
from jax import lax
from jax.experimental import pallas as pl
from jax.experimental.pallas import tpu as pltpu

F32 = jnp.float32
BF16 = jnp.bfloat16
I32 = jnp.int32
HIGHEST = lax.Precision.HIGHEST

GRID_W = 64
POOL_WINDOWS = (2, 4, 8, 16)
CAPACITY_FACTOR = 2
RMS_EPS = 1e-6
GN_EPS = 64e-5
EXP_NEG_HALF = math.exp(-0.5)

LANES = 128
SUBLANES = 8
VMEM_LIMIT_BYTES = 56 * 1024 * 1024

TOKEN_TILE = 256
WKV_CHUNK = 64
POOL_HALO = 8
HEAD_COLS = LANES


def _cparams(sem):
    return pltpu.CompilerParams(dimension_semantics=sem, vmem_limit_bytes=VMEM_LIMIT_BYTES)


def _dot(a, b):
    return jnp.dot(a, b, preferred_element_type=F32)


def _dot_nt(a, b):
    return lax.dot_general(a, b, (((1,), (1,)), ((), ())), preferred_element_type=F32)


def _dot_tn(a, b):
    return lax.dot_general(a, b, (((0,), (0,)), ((), ())), preferred_element_type=F32)


def _split3(x):
    hi = x.astype(BF16)
    r1 = x - hi.astype(F32)
    mid = r1.astype(BF16)
    lo = (r1 - mid.astype(F32)).astype(BF16)
    return hi, mid, lo


def _dot3(x, m):
    hi, mid, lo = _split3(x)
    return _dot(hi, m) + _dot(mid, m) + _dot(lo, m)


def _dot3_left(m, x):
    hi, mid, lo = _split3(x)
    return _dot(m, hi) + _dot(m, mid) + _dot(m, lo)


def _sigmoid(x):
    return 1.0 / (1.0 + jnp.exp(-x))


def _norm_mod(x, g, shift, scale):
    ms = jnp.mean(x * x, axis=-1, keepdims=True)
    y = x * lax.rsqrt(ms + RMS_EPS) * g
    return y * (1.0 + scale) + shift


def _ada_kernel(c_ref, w_ref, b_ref, o_ref):
    c = c_ref[...]
    s = c * _sigmoid(c)
    o_ref[0] = jnp.dot(s, w_ref[0], precision=HIGHEST, preferred_element_type=F32) + b_ref[0]


def _ada_all(cond, ada_w, ada_b):
    depth, d, n6 = ada_w.shape
    rows = cond.shape[0]
    nt = 512
    return pl.pallas_call(
        _ada_kernel,
        grid=(depth, n6 // nt),
        in_specs=[
            pl.BlockSpec((rows, d), lambda i, n: (0, 0)),
            pl.BlockSpec((1, d, nt), lambda i, n: (i, 0, n)),
            pl.BlockSpec((1, 1, nt), lambda i, n: (i, 0, n)),
        ],
        out_specs=pl.BlockSpec((1, rows, nt), lambda i, n: (i, 0, n)),
        out_shape=jax.ShapeDtypeStruct((depth, rows, n6), F32),
        name="ada_mod",
        compiler_params=_cparams(("arbitrary", "arbitrary")),
    )(cond, ada_w, ada_b.reshape(depth, 1, n6))


def _rwkv_proj_kernel(nl, seq, ctx_len, has_vres, *refs):
    if has_vres:
        (x_ref, xp_ref, xn_ref, mod_ref, gn_ref, mix_ref, wr_ref, wk_ref, wv_ref, g1_ref, g2_ref,
         w1_ref, w2_ref, w0_ref, a1_ref, a2_ref, a0_ref, kkw_ref, kaw_ref, hs_ref, hst_ref,
         v0_ref, v1_ref, v2_ref, vf_ref,
         r_o, v_o, g_o, kk_o, lw0_o, lw1_o, b0_o, b1_o, kd0_o, kd1_o, ext_scr, sh_scr) = refs
    else:
        (x_ref, xp_ref, xn_ref, mod_ref, gn_ref, mix_ref, wr_ref, wk_ref, wv_ref, g1_ref, g2_ref,
         w1_ref, w2_ref, w0_ref, a1_ref, a2_ref, a0_ref, kkw_ref, kaw_ref, hs_ref, hst_ref,
         r_o, v_o, g_o, kk_o, lw0_o, lw1_o, b0_o, b1_o, kd0_o, kd1_o, ext_scr, sh_scr) = refs
    tt = x_ref.shape[1]
    d = x_ref.shape[2]
    hw = GRID_W
    j = pl.program_id(1)
    shift = mod_ref[0, 0, 0:1, :]
    scale = mod_ref[0, 0, 1:2, :]
    gn = gn_ref[...]
    ext_scr[0:hw, :] = _norm_mod(xp_ref[0], gn, shift, scale)
    ext_scr[hw:hw + tt, :] = _norm_mod(x_ref[0], gn, shift, scale)
    ext_scr[hw + tt:, :] = _norm_mod(xn_ref[0], gn, shift, scale)

    i = lax.broadcasted_iota(I32, (tt, 1), 0)
    q = d // 4

    @pl.when(j < nl)
    def _latent_shift():
        t = j * tt + i
        col = i % hw
        sh_scr[:, 0:q] = jnp.where(col != 0, ext_scr[hw - 1:hw - 1 + tt, 0:q], 0.0)
        sh_scr[:, q:2 * q] = jnp.where(col != hw - 1, ext_scr[hw + 1:hw + 1 + tt, q:2 * q], 0.0)
        sh_scr[:, 2 * q:3 * q] = jnp.where(t >= hw, ext_scr[0:tt, 2 * q:3 * q], 0.0)
        sh_scr[:, 3 * q:] = jnp.where(t < seq - hw, ext_scr[2 * hw:2 * hw + tt, 3 * q:], 0.0)

    @pl.when(j >= nl)
    def _context_shift():
        t = (j - nl) * tt + i
        hd = d // 2
        sh_scr[:, 0:hd] = jnp.where(t != 0, ext_scr[hw - 1:hw - 1 + tt, 0:hd], 0.0)
        sh_scr[:, hd:] = jnp.where(t != ctx_len - 1, ext_scr[hw + 1:hw + 1 + tt, hd:], 0.0)

    h = ext_scr[hw:hw + tt, :]
    xx = sh_scr[...] - h

    def mixed(n):
        return (h + xx * mix_ref[n:n + 1, :]).astype(BF16)

    xr, xw, xk, xv, xa, xg = [mixed(n) for n in range(6)]
    r = _dot(xr, wr_ref[...])
    k = _dot(xk, wk_ref[...])
    v = _dot(xv, wv_ref[...])
    if has_vres:
        lor = _dot(_dot(xv, v1_ref[...]).astype(BF16), v2_ref[...])
        v = v + (vf_ref[0] - v) * _sigmoid(v0_ref[...] + lor)
    g = _dot(_sigmoid(_dot(xg, g1_ref[...])).astype(BF16), g2_ref[...])
    r_o[0] = r
    v_o[0] = v
    g_o[0] = g

    kkr = k * kkw_ref[...]
    ss = _dot3(kkr * kkr, hs_ref[...])
    inv = 1.0 / jnp.maximum(jnp.sqrt(ss), 1e-12)
    kk = kkr * _dot3(inv, hst_ref[...])
    kk_o[0] = kk
    kaw = kaw_ref[...]
    for dr, (lw_o, b_o, kd_o) in enumerate(((lw0_o, b0_o, kd0_o), (lw1_o, b1_o, kd1_o))):
        wpre = w0_ref[dr:dr + 1, :] + _dot(jnp.tanh(_dot(xw, w1_ref[dr])).astype(BF16), w2_ref[dr])
        lw_o[0] = -EXP_NEG_HALF * _sigmoid(wpre)
        a = _sigmoid(a0_ref[dr:dr + 1, :] + _dot(_dot(xa, a1_ref[dr]).astype(BF16), a2_ref[dr]))
        b_o[0] = kk * a
        kd_o[0] = k * (1.0 + (a - 1.0) * kaw)


def _rwkv_proj(xc, mod, gn, p, vres, v_first, nl, seq, ctx_len):
    bsz, tc, d = xc.shape
    tt = TOKEN_TILE
    nt = tc // tt
    hb = tt // GRID_W
    nhb = tc // GRID_W
    has_vres = vres is not None

    def full(a):
        nd = a.ndim
        return pl.BlockSpec(a.shape, lambda b, j, _n=nd: (0,) * _n)

    tile = pl.BlockSpec((1, tt, d), lambda b, j: (b, j, 0))
    ins = [xc, xc, xc, mod, gn, p['mix'], p['wr'], p['wk'], p['wv'], p['g1'], p['g2'],
           p['w1'], p['w2'], p['w0'], p['a1'], p['a2'], p['a0'], p['k_k'], p['k_a'], p['hs'], p['hst']]
    specs = [
        tile,
        pl.BlockSpec((1, GRID_W, d), lambda b, j: (b, jnp.maximum(j * hb - 1, 0), 0)),
        pl.BlockSpec((1, GRID_W, d), lambda b, j: (b, jnp.minimum((j + 1) * hb, nhb - 1), 0)),
        pl.BlockSpec((1, 1, 6, d), lambda b, j: (b, (j >= nl).astype(I32), 0, 0)),
    ] + [full(a) for a in ins[4:]]
    if has_vres:
        ins += [vres[0], vres[1], vres[2], v_first]
        specs += [full(vres[0]), full(vres[1]), full(vres[2]), tile]
    out_sds = jax.ShapeDtypeStruct((bsz, tc, d), F32)
    return pl.pallas_call(
        functools.partial(_rwkv_proj_kernel, nl, seq, ctx_len, has_vres),
        grid=(bsz, nt),
        in_specs=specs,
        out_specs=[tile] * 10,
        out_shape=[out_sds] * 10,
        scratch_shapes=[pltpu.VMEM((tt + 2 * GRID_W, d), F32), pltpu.VMEM((tt, d), F32)],
        name="rwkv_proj",
        compiler_params=_cparams(("parallel", "arbitrary")),
    )(*ins)


def _wkv_chain(r, v, kk, lw, b, kd, s_prev, reverse):
    L = r.shape[0]
    w2 = r.shape[1]
    n = w2 // 2
    row = lax.broadcasted_iota(I32, (L, L), 0)
    colm = lax.broadcasted_iota(I32, (L, L), 1)
    tri = jnp.where((colm >= row) if reverse else (colm <= row), 1.0, 0.0).astype(BF16)
    c = _dot3_left(tri, lw)
    ce = c - lw
    ctot = c[0:1, :] if reverse else c[L - 1:L, :]
    e_c = jnp.exp(c)
    e_nc = jnp.exp(-c)
    e_tc = jnp.exp(ctot - c)
    ah = -kk * jnp.exp(ce)
    rh = r * e_c
    bh = b * e_nc
    kh = kd * e_nc
    bt = (b * e_tc).astype(BF16)
    kt = (kd * e_tc).astype(BF16)

    lane = lax.broadcasted_iota(I32, (1, w2), 1)
    m0 = jnp.where(lane < n, 1.0, 0.0)
    m1 = 1.0 - m0

    def stack(x):
        return jnp.concatenate([x * m0, x * m1], axis=0).astype(BF16)

    def pp(pm, x):
        return _dot(pm.astype(BF16), stack(x))

    lhs = jnp.concatenate([ah, rh], axis=0).astype(BF16)
    rhs = jnp.concatenate([stack(bh), stack(kh)], axis=0)
    aa = _dot_nt(lhs, rhs)
    t_i = lax.broadcasted_iota(I32, (L, 2 * L), 0)
    s_i = lax.broadcasted_iota(I32, (L, 2 * L), 1) % L
    strict = (s_i > t_i) if reverse else (s_i < t_i)
    incl = (s_i >= t_i) if reverse else (s_i <= t_i)
    a_ab = jnp.where(strict, aa[0:L, 0:2 * L], 0.0)
    a_ak = jnp.where(strict, aa[0:L, 2 * L:4 * L], 0.0)
    a_rb = jnp.where(incl, aa[L:2 * L, 0:2 * L], 0.0)
    a_rk = jnp.where(incl, aa[L:2 * L, 2 * L:4 * L], 0.0)

    tm = jnp.where(s_i == t_i, 1.0, 0.0) + a_ab
    pw = a_ab
    steps = int(math.log2(L)) - 1
    for _ in range(steps):
        pw = pp(pw, pw)
        tm = tm + pp(tm, pw)

    sb = s_prev.astype(BF16)
    wmat = _dot_nt(ah.astype(BF16), sb) + pp(a_ak, v)
    u = pp(tm, wmat)
    y = _dot_nt(rh.astype(BF16), sb) + pp(a_rb, u) + pp(a_rk, v)
    ri = lax.broadcasted_iota(I32, (w2, w2), 0)
    ci = lax.broadcasted_iota(I32, (w2, w2), 1)
    bd = (ri < n) == (ci < n)
    upd = _dot_tn(u.astype(BF16), bt) + _dot_tn(v.astype(BF16), kt)
    s_new = s_prev * jnp.exp(ctot) + jnp.where(bd, upd, 0.0)
    return y, s_new


def _wkv_kernel(rf, vf, kkf, lwf, bf, kdf, rr, vr, kkr, lwr, br, kdr, yf_o, yr_o, sf_scr, sr_scr):
    @pl.when(pl.program_id(2) == 0)
    def _init():
        sf_scr[...] = jnp.zeros_like(sf_scr)
        sr_scr[...] = jnp.zeros_like(sr_scr)

    y, s = _wkv_chain(rf[0], vf[0], kkf[0], lwf[0], bf[0], kdf[0], sf_scr[...], False)
    yf_o[0] = y
    sf_scr[...] = s
    y, s = _wkv_chain(rr[0], vr[0], kkr[0], lwr[0], br[0], kdr[0], sr_scr[...], True)
    yr_o[0] = y
    sr_scr[...] = s


def _wkv(r, v, kk, lw0, b0, kd0, lw1, b1, kd1, seq, ctx_len):
    bsz, tc, d = r.shape
    L = WKV_CHUNK
    nlc = seq // L
    ncc = ctx_len // L
    nch = nlc + ncc
    hp = d // LANES

    def fwd_map(b, h, c):
        return (b, jnp.where(c < ncc, nlc + c, c - ncc), h)

    def rev_map(b, h, c):
        return (b, jnp.where(c < ncc, nlc + ncc - 1 - c, nlc - 1 - (c - ncc)), h)

    fs = pl.BlockSpec((1, L, LANES), fwd_map)
    rs = pl.BlockSpec((1, L, LANES), rev_map)
    sds = jax.ShapeDtypeStruct((bsz, tc, d), F32)
    return pl.pallas_call(
        _wkv_kernel,
        grid=(bsz, hp, nch),
        in_specs=[fs] * 6 + [rs] * 6,
        out_specs=[fs, rs],
        out_shape=[sds, sds],
        scratch_shapes=[pltpu.VMEM((LANES, LANES), F32), pltpu.VMEM((LANES, LANES), F32)],
        name="wkv_scan",
        compiler_params=_cparams(("parallel", "parallel", "arbitrary")),
    )(r, v, kk, lw0, b0, kd0, r, v, kk, lw1, b1, kd1)


def _rwkv_out_kernel(head, x_ref, yf_ref, yr_ref, r_ref, v_ref, g_ref, kd0_ref, kd1_ref, mod_ref,
                     rk_ref, lnw_ref, lnb_ref, wo_ref, hs_ref, hst_ref, o_ref):
    hs = hs_ref[...]
    hst = hst_ref[...]
    o = yf_ref[0] + yr_ref[0]
    inv_n = 1.0 / head
    mu = _dot3(_dot3(o, hs) * inv_n, hst)
    dlt = o - mu
    var = _dot3(dlt * dlt, hs) * inv_n
    on = dlt * _dot3(lax.rsqrt(var + GN_EPS), hst)
    on = on * lnw_ref[...] + lnb_ref[...]
    kb = 0.5 * kd0_ref[0] + 0.5 * kd1_ref[0]
    bonus = _dot3(_dot3(r_ref[0] * kb * rk_ref[...], hs), hst) * v_ref[0]
    y = ((on + bonus) * g_ref[0]).astype(BF16)
    gate = mod_ref[0, 0, 2:3, :]
    o_ref[0] = x_ref[0] + gate * _dot(y, wo_ref[...])


def _rwkv_out(xc, yf, yr, r, v, g, kd0, kd1, mod, p, nl, head):
    bsz, tc, d = xc.shape
    tt = TOKEN_TILE
    tile = pl.BlockSpec((1, tt, d), lambda b, j: (b, j, 0))

    def full(a):
        nd = a.ndim
        return pl.BlockSpec(a.shape, lambda b, j, _n=nd: (0,) * _n)

    consts = [p['r_k'], p['ln_w'], p['ln_b'], p['wo'], p['hs'], p['hst']]
    return pl.pallas_call(
        functools.partial(_rwkv_out_kernel, head),
        grid=(bsz, tc // tt),
        in_specs=[tile] * 8 + [pl.BlockSpec((1, 1, 6, d), lambda b, j: (b, (j >= nl).astype(I32), 0, 0))]
        + [full(a) for a in consts],
        out_specs=tile,
        out_shape=jax.ShapeDtypeStruct((bsz, tc, d), F32),
        input_output_aliases={0: 0},
        name="rwkv_out",
        compiler_params=_cparams(("parallel", "arbitrary")),
    )(xc, yf, yr, r, v, g, kd0, kd1, mod, *consts)


def _pool_kernel(nl, seq, ctx_len, x_ref, xp_ref, xn_ref, mod_ref, gn_ref, w_ref, sc_ref, o_ref, ext_scr):
    tt = x_ref.shape[1]
    d = x_ref.shape[2]
    ph = POOL_HALO
    j = pl.program_id(1)
    nct = ctx_len // tt
    shift = mod_ref[0, 0, 0:1, :]
    scale = mod_ref[0, 0, 1:2, :]
    gn = gn_ref[...]
    first = jnp.logical_or(j == 0, j == nl)
    last = jnp.logical_or(j == nl - 1, j == nl + nct - 1)
    hp = _norm_mod(xp_ref[0], gn, shift, scale)
    hn = _norm_mod(xn_ref[0], gn, shift, scale)
    ext_scr[0:ph, :] = jnp.where(first, 0.0, hp)
    ext_scr[ph:ph + tt, :] = _norm_mod(x_ref[0], gn, shift, scale)
    ext_scr[ph + tt:, :] = jnp.where(last, 0.0, hn)

    i = lax.broadcasted_iota(I32, (tt, 1), 0)
    t = jnp.where(j < nl, j * tt + i, (j - nl) * tt + i)
    tseg = jnp.where(j < nl, seq, ctx_len)
    ng = len(POOL_WINDOWS)
    dg = d // ng
    gate = mod_ref[0, 0, 2:3, :]
    for gi, win in enumerate(POOL_WINDOWS):
        half = win // 2
        cs = slice(gi * dg, (gi + 1) * dg)
        acc = ext_scr[ph - half:ph - half + tt, cs]
        for o in range(1, win):
            acc = acc + ext_scr[ph - half + o:ph - half + o + tt, cs]
        cnt = (jnp.minimum(t + half, tseg) - jnp.maximum(t - half, 0)).astype(F32)
        hg = ext_scr[ph:ph + tt, cs]
        dlt = (acc / cnt - hg).astype(BF16)
        y = _dot(dlt, w_ref[gi]) * sc_ref[:, cs]
        o_ref[0, :, cs] = x_ref[0, :, cs] + gate[:, cs] * y


def _pool(xc, mod, gn, w, sc, nl, seq, ctx_len, n_tiles):
    bsz, tc, d = xc.shape
    tt = TOKEN_TILE
    hb = tt // POOL_HALO
    nhb = tc // POOL_HALO
    tile = pl.BlockSpec((1, tt, d), lambda b, j: (b, j, 0))
    return pl.pallas_call(
        functools.partial(_pool_kernel, nl, seq, ctx_len),
        grid=(bsz, n_tiles),
        in_specs=[
            tile,
            pl.BlockSpec((1, POOL_HALO, d), lambda b, j: (b, jnp.maximum(j * hb - 1, 0), 0)),
            pl.BlockSpec((1, POOL_HALO, d), lambda b, j: (b, jnp.minimum((j + 1) * hb, nhb - 1), 0)),
            pl.BlockSpec((1, 1, 6, d), lambda b, j: (b, (j >= nl).astype(I32), 0, 0)),
            pl.BlockSpec(gn.shape, lambda b, j: (0, 0)),
            pl.BlockSpec(w.shape, lambda b, j: (0, 0, 0)),
            pl.BlockSpec(sc.shape, lambda b, j: (0, 0)),
        ],
        out_specs=tile,
        out_shape=jax.ShapeDtypeStruct((bsz, n_tiles * tt, d), F32),
        scratch_shapes=[pltpu.VMEM((tt + 2 * POOL_HALO, d), F32)],
        name="pool_mix",
        compiler_params=_cparams(("parallel", "arbitrary")),
    )(xc, xc, xc, mod, gn, w, sc)


def _router_kernel(x_ref, mod_ref, gn_ref, rt_ref, h_o, aff_o):
    h = _norm_mod(x_ref[0], gn_ref[...], mod_ref[0, 0, 3:4, :], mod_ref[0, 0, 4:5, :])
    h_o[0] = h.astype(BF16)
    logits = lax.dot_general(rt_ref[...], h, (((1,), (1,)), ((), ())),
                             precision=HIGHEST, preferred_element_type=F32)
    m = jnp.max(logits, axis=0, keepdims=True)
    ex = jnp.exp(logits - m)
    aff_o[0] = ex / jnp.sum(ex, axis=0, keepdims=True)


def _router(xc, mod, gn, router_t, nl, n_tiles):
    bsz, tc, d = xc.shape
    tt = TOKEN_TILE
    e = router_t.shape[0]
    tile = pl.BlockSpec((1, tt, d), lambda b, j: (b, j, 0))
    return pl.pallas_call(
        _router_kernel,
        grid=(bsz, n_tiles),
        in_specs=[
            tile,
            pl.BlockSpec((1, 1, 6, d), lambda b, j: (b, (j >= nl).astype(I32), 0, 0)),
            pl.BlockSpec(gn.shape, lambda b, j: (0, 0)),
            pl.BlockSpec(router_t.shape, lambda b, j: (0, 0)),
        ],
        out_specs=[tile, pl.BlockSpec((1, e, tt), lambda b, j: (b, 0, j))],
        out_shape=[jax.ShapeDtypeStruct((bsz, n_tiles * tt, d), BF16),
                   jax.ShapeDtypeStruct((bsz, e, n_tiles * tt), F32)],
        name="moe_router",
        compiler_params=_cparams(("parallel", "arbitrary")),
    )(xc, mod, gn, router_t)


def _lane_cumsum_excl(x, utri):
    e, t = x.shape
    off = jnp.zeros((e, 1), F32)
    parts = []
    for kb in range(t // LANES):
        blk = x[:, kb * LANES:(kb + 1) * LANES]
        inc = _dot(blk.astype(BF16), utri)
        parts.append(inc - blk + off)
        off = off + inc[:, LANES - 1:LANES]
    return jnp.concatenate(parts, axis=1), off


def _select_kernel(cap, slot_stride, aff_ref, pos_o, post_o, gate_o):
    a = aff_ref[0]
    e, t = a.shape
    bits = pltpu.bitcast(a, I32)

    def body(it, thr):
        cand = thr | lax.shift_left(jnp.int32(1), jnp.int32(29) - it)
        cnt = jnp.sum(jnp.where(bits >= cand, 1.0, 0.0), axis=1, keepdims=True)
        return jnp.where(cnt >= cap, cand, thr)

    thr = lax.fori_loop(0, 30, body, jnp.zeros((e, 1), I32))
    gt = jnp.where(bits > thr, 1.0, 0.0)
    eq = jnp.where(bits == thr, 1.0, 0.0)
    ri = lax.broadcasted_iota(I32, (LANES, LANES), 0)
    ci = lax.broadcasted_iota(I32, (LANES, LANES), 1)
    utri = jnp.where(ri <= ci, 1.0, 0.0).astype(BF16)
    n_gt = jnp.sum(gt, axis=1, keepdims=True)
    eq_rank, _ = _lane_cumsum_excl(eq, utri)
    sel = gt + eq * jnp.where(eq_rank < cap - n_gt, 1.0, 0.0)
    pos, _ = _lane_cumsum_excl(sel, utri)
    base = pl.program_id(0) * slot_stride
    posm = jnp.where(sel > 0.5, pos.astype(I32) + base, -1)
    pos_o[0] = posm
    post_o[0] = jnp.transpose(posm.astype(F32)).astype(I32)
    gate_o[0] = jnp.where(sel > 0.5, a, 0.0)


def _select(aff, cap, slot_stride):
    bsz, e, t = aff.shape
    blk = pl.BlockSpec((1, e, t), lambda b: (b, 0, 0))
    return pl.pallas_call(
        functools.partial(_select_kernel, cap, slot_stride),
        grid=(bsz,),
        in_specs=[blk],
        out_specs=[blk, pl.BlockSpec((1, t, e), lambda b: (b, 0, 0)), blk],
        out_shape=[jax.ShapeDtypeStruct((bsz, e, t), I32),
                   jax.ShapeDtypeStruct((bsz, t, e), I32),
                   jax.ShapeDtypeStruct((bsz, e, t), F32)],
        name="moe_select",
        compiler_params=_cparams(("parallel",)),
    )(aff)


def _ffn_kernel(cap, tk, fk, h_ref, pos_ref, gate_ref, w1_ref, w3_ref, w2_ref, ys_o):
    t = h_ref.shape[1]
    d = h_ref.shape[2]
    f = w1_ref.shape[2]
    pos = pos_ref[0, 0]
    g_hi, g_mid, g_lo = _split3(gate_ref[0, 0])
    prow = lax.broadcasted_iota(I32, (SUBLANES, t), 0)
    g8 = jnp.where(prow == 0, g_hi.astype(F32),
                   jnp.where(prow == 1, g_mid.astype(F32),
                             jnp.where(prow == 2, g_lo.astype(F32), 0.0))).astype(BF16)
    slot = lax.broadcasted_iota(I32, (cap, tk), 0)
    xin = jnp.zeros((cap, d), F32)
    gsl = jnp.zeros((cap, SUBLANES), F32)
    for kt in range(t // tk):
        ts = slice(kt * tk, (kt + 1) * tk)
        onehot = jnp.where(pos[:, ts] == slot, 1.0, 0.0).astype(BF16)
        xin = xin + _dot(onehot, h_ref[0, ts, :])
        gsl = gsl + _dot_nt(onehot, g8[:, ts])
    gate_slot = jnp.sum(gsl, axis=1, keepdims=True)
    xb = xin.astype(BF16)
    y = jnp.zeros((cap, d), F32)
    for kf in range(f // fk):
        fs = slice(kf * fk, (kf + 1) * fk)
        h1 = _dot(xb, w1_ref[0, :, fs])
        h3 = _dot(xb, w3_ref[0, :, fs])
        hid = (h1 * _sigmoid(h1) * h3).astype(BF16)
        y = y + _dot(hid, w2_ref[0, fs, :])
    ys_o[0, 0] = (y * gate_slot).astype(BF16)


def _ffn(h, posm, gate, w1, w3, w2, cap):
    gsz, t, d = h.shape
    e, _, f = w1.shape
    tk = min(t, 1024)
    fk = min(f, 512)
    return pl.pallas_call(
        functools.partial(_ffn_kernel, cap, tk, fk),
        grid=(e, gsz),
        in_specs=[
            pl.BlockSpec((1, t, d), lambda ei, gi: (gi, 0, 0)),
            pl.BlockSpec((1, 1, 1, t), lambda ei, gi: (gi, ei, 0, 0)),
            pl.BlockSpec((1, 1, 1, t), lambda ei, gi: (gi, ei, 0, 0)),
            pl.BlockSpec((1, d, f), lambda ei, gi: (ei, 0, 0)),
            pl.BlockSpec((1, d, f), lambda ei, gi: (ei, 0, 0)),
            pl.BlockSpec((1, f, d), lambda ei, gi: (ei, 0, 0)),
        ],
        out_specs=pl.BlockSpec((1, 1, cap, d), lambda ei, gi: (gi, ei, 0, 0)),
        out_shape=jax.ShapeDtypeStruct((gsz, e, cap, d), BF16),
        name="moe_ffn",
        compiler_params=_cparams(("arbitrary", "arbitrary")),
    )(h, posm, gate, w1, w3, w2)


def _combine_kernel(x_ref, post_ref, ys_ref, mod_ref, o_ref):
    tt = x_ref.shape[1]
    e = ys_ref.shape[1]
    cap = ys_ref.shape[2]
    slot = lax.broadcasted_iota(I32, (tt, cap), 1)
    acc = jnp.zeros(x_ref.shape[1:], F32)
    for ei in range(e):
        onehot = jnp.where(post_ref[0, :, ei:ei + 1] == slot, 1.0, 0.0).astype(BF16)
        acc = acc + _dot(onehot, ys_ref[0, ei])
    o_ref[0] = x_ref[0] + mod_ref[0, 0, 5:6, :] * acc


def _combine(xc, post, ys, mod, tile0, n_tiles, shared_slots):
    bsz, tc, d = xc.shape
    tt = TOKEN_TILE
    _, e, cap, _ = ys.shape
    is_ctx = 1 if tile0 > 0 else 0
    ys_map = (lambda b, j: (0, 0, 0, 0)) if shared_slots else (lambda b, j: (b, 0, 0, 0))
    return pl.pallas_call(
        _combine_kernel,
        grid=(bsz, n_tiles),
        in_specs=[
            pl.BlockSpec((1, tt, d), lambda b, j: (b, tile0 + j, 0)),
            pl.BlockSpec((1, tt, e), lambda b, j: (b, j, 0)),
            pl.BlockSpec((1, e, cap, d), ys_map),
            pl.BlockSpec((1, 1, 6, d), lambda b, j: (b, is_ctx, 0, 0)),
        ],
        out_specs=pl.BlockSpec((1, tt, d), lambda b, j: (b, tile0 + j, 0)),
        out_shape=jax.ShapeDtypeStruct((bsz, tc, d), F32),
        input_output_aliases={0: 0},
        name="moe_combine",
        compiler_params=_cparams(("parallel", "arbitrary")),
    )(xc, post, ys, mod)


def _moe(xc, mod, gn, router_t, w1, w3, w2, nl, nct, seq, ctx_len, do_ctx):
    bsz, tc, d = xc.shape
    e = router_t.shape[0]
    n_tiles = nl + nct if do_ctx else nl
    h, aff = _router(xc, mod, gn, router_t, nl, n_tiles)
    cap_l = CAPACITY_FACTOR * seq // e
    pos, post, gate = _select(aff[:, :, :seq], cap_l, 0)
    ys = _ffn(h[:, :seq, :], pos.reshape(bsz, e, 1, seq), gate.reshape(bsz, e, 1, seq), w1, w3, w2, cap_l)
    xc = _combine(xc, post, ys, mod, 0, nl, False)
    if do_ctx:
        cap_c = CAPACITY_FACTOR * ctx_len // e
        pos, post, gate = _select(aff[:, :, seq:], cap_c, cap_c)
        pos = jnp.transpose(pos, (1, 0, 2)).reshape(1, e, 1, bsz * ctx_len)
        gate = jnp.transpose(gate, (1, 0, 2)).reshape(1, e, 1, bsz * ctx_len)
        h_ctx = h[:, seq:, :].reshape(1, bsz * ctx_len, d)
        ys = _ffn(h_ctx, pos, gate, w1, w3, w2, bsz * cap_c)
        xc = _combine(xc, post, ys, mod, nl, nct, True)
    return xc


def _final_kernel(x_ref, g_ref, o_ref):
    x = x_ref[0]
    ms = jnp.mean(x * x, axis=-1, keepdims=True)
    o_ref[0] = x * lax.rsqrt(ms + RMS_EPS) * g_ref[...]


def _final_norm(xc, g, seq):
    bsz, tc, d = xc.shape
    tt = TOKEN_TILE
    tile = pl.BlockSpec((1, tt, d), lambda b, j: (b, j, 0))
    return pl.pallas_call(
        _final_kernel,
        grid=(bsz, seq // tt),
        in_specs=[tile, pl.BlockSpec(g.shape, lambda b, j: (0, 0))],
        out_specs=tile,
        out_shape=jax.ShapeDtypeStruct((bsz, seq, d), F32),
        name="final_norm",
        compiler_params=_cparams(("parallel", "arbitrary")),
    )(xc, g)


def kernel(x, c, ctx, c_ctx, ada_w, ada_b, norm1_g, norm2_g, rwkv_mix, rwkv_wrkv, rwkv_w0, rwkv_w1, rwkv_w2, rwkv_a0, rwkv_a1, rwkv_a2, rwkv_v0, rwkv_v1, rwkv_v2, rwkv_g1, rwkv_g2, rwkv_kk, rwkv_ka, rwkv_rk, rwkv_lnw, rwkv_lnb, rwkv_wo, pool_w, pool_scale, moe_router, moe_w1, moe_w3, moe_w2, final_g):
    bsz, seq, d = x.shape
    ctx_len = ctx.shape[1]
    depth = ada_w.shape[0]
    n_heads, head = rwkv_rk.shape[1], rwkv_rk.shape[2]
    n_mixers = 2
    tt = TOKEN_TILE
    assert seq % tt == 0 and ctx_len % tt == 0 and tt % GRID_W == 0
    assert head == WKV_CHUNK and 2 * head == LANES and d % LANES == 0
    nl, nct = seq // tt, ctx_len // tt

    xc = jnp.concatenate([x, ctx], axis=1)

    rows = -(-(bsz + 1) // SUBLANES) * SUBLANES
    cond = jnp.zeros((rows, d), F32).at[:bsz].set(c).at[bsz].set(c_ctx)
    m_all = _ada_all(cond, ada_w, ada_b)
    m_lat = m_all[:, :bsz].reshape(depth, bsz, 1, 6, d)
    m_ctx = jnp.broadcast_to(m_all[:, bsz].reshape(depth, 1, 1, 6, d), (depth, bsz, 1, 6, d))
    mods = jnp.concatenate([m_lat, m_ctx], axis=2)

    head_of = jnp.arange(d) // head
    hs = (head_of[:, None] == jnp.arange(HEAD_COLS)[None, :]).astype(BF16)
    hst = jnp.transpose(hs)

    v_first = None
    for i in range(depth):
        last = i == depth - 1
        is_rwkv = i % n_mixers == 0
        jn = i // n_mixers
        mod = mods[i]
        gn1 = norm1_g[i].reshape(1, d)
        if is_rwkv:
            p = {
                'mix': rwkv_mix[jn],
                'wr': rwkv_wrkv[jn, 0].astype(BF16), 'wk': rwkv_wrkv[jn, 1].astype(BF16),
                'wv': rwkv_wrkv[jn, 2].astype(BF16),
                'g1': rwkv_g1[jn].astype(BF16), 'g2': rwkv_g2[jn].astype(BF16),
                'w1': rwkv_w1[jn].astype(BF16), 'w2': rwkv_w2[jn].astype(BF16), 'w0': rwkv_w0[jn],
                'a1': rwkv_a1[jn].astype(BF16), 'a2': rwkv_a2[jn].astype(BF16), 'a0': rwkv_a0[jn],
                'k_k': rwkv_kk[jn].reshape(1, d), 'k_a': rwkv_ka[jn].reshape(1, d),
                'r_k': rwkv_rk[jn].reshape(1, d), 'ln_w': rwkv_lnw[jn].reshape(1, d),
                'ln_b': rwkv_lnb[jn].reshape(1, d), 'wo': rwkv_wo[jn].astype(BF16),
                'hs': hs, 'hst': hst,
            }
            vres = None if jn == 0 else (rwkv_v0[jn - 1].reshape(1, d), rwkv_v1[jn - 1].astype(BF16),
                                         rwkv_v2[jn - 1].astype(BF16))
            r, v, g, kk, lw0, lw1, b0, b1, kd0, kd1 = _rwkv_proj(xc, mod, gn1, p, vres, v_first, nl, seq, ctx_len)
            if v_first is None:
                v_first = v
            yf, yr = _wkv(r, v, kk, lw0, b0, kd0, lw1, b1, kd1, seq, ctx_len)
            xc = _rwkv_out(xc, yf, yr, r, v, g, kd0, kd1, mod, p, nl, head)
        else:
            n_tiles = nl if last else nl + nct
            xc = _pool(xc, mod, gn1, pool_w[jn].astype(BF16), pool_scale[jn].reshape(1, d),
                       nl, seq, ctx_len, n_tiles)
        xc = _moe(xc, mod, norm2_g[i].reshape(1, d), jnp.transpose(moe_router[i]),
                  moe_w1[i].astype(BF16), moe_w3[i].astype(BF16), moe_w2[i].astype(BF16),
                  nl, nct, seq, ctx_len, not last)
    return _final_norm(xc, final_g.reshape(1, d), seq)
```

```python
import functools
import math

import jax
import jax.numpy as jnp
from jax import lax
from jax.experimental import pallas as pl
from jax.experimental.pallas import tpu as pltpu

F32 = jnp.float32
BF16 = jnp.bfloat16
I32 = jnp.int32
HIGHEST = lax.Precision.HIGHEST

GRID_W = 64
POOL_WINDOWS = (2, 4, 8, 16)
CAPACITY_FACTOR = 2
RMS_EPS = 1e-6
GN_EPS = 64e-5
EXP_NEG_HALF = math.exp(-0.5)

LANES = 128
SUBLANES = 8
BF16_ROWS = 16
VMEM_LIMIT_BYTES = 56 * 1024 * 1024

TOKEN_TILE = 256
WKV_CHUNK = 64
WKV_PAIRS_PER_STEP = 8
POOL_HALO = 8
HEAD_COLS = LANES


def _cparams(sem):
    return pltpu.CompilerParams(dimension_semantics=sem, vmem_limit_bytes=VMEM_LIMIT_BYTES)


def _dot(a, b):
    return jnp.dot(a, b, preferred_element_type=F32)


def _dot_nt(a, b):
    return lax.dot_general(a, b, (((1,), (1,)), ((), ())), preferred_element_type=F32)


def _dot_tn(a, b):
    return lax.dot_general(a, b, (((0,), (0,)), ((), ())), preferred_element_type=F32)


def _split3(x):
    hi = x.astype(BF16)
    r1 = x - hi.astype(F32)
    mid = r1.astype(BF16)
    lo = (r1 - mid.astype(F32)).astype(BF16)
    return hi, mid, lo


def _dot3(x, m):
    hi, mid, lo = _split3(x)
    return _dot(hi, m) + _dot(mid, m) + _dot(lo, m)


def _dot3_left(m, x):
    w = x.shape[1]
    z = _dot(m, jnp.concatenate(_split3(x), axis=1))
    return z[:, 0:w] + z[:, w:2 * w] + z[:, 2 * w:3 * w]


def _sigmoid(x):
    return 1.0 / (1.0 + jnp.exp(-x))


def _norm_mod(x, g, shift, scale):
    ms = jnp.mean(x * x, axis=-1, keepdims=True)
    y = x * lax.rsqrt(ms + RMS_EPS) * g
    return y * (1.0 + scale) + shift


def _ada_kernel(c_ref, w_ref, b_ref, o_ref):
    c = c_ref[...]
    s = c * _sigmoid(c)
    o_ref[0] = jnp.dot(s, w_ref[0], precision=HIGHEST, preferred_element_type=F32) + b_ref[0]


def _ada_all(cond, ada_w, ada_b):
    depth, d, n6 = ada_w.shape
    rows = cond.shape[0]
    nt = 512
    return pl.pallas_call(
        _ada_kernel,
        grid=(depth, n6 // nt),
        in_specs=[
            pl.BlockSpec((rows, d), lambda i, n: (0, 0)),
            pl.BlockSpec((1, d, nt), lambda i, n: (i, 0, n)),
            pl.BlockSpec((1, 1, nt), lambda i, n: (i, 0, n)),
        ],
        out_specs=pl.BlockSpec((1, rows, nt), lambda i, n: (i, 0, n)),
        out_shape=jax.ShapeDtypeStruct((depth, rows, n6), F32),
        name="ada_mod",
        compiler_params=_cparams(("arbitrary", "arbitrary")),
    )(cond, ada_w, ada_b.reshape(depth, 1, n6))


def _rwkv_proj_kernel(nl, seq, ctx_len, has_vres, *refs):
    if has_vres:
        (x_ref, xp_ref, xn_ref, mod_ref, gn_ref, mix_ref, wr_ref, wk_ref, wv_ref, g1_ref, g2_ref,
         w1_ref, w2_ref, w0_ref, a1_ref, a2_ref, a0_ref, kkw_ref, kaw_ref, hs_ref, hst_ref,
         v0_ref, v1_ref, v2_ref, vf_ref,
         r_o, v_o, g_o, kk_o, lw0_o, lw1_o, b0_o, b1_o, kd0_o, kd1_o, ext_scr, sh_scr) = refs
    else:
        (x_ref, xp_ref, xn_ref, mod_ref, gn_ref, mix_ref, wr_ref, wk_ref, wv_ref, g1_ref, g2_ref,
         w1_ref, w2_ref, w0_ref, a1_ref, a2_ref, a0_ref, kkw_ref, kaw_ref, hs_ref, hst_ref,
         r_o, v_o, g_o, kk_o, lw0_o, lw1_o, b0_o, b1_o, kd0_o, kd1_o, ext_scr, sh_scr) = refs
    tt = x_ref.shape[1]
    d = x_ref.shape[2]
    hw = GRID_W
    j = pl.program_id(1)
    shift = mod_ref[0, 0, 0:1, :]
    scale = mod_ref[0, 0, 1:2, :]
    gn = gn_ref[...]
    ext_scr[0:hw, :] = _norm_mod(xp_ref[0], gn, shift, scale)
    ext_scr[hw:hw + tt, :] = _norm_mod(x_ref[0], gn, shift, scale)
    ext_scr[hw + tt:, :] = _norm_mod(xn_ref[0], gn, shift, scale)

    i = lax.broadcasted_iota(I32, (tt, 1), 0)
    q = d // 4

    @pl.when(j < nl)
    def _latent_shift():
        t = j * tt + i
        col = i % hw
        sh_scr[:, 0:q] = jnp.where(col != 0, ext_scr[hw - 1:hw - 1 + tt, 0:q], 0.0)
        sh_scr[:, q:2 * q] = jnp.where(col != hw - 1, ext_scr[hw + 1:hw + 1 + tt, q:2 * q], 0.0)
        sh_scr[:, 2 * q:3 * q] = jnp.where(t >= hw, ext_scr[0:tt, 2 * q:3 * q], 0.0)
        sh_scr[:, 3 * q:] = jnp.where(t < seq - hw, ext_scr[2 * hw:2 * hw + tt, 3 * q:], 0.0)

    @pl.when(j >= nl)
    def _context_shift():
        t = (j - nl) * tt + i
        hd = d // 2
        sh_scr[:, 0:hd] = jnp.where(t != 0, ext_scr[hw - 1:hw - 1 + tt, 0:hd], 0.0)
        sh_scr[:, hd:] = jnp.where(t != ctx_len - 1, ext_scr[hw + 1:hw + 1 + tt, hd:], 0.0)

    h = ext_scr[hw:hw + tt, :]
    xx = sh_scr[...] - h

    def mixed(n):
        return (h + xx * mix_ref[n:n + 1, :]).astype(BF16)

    xr, xw, xk, xv, xa, xg = [mixed(n) for n in range(6)]
    r = _dot(xr, wr_ref[...])
    k = _dot(xk, wk_ref[...])
    v = _dot(xv, wv_ref[...])
    if has_vres:
        lor = _dot(_dot(xv, v1_ref[...]).astype(BF16), v2_ref[...])
        v = v + (vf_ref[0] - v) * _sigmoid(v0_ref[...] + lor)
    g = _dot(_sigmoid(_dot(xg, g1_ref[...])).astype(BF16), g2_ref[...])
    r_o[0] = r
    v_o[0] = v
    g_o[0] = g

    kkr = k * kkw_ref[...]
    ss = _dot3(kkr * kkr, hs_ref[...])
    inv = 1.0 / jnp.maximum(jnp.sqrt(ss), 1e-12)
    kk = kkr * _dot3(inv, hst_ref[...])
    kk_o[0] = kk
    kaw = kaw_ref[...]
    for dr, (lw_o, b_o, kd_o) in enumerate(((lw0_o, b0_o, kd0_o), (lw1_o, b1_o, kd1_o))):
        wpre = w0_ref[dr:dr + 1, :] + _dot(jnp.tanh(_dot(xw, w1_ref[dr])).astype(BF16), w2_ref[dr])
        lw_o[0] = -EXP_NEG_HALF * _sigmoid(wpre)
        a = _sigmoid(a0_ref[dr:dr + 1, :] + _dot(_dot(xa, a1_ref[dr]).astype(BF16), a2_ref[dr]))
        b_o[0] = kk * a
        kd_o[0] = k * (1.0 + (a - 1.0) * kaw)


def _rwkv_proj(xc, mod, gn, p, vres, v_first, nl, seq, ctx_len):
    bsz, tc, d = xc.shape
    tt = TOKEN_TILE
    nt = tc // tt
    hb = tt // GRID_W
    nhb = tc // GRID_W
    has_vres = vres is not None

    def full(a):
        nd = a.ndim
        return pl.BlockSpec(a.shape, lambda b, j, _n=nd: (0,) * _n)

    tile = pl.BlockSpec((1, tt, d), lambda b, j: (b, j, 0))
    ins = [xc, xc, xc, mod, gn, p['mix'], p['wr'], p['wk'], p['wv'], p['g1'], p['g2'],
           p['w1'], p['w2'], p['w0'], p['a1'], p['a2'], p['a0'], p['k_k'], p['k_a'], p['hs'], p['hst']]
    specs = [
        tile,
        pl.BlockSpec((1, GRID_W, d), lambda b, j: (b, jnp.maximum(j * hb - 1, 0), 0)),
        pl.BlockSpec((1, GRID_W, d), lambda b, j: (b, jnp.minimum((j + 1) * hb, nhb - 1), 0)),
        pl.BlockSpec((1, 1, 6, d), lambda b, j: (b, (j >= nl).astype(I32), 0, 0)),
    ] + [full(a) for a in ins[4:]]
    if has_vres:
        ins += [vres[0], vres[1], vres[2], v_first]
        specs += [full(vres[0]), full(vres[1]), full(vres[2]), tile]
    out_sds = jax.ShapeDtypeStruct((bsz, tc, d), F32)
    return pl.pallas_call(
        functools.partial(_rwkv_proj_kernel, nl, seq, ctx_len, has_vres),
        grid=(bsz, nt),
        in_specs=specs,
        out_specs=[tile] * 10,
        out_shape=[out_sds] * 10,
        scratch_shapes=[pltpu.VMEM((tt + 2 * GRID_W, d), F32), pltpu.VMEM((tt, d), F32)],
        name="rwkv_proj",
        compiler_params=_cparams(("parallel", "arbitrary")),
    )(*ins)


def _wkv_chains(chains, masks):
    nc = len(chains)
    ks = range(nc)
    L = chains[0][0].shape[0]
    r, v, kk, lw, b, kd, s_prev, rev = [[ch[i] for ch in chains] for i in range(8)]
    mk = [masks[1] if rv else masks[0] for rv in rev]
    tri, m0, m1, strict, incl, eye2, bd = [[m[i] for m in mk] for i in range(7)]

    def stack(k, x):
        xb = x.astype(BF16)
        return jnp.concatenate([xb * m0[k], xb * m1[k]], axis=0)

    c = [_dot3_left(tri[k], lw[k]) for k in ks]
    ctot = [c[k][0:1, :] if rev[k] else c[k][L - 1:L, :] for k in ks]
    e_c = [jnp.exp(c[k]) for k in ks]
    e_nc = [jnp.exp(-c[k]) for k in ks]
    e_tc = [jnp.exp(ctot[k] - c[k]) for k in ks]
    ah = [-kk[k] * jnp.exp(c[k] - lw[k]) for k in ks]
    rh = [r[k] * e_c[k] for k in ks]
    lhs = [jnp.concatenate([ah[k], rh[k]], axis=0).astype(BF16) for k in ks]
    rhs = [jnp.concatenate([stack(k, b[k] * e_nc[k]), stack(k, kd[k] * e_nc[k])], axis=0) for k in ks]
    aa = [_dot_nt(lhs[k], rhs[k]) for k in ks]
    a_ab = [jnp.where(strict[k], aa[k][0:L, 0:2 * L], 0.0) for k in ks]
    a_ak = [jnp.where(strict[k], aa[k][0:L, 2 * L:4 * L], 0.0).astype(BF16) for k in ks]
    a_r = [jnp.where(jnp.concatenate([incl[k], incl[k]], axis=1), aa[k][L:2 * L, :], 0.0).astype(BF16)
           for k in ks]

    n_dbl = int(math.log2(L))
    tm = [eye2[k] + a_ab[k] for k in ks]
    pw = [_dot(a_ab[k].astype(BF16), stack(k, a_ab[k])) for k in ks]
    for _ in range(n_dbl - 2):
        z = [_dot(pw[k].astype(BF16), jnp.concatenate([stack(k, tm[k]), stack(k, pw[k])], axis=1)) for k in ks]
        tm = [tm[k] + z[k][:, 0:2 * L] for k in ks]
        pw = [z[k][:, 2 * L:4 * L] for k in ks]
    tm = [tm[k] + _dot(pw[k].astype(BF16), stack(k, tm[k])) for k in ks]

    ss0 = [_dot_nt(lhs[k], s_prev[k].astype(BF16)) for k in ks]
    vs = [stack(k, v[k]) for k in ks]
    wmat = [ss0[k][0:L] + _dot(a_ak[k], vs[k]) for k in ks]
    u = [_dot(tm[k].astype(BF16), stack(k, wmat[k])) for k in ks]
    y = [ss0[k][L:2 * L] + _dot(a_r[k], jnp.concatenate([stack(k, u[k]), vs[k]], axis=0)) for k in ks]
    uv = [jnp.concatenate([u[k], v[k]], axis=0).astype(BF16) for k in ks]
    bk = [jnp.concatenate([b[k] * e_tc[k], kd[k] * e_tc[k]], axis=0).astype(BF16) for k in ks]
    upd = [_dot_tn(uv[k], bk[k]) for k in ks]
    s_new = [s_prev[k] * jnp.exp(ctot[k]) + jnp.where(bd[k], upd[k], 0.0) for k in ks]
    return y, s_new


def _wkv_masks(L, w2, reverse):
    n = w2 // 2
    row = lax.broadcasted_iota(I32, (L, L), 0)
    colm = lax.broadcasted_iota(I32, (L, L), 1)
    tri = jnp.where((colm >= row) if reverse else (colm <= row), 1.0, 0.0).astype(BF16)
    lane = lax.broadcasted_iota(I32, (1, w2), 1)
    m0 = jnp.where(lane < n, 1.0, 0.0).astype(BF16)
    m1 = jnp.where(lane < n, 0.0, 1.0).astype(BF16)
    t_i = lax.broadcasted_iota(I32, (L, 2 * L), 0)
    s_i = lax.broadcasted_iota(I32, (L, 2 * L), 1) % L
    strict = (s_i > t_i) if reverse else (s_i < t_i)
    incl = (s_i >= t_i) if reverse else (s_i <= t_i)
    eye2 = jnp.where(s_i == t_i, 1.0, 0.0)
    ri = lax.broadcasted_iota(I32, (w2, w2), 0)
    ci = lax.broadcasted_iota(I32, (w2, w2), 1)
    bd = (ri < n) == (ci < n)
    return tri, m0, m1, strict, incl, eye2, bd


def _wkv_kernel(rf, vf, kkf, lwf, bf, kdf, rr, vr, kkr, lwr, br, kdr, yf_o, yr_o, s_scr):
    @pl.when(pl.program_id(2) == 0)
    def _init():
        s_scr[...] = jnp.zeros_like(s_scr)

    L = rf.shape[1]
    masks = (_wkv_masks(L, LANES, False), _wkv_masks(L, LANES, True))
    chains = []
    for hp in range(rf.shape[2] // LANES):
        ls = slice(hp * LANES, (hp + 1) * LANES)
        chains.append((rf[0, :, ls], vf[0, :, ls], kkf[0, :, ls], lwf[0, :, ls], bf[0, :, ls],
                       kdf[0, :, ls], s_scr[0, hp], False))
        chains.append((rr[0, :, ls], vr[0, :, ls], kkr[0, :, ls], lwr[0, :, ls], br[0, :, ls],
                       kdr[0, :, ls], s_scr[1, hp], True))
    ys, ss = _wkv_chains(chains, masks)
    for hp in range(rf.shape[2] // LANES):
        ls = slice(hp * LANES, (hp + 1) * LANES)
        yf_o[0, :, ls] = ys[2 * hp]
        yr_o[0, :, ls] = ys[2 * hp + 1]
        s_scr[0, hp] = ss[2 * hp]
        s_scr[1, hp] = ss[2 * hp + 1]


def _wkv(r, v, kk, lw0, b0, kd0, lw1, b1, kd1, seq, ctx_len):
    bsz, tc, d = r.shape
    L = WKV_CHUNK
    nlc = seq // L
    ncc = ctx_len // L
    nch = nlc + ncc
    gp = min(WKV_PAIRS_PER_STEP, d // LANES)
    width = gp * LANES

    def fwd_map(b, h, c):
        return (b, jnp.where(c < ncc, nlc + c, c - ncc), h)

    def rev_map(b, h, c):
        return (b, jnp.where(c < ncc, nlc + ncc - 1 - c, nlc - 1 - (c - ncc)), h)

    fs = pl.BlockSpec((1, L, width), fwd_map)
    rs = pl.BlockSpec((1, L, width), rev_map)
    sds = jax.ShapeDtypeStruct((bsz, tc, d), F32)
    return pl.pallas_call(
        _wkv_kernel,
        grid=(bsz, d // width, nch),
        in_specs=[fs] * 6 + [rs] * 6,
        out_specs=[fs, rs],
        out_shape=[sds, sds],
        scratch_shapes=[pltpu.VMEM((2, gp, LANES, LANES), F32)],
        name="wkv_scan",
        compiler_params=_cparams(("parallel", "parallel", "arbitrary")),
    )(r, v, kk, lw0, b0, kd0, r, v, kk, lw1, b1, kd1)


def _rwkv_out_kernel(head, x_ref, yf_ref, yr_ref, r_ref, v_ref, g_ref, kd0_ref, kd1_ref, mod_ref,
                     rk_ref, lnw_ref, lnb_ref, wo_ref, hs_ref, hst_ref, o_ref):
    hs = hs_ref[...]
    hst = hst_ref[...]
    o = yf_ref[0] + yr_ref[0]
    inv_n = 1.0 / head
    mu = _dot3(_dot3(o, hs) * inv_n, hst)
    dlt = o - mu
    var = _dot3(dlt * dlt, hs) * inv_n
    on = dlt * _dot3(lax.rsqrt(var + GN_EPS), hst)
    on = on * lnw_ref[...] + lnb_ref[...]
    kb = 0.5 * kd0_ref[0] + 0.5 * kd1_ref[0]
    bonus = _dot3(_dot3(r_ref[0] * kb * rk_ref[...], hs), hst) * v_ref[0]
    y = ((on + bonus) * g_ref[0]).astype(BF16)
    gate = mod_ref[0, 0, 2:3, :]
    o_ref[0] = x_ref[0] + gate * _dot(y, wo_ref[...])


def _rwkv_out(xc, yf, yr, r, v, g, kd0, kd1, mod, p, nl, head):
    bsz, tc, d = xc.shape
    tt = TOKEN_TILE
    tile = pl.BlockSpec((1, tt, d), lambda b, j: (b, j, 0))

    def full(a):
        nd = a.ndim
        return pl.BlockSpec(a.shape, lambda b, j, _n=nd: (0,) * _n)

    consts = [p['r_k'], p['ln_w'], p['ln_b'], p['wo'], p['hs'], p['hst']]
    return pl.pallas_call(
        functools.partial(_rwkv_out_kernel, head),
        grid=(bsz, tc // tt),
        in_specs=[tile] * 8 + [pl.BlockSpec((1, 1, 6, d), lambda b, j: (b, (j >= nl).astype(I32), 0, 0))]
        + [full(a) for a in consts],
        out_specs=tile,
        out_shape=jax.ShapeDtypeStruct((bsz, tc, d), F32),
        input_output_aliases={0: 0},
        name="rwkv_out",
        compiler_params=_cparams(("parallel", "arbitrary")),
    )(xc, yf, yr, r, v, g, kd0, kd1, mod, *consts)


def _pool_kernel(nl, seq, ctx_len, x_ref, xp_ref, xn_ref, mod_ref, gn_ref, w_ref, sc_ref, o_ref, ext_scr):
    tt = x_ref.shape[1]
    d = x_ref.shape[2]
    ph = POOL_HALO
    j = pl.program_id(1)
    nct = ctx_len // tt
    shift = mod_ref[0, 0, 0:1, :]
    scale = mod_ref[0, 0, 1:2, :]
    gn = gn_ref[...]
    first = jnp.logical_or(j == 0, j == nl)
    last = jnp.logical_or(j == nl - 1, j == nl + nct - 1)
    hp = _norm_mod(xp_ref[0], gn, shift, scale)
    hn = _norm_mod(xn_ref[0], gn, shift, scale)
    ext_scr[0:ph, :] = jnp.where(first, 0.0, hp)
    ext_scr[ph:ph + tt, :] = _norm_mod(x_ref[0], gn, shift, scale)
    ext_scr[ph + tt:, :] = jnp.where(last, 0.0, hn)

    i = lax.broadcasted_iota(I32, (tt, 1), 0)
    t = jnp.where(j < nl, j * tt + i, (j - nl) * tt + i)
    tseg = jnp.where(j < nl, seq, ctx_len)
    ng = len(POOL_WINDOWS)
    dg = d // ng
    gate = mod_ref[0, 0, 2:3, :]
    for gi, win in enumerate(POOL_WINDOWS):
        half = win // 2
        cs = slice(gi * dg, (gi + 1) * dg)
        acc = ext_scr[ph - half:ph - half + tt, cs]
        for o in range(1, win):
            acc = acc + ext_scr[ph - half + o:ph - half + o + tt, cs]
        cnt = (jnp.minimum(t + half, tseg) - jnp.maximum(t - half, 0)).astype(F32)
        hg = ext_scr[ph:ph + tt, cs]
        dlt = (acc / cnt - hg).astype(BF16)
        y = _dot(dlt, w_ref[gi]) * sc_ref[:, cs]
        o_ref[0, :, cs] = x_ref[0, :, cs] + gate[:, cs] * y


def _pool(xc, mod, gn, w, sc, nl, seq, ctx_len, n_tiles):
    bsz, tc, d = xc.shape
    tt = TOKEN_TILE
    hb = tt // POOL_HALO
    nhb = tc // POOL_HALO
    tile = pl.BlockSpec((1, tt, d), lambda b, j: (b, j, 0))
    return pl.pallas_call(
        functools.partial(_pool_kernel, nl, seq, ctx_len),
        grid=(bsz, n_tiles),
        in_specs=[
            tile,
            pl.BlockSpec((1, POOL_HALO, d), lambda b, j: (b, jnp.maximum(j * hb - 1, 0), 0)),
            pl.BlockSpec((1, POOL_HALO, d), lambda b, j: (b, jnp.minimum((j + 1) * hb, nhb - 1), 0)),
            pl.BlockSpec((1, 1, 6, d), lambda b, j: (b, (j >= nl).astype(I32), 0, 0)),
            pl.BlockSpec(gn.shape, lambda b, j: (0, 0)),
            pl.BlockSpec(w.shape, lambda b, j: (0, 0, 0)),
            pl.BlockSpec(sc.shape, lambda b, j: (0, 0)),
        ],
        out_specs=tile,
        out_shape=jax.ShapeDtypeStruct((bsz, n_tiles * tt, d), F32),
        scratch_shapes=[pltpu.VMEM((tt + 2 * POOL_HALO, d), F32)],
        name="pool_mix",
        compiler_params=_cparams(("parallel", "arbitrary")),
    )(xc, xc, xc, mod, gn, w, sc)


def _router_kernel(x_ref, mod_ref, gn_ref, rt_ref, h_o, aff_o):
    h = _norm_mod(x_ref[0], gn_ref[...], mod_ref[0, 0, 3:4, :], mod_ref[0, 0, 4:5, :])
    h_o[0] = h.astype(BF16)
    logits = lax.dot_general(rt_ref[...], h, (((1,), (1,)), ((), ())),
                             precision=HIGHEST, preferred_element_type=F32)
    m = jnp.max(logits, axis=0, keepdims=True)
    ex = jnp.exp(logits - m)
    aff_o[0] = ex / jnp.sum(ex, axis=0, keepdims=True)


def _router(xc, mod, gn, router_t, nl, n_tiles):
    bsz, tc, d = xc.shape
    tt = TOKEN_TILE
    e = router_t.shape[0]
    tile = pl.BlockSpec((1, tt, d), lambda b, j: (b, j, 0))
    return pl.pallas_call(
        _router_kernel,
        grid=(bsz, n_tiles),
        in_specs=[
            tile,
            pl.BlockSpec((1, 1, 6, d), lambda b, j: (b, (j >= nl).astype(I32), 0, 0)),
            pl.BlockSpec(gn.shape, lambda b, j: (0, 0)),
            pl.BlockSpec(router_t.shape, lambda b, j: (0, 0)),
        ],
        out_specs=[tile, pl.BlockSpec((1, e, tt), lambda b, j: (b, 0, j))],
        out_shape=[jax.ShapeDtypeStruct((bsz, n_tiles * tt, d), BF16),
                   jax.ShapeDtypeStruct((bsz, e, n_tiles * tt), F32)],
        name="moe_router",
        compiler_params=_cparams(("parallel", "arbitrary")),
    )(xc, mod, gn, router_t)


def _lane_cumsum_excl(x, utri):
    e, t = x.shape
    off = jnp.zeros((e, 1), F32)
    parts = []
    for kb in range(t // LANES):
        blk = x[:, kb * LANES:(kb + 1) * LANES]
        inc = _dot(blk.astype(BF16), utri)
        parts.append(inc - blk + off)
        off = off + inc[:, LANES - 1:LANES]
    return jnp.concatenate(parts, axis=1), off


def _select_kernel(cap, slot_stride, aff_ref, pos_o, post_o, gate_o):
    a = aff_ref[0]
    e, t = a.shape
    bits = pltpu.bitcast(a, I32)

    def body(it, thr):
        cand = thr | lax.shift_left(jnp.int32(1), jnp.int32(29) - it)
        cnt = jnp.sum(jnp.where(bits >= cand, 1.0, 0.0), axis=1, keepdims=True)
        return jnp.where(cnt >= cap, cand, thr)

    thr = lax.fori_loop(0, 30, body, jnp.zeros((e, 1), I32))
    gt = jnp.where(bits > thr, 1.0, 0.0)
    eq = jnp.where(bits == thr, 1.0, 0.0)
    ri = lax.broadcasted_iota(I32, (LANES, LANES), 0)
    ci = lax.broadcasted_iota(I32, (LANES, LANES), 1)
    utri = jnp.where(ri <= ci, 1.0, 0.0).astype(BF16)
    n_gt = jnp.sum(gt, axis=1, keepdims=True)
    eq_rank, _ = _lane_cumsum_excl(eq, utri)
    sel = gt + eq * jnp.where(eq_rank < cap - n_gt, 1.0, 0.0)
    pos, _ = _lane_cumsum_excl(sel, utri)
    base = pl.program_id(0) * slot_stride
    posm = jnp.where(sel > 0.5, pos.astype(I32) + base, -1)
    pos_o[0] = posm
    post_o[0] = jnp.transpose(posm.astype(F32)).astype(I32)
    gate_o[0] = jnp.where(sel > 0.5, a, 0.0)


def _select(aff, cap, slot_stride):
    bsz, e, t = aff.shape
    blk = pl.BlockSpec((1, e, t), lambda b: (b, 0, 0))
    return pl.pallas_call(
        functools.partial(_select_kernel, cap, slot_stride),
        grid=(bsz,),
        in_specs=[blk],
        out_specs=[blk, pl.BlockSpec((1, t, e), lambda b: (b, 0, 0)), blk],
        out_shape=[jax.ShapeDtypeStruct((bsz, e, t), I32),
                   jax.ShapeDtypeStruct((bsz, t, e), I32),
                   jax.ShapeDtypeStruct((bsz, e, t), F32)],
        name="moe_select",
        compiler_params=_cparams(("parallel",)),
    )(aff)


def _ffn_kernel(cap, tk, fk, h_ref, pos_ref, gate_ref, w1_ref, w3_ref, w2_ref, ys_o):
    t = h_ref.shape[1]
    d = h_ref.shape[2]
    f = w1_ref.shape[2]
    pos = pos_ref[0, 0]
    g_hi, g_mid, g_lo = _split3(gate_ref[0, 0])
    prow = lax.broadcasted_iota(I32, (SUBLANES, t), 0)
    g8 = jnp.where(prow == 0, g_hi.astype(F32),
                   jnp.where(prow == 1, g_mid.astype(F32),
                             jnp.where(prow == 2, g_lo.astype(F32), 0.0))).astype(BF16)
    slot = lax.broadcasted_iota(I32, (cap, tk), 0)
    xin = jnp.zeros((cap, d), F32)
    gsl = jnp.zeros((cap, SUBLANES), F32)
    for kt in range(t // tk):
        ts = slice(kt * tk, (kt + 1) * tk)
        onehot = jnp.where(pos[:, ts] == slot, 1.0, 0.0).astype(BF16)
        xin = xin + _dot(onehot, h_ref[0, ts, :])
        gsl = gsl + _dot_nt(onehot, g8[:, ts])
    gate_slot = jnp.sum(gsl, axis=1, keepdims=True)
    xb = xin.astype(BF16)
    y = jnp.zeros((cap, d), F32)
    for kf in range(f // fk):
        fs = slice(kf * fk, (kf + 1) * fk)
        h1 = _dot(xb, w1_ref[0, :, fs])
        h3 = _dot(xb, w3_ref[0, :, fs])
        hid = (h1 * _sigmoid(h1) * h3).astype(BF16)
        y = y + _dot(hid, w2_ref[0, fs, :])
    ys_o[0, 0] = (y * gate_slot).astype(BF16)


def _ffn(h, posm, gate, w1, w3, w2, cap):
    gsz, t, d = h.shape
    e, _, f = w1.shape
    tk = min(t, 1024)
    fk = min(f, 512)
    return pl.pallas_call(
        functools.partial(_ffn_kernel, cap, tk, fk),
        grid=(e, gsz),
        in_specs=[
            pl.BlockSpec((1, t, d), lambda ei, gi: (gi, 0, 0)),
            pl.BlockSpec((1, 1, 1, t), lambda ei, gi: (gi, ei, 0, 0)),
            pl.BlockSpec((1, 1, 1, t), lambda ei, gi: (gi, ei, 0, 0)),
            pl.BlockSpec((1, d, f), lambda ei, gi: (ei, 0, 0)),
            pl.BlockSpec((1, d, f), lambda ei, gi: (ei, 0, 0)),
            pl.BlockSpec((1, f, d), lambda ei, gi: (ei, 0, 0)),
        ],
        out_specs=pl.BlockSpec((1, 1, cap, d), lambda ei, gi: (gi, ei, 0, 0)),
        out_shape=jax.ShapeDtypeStruct((gsz, e, cap, d), BF16),
        name="moe_ffn",
        compiler_params=_cparams(("arbitrary", "arbitrary")),
    )(h, posm, gate, w1, w3, w2)


def _combine_kernel(win, lo_ref, x_ref, post_ref, ys_ref, mod_ref, o_ref):
    tt = x_ref.shape[1]
    e = ys_ref.shape[1]
    cap = ys_ref.shape[2]
    base = (pl.program_id(0) * pl.num_programs(1) + pl.program_id(1)) * e
    slot = lax.broadcasted_iota(I32, (tt, win), 1)
    acc = jnp.zeros(x_ref.shape[1:], F32)
    for ei in range(e):
        lo = lo_ref[base + ei]
        start = pl.multiple_of(jnp.minimum((lo // BF16_ROWS) * BF16_ROWS, cap - win), BF16_ROWS)
        onehot = jnp.where(post_ref[0, :, ei:ei + 1] - start == slot, 1.0, 0.0).astype(BF16)
        acc = acc + _dot(onehot, ys_ref[0, ei, pl.ds(start, win), :])
    o_ref[0] = x_ref[0] + mod_ref[0, 0, 5:6, :] * acc


def _combine(xc, post, ys, mod, tile0, n_tiles, shared_slots):
    bsz, tc, d = xc.shape
    tt = TOKEN_TILE
    _, e, cap, _ = ys.shape
    is_ctx = 1 if tile0 > 0 else 0
    ys_map = (lambda b, j, lo: (0, 0, 0, 0)) if shared_slots else (lambda b, j, lo: (b, 0, 0, 0))
    win = min(cap, tt + BF16_ROWS)
    first = jnp.min(jnp.where(post >= 0, post, cap).reshape(bsz, n_tiles, tt, e), axis=2)
    first = jnp.where(first >= cap, 0, first).reshape(-1)
    return pl.pallas_call(
        functools.partial(_combine_kernel, win),
        grid_spec=pltpu.PrefetchScalarGridSpec(
            num_scalar_prefetch=1,
            grid=(bsz, n_tiles),
            in_specs=[
                pl.BlockSpec((1, tt, d), lambda b, j, lo: (b, tile0 + j, 0)),
                pl.BlockSpec((1, tt, e), lambda b, j, lo: (b, j, 0)),
                pl.BlockSpec((1, e, cap, d), ys_map),
                pl.BlockSpec((1, 1, 6, d), lambda b, j, lo: (b, is_ctx, 0, 0)),
            ],
            out_specs=pl.BlockSpec((1, tt, d), lambda b, j, lo: (b, tile0 + j, 0)),
        ),
        out_shape=jax.ShapeDtypeStruct((bsz, tc, d), F32),
        input_output_aliases={1: 0},
        name="moe_combine",
        compiler_params=_cparams(("parallel", "arbitrary")),
    )(first, xc, post, ys, mod)


def _moe(xc, mod, gn, router_t, w1, w3, w2, nl, nct, seq, ctx_len, do_ctx):
    bsz, tc, d = xc.shape
    e = router_t.shape[0]
    n_tiles = nl + nct if do_ctx else nl
    h, aff = _router(xc, mod, gn, router_t, nl, n_tiles)
    cap_l = CAPACITY_FACTOR * seq // e
    pos, post, gate = _select(aff[:, :, :seq], cap_l, 0)
    ys = _ffn(h[:, :seq, :], pos.reshape(bsz, e, 1, seq), gate.reshape(bsz, e, 1, seq), w1, w3, w2, cap_l)
    xc = _combine(xc, post, ys, mod, 0, nl, False)
    if do_ctx:
        cap_c = CAPACITY_FACTOR * ctx_len // e
        pos, post, gate = _select(aff[:, :, seq:], cap_c, cap_c)
        pos = jnp.transpose(pos, (1, 0, 2)).reshape(1, e, 1, bsz * ctx_len)
        gate = jnp.transpose(gate, (1, 0, 2)).reshape(1, e, 1, bsz * ctx_len)
        h_ctx = h[:, seq:, :].reshape(1, bsz * ctx_len, d)
        ys = _ffn(h_ctx, pos, gate, w1, w3, w2, bsz * cap_c)
        xc = _combine(xc, post, ys, mod, nl, nct, True)
    return xc


def _final_kernel(x_ref, g_ref, o_ref):
    x = x_ref[0]
    ms = jnp.mean(x * x, axis=-1, keepdims=True)
    o_ref[0] = x * lax.rsqrt(ms + RMS_EPS) * g_ref[...]


def _final_norm(xc, g, seq):
    bsz, tc, d = xc.shape
    tt = TOKEN_TILE
    tile = pl.BlockSpec((1, tt, d), lambda b, j: (b, j, 0))
    return pl.pallas_call(
        _final_kernel,
        grid=(bsz, seq // tt),
        in_specs=[tile, pl.BlockSpec(g.shape, lambda b, j: (0, 0))],
        out_specs=tile,
        out_shape=jax.ShapeDtypeStruct((bsz, seq, d), F32),
        name="final_norm",
        compiler_params=_cparams(("parallel", "arbitrary")),
    )(xc, g)


def kernel(x, c, ctx, c_ctx, ada_w, ada_b, norm1_g, norm2_g, rwkv_mix, rwkv_wrkv, rwkv_w0, rwkv_w1, rwkv_w2, rwkv_a0, rwkv_a1, rwkv_a2, rwkv_v0, rwkv_v1, rwkv_v2, rwkv_g1, rwkv_g2, rwkv_kk, rwkv_ka, rwkv_rk, rwkv_lnw, rwkv_lnb, rwkv_wo, pool_w, pool_scale, moe_router, moe_w1, moe_w3, moe_w2, final_g):
    bsz, seq, d = x.shape
    ctx_len = ctx.shape[1]
    depth = ada_w.shape[0]
    n_heads, head = rwkv_rk.shape[1], rwkv_rk.shape[2]
    n_mixers = 2
    tt = TOKEN_TILE
    assert seq % tt == 0 and ctx_len % tt == 0 and tt % GRID_W == 0
    assert head == WKV_CHUNK and 2 * head == LANES and d % LANES == 0
    nl, nct = seq // tt, ctx_len // tt

    xc = jnp.concatenate([x, ctx], axis=1)

    rows = -(-(bsz + 1) // SUBLANES) * SUBLANES
    cond = jnp.zeros((rows, d), F32).at[:bsz].set(c).at[bsz].set(c_ctx)
    m_all = _ada_all(cond, ada_w, ada_b)
    m_lat = m_all[:, :bsz].reshape(depth, bsz, 1, 6, d)
    m_ctx = jnp.broadcast_to(m_all[:, bsz].reshape(depth, 1, 1, 6, d), (depth, bsz, 1, 6, d))
    mods = jnp.concatenate([m_lat, m_ctx], axis=2)

    head_of = jnp.arange(d) // head
    hs = (head_of[:, None] == jnp.arange(HEAD_COLS)[None, :]).astype(BF16)
    hst = jnp.transpose(hs)

    v_first = None
    for i in range(depth):
        last = i == depth - 1
        is_rwkv = i % n_mixers == 0
        jn = i // n_mixers
        mod = mods[i]
        gn1 = norm1_g[i].reshape(1, d)
        if is_rwkv:
            p = {
                'mix': rwkv_mix[jn],
                'wr': rwkv_wrkv[jn, 0].astype(BF16), 'wk': rwkv_wrkv[jn, 1].astype(BF16),
                'wv': rwkv_wrkv[jn, 2].astype(BF16),
                'g1': rwkv_g1[jn].astype(BF16), 'g2': rwkv_g2[jn].astype(BF16),
                'w1': rwkv_w1[jn].astype(BF16), 'w2': rwkv_w2[jn].astype(BF16), 'w0': rwkv_w0[jn],
                'a1': rwkv_a1[jn].astype(BF16), 'a2': rwkv_a2[jn].astype(BF16), 'a0': rwkv_a0[jn],
                'k_k': rwkv_kk[jn].reshape(1, d), 'k_a': rwkv_ka[jn].reshape(1, d),
                'r_k': rwkv_rk[jn].reshape(1, d), 'ln_w': rwkv_lnw[jn].reshape(1, d),
                'ln_b': rwkv_lnb[jn].reshape(1, d), 'wo': rwkv_wo[jn].astype(BF16),
                'hs': hs, 'hst': hst,
            }
            vres = None if jn == 0 else (rwkv_v0[jn - 1].reshape(1, d), rwkv_v1[jn - 1].astype(BF16),
                                         rwkv_v2[jn - 1].astype(BF16))
            r, v, g, kk, lw0, lw1, b0, b1, kd0, kd1 = _rwkv_proj(xc, mod, gn1, p, vres, v_first, nl, seq, ctx_len)
            if v_first is None:
                v_first = v
            yf, yr = _wkv(r, v, kk, lw0, b0, kd0, lw1, b1, kd1, seq, ctx_len)
            xc = _rwkv_out(xc, yf, yr, r, v, g, kd0, kd1, mod, p, nl, head)
        else:
            n_tiles = nl if last else nl + nct
            xc = _pool(xc, mod, gn1, pool_w[jn].astype(BF16), pool_scale[jn].reshape(1, d),
                       nl, seq, ctx_len, n_tiles)
        xc = _moe(xc, mod, norm2_g[i].reshape(1, d), jnp.transpose(moe_router[i]),
                  moe_w1[i].astype(BF16), moe_w3[i].astype(BF16), moe_w2[i].astype(BF16),
                  nl, nct, seq, ctx_len, not last)
    return _final_norm(xc, final_g.reshape(1, d), seq)
```

```python
import functools
import math

import jax
import jax.numpy as jnp
from jax import lax
from jax.experimental import pallas as pl
from jax.experimental.pallas import tpu as pltpu

F32 = jnp.float32
BF16 = jnp.bfloat16
I32 = jnp.int32
HIGHEST = lax.Precision.HIGHEST

GRID_W = 64
POOL_WINDOWS = (2, 4, 8, 16)
CAPACITY_FACTOR = 2
RMS_EPS = 1e-6
GN_EPS = 64e-5
EXP_NEG_HALF = math.exp(-0.5)

LANES = 128
SUBLANES = 8
BF16_ROWS = 16
VMEM_LIMIT_BYTES = 56 * 1024 * 1024

TOKEN_TILE = 256
GATHER_TILE = 256
WKV_CHUNK = 64
WKV_PAIRS_PER_STEP = 8
POOL_HALO = 8
HEAD_COLS = LANES


def _cparams(sem):
    return pltpu.CompilerParams(dimension_semantics=sem, vmem_limit_bytes=VMEM_LIMIT_BYTES)


def _dot(a, b):
    return jnp.dot(a, b, preferred_element_type=F32)


def _dot_nt(a, b):
    return lax.dot_general(a, b, (((1,), (1,)), ((), ())), preferred_element_type=F32)


def _dot_tn(a, b):
    return lax.dot_general(a, b, (((0,), (0,)), ((), ())), preferred_element_type=F32)


def _split3(x):
    hi = x.astype(BF16)
    r1 = x - hi.astype(F32)
    mid = r1.astype(BF16)
    lo = (r1 - mid.astype(F32)).astype(BF16)
    return hi, mid, lo


def _dot3(x, m):
    hi, mid, lo = _split3(x)
    return _dot(hi, m) + _dot(mid, m) + _dot(lo, m)


def _dot3_left(m, x):
    w = x.shape[1]
    z = _dot(m, jnp.concatenate(_split3(x), axis=1))
    return z[:, 0:w] + z[:, w:2 * w] + z[:, 2 * w:3 * w]


def _sigmoid(x):
    return 0.5 * jnp.tanh(0.5 * x) + 0.5


def _norm_mod(x, g, shift, scale):
    ms = jnp.mean(x * x, axis=-1, keepdims=True)
    y = x * lax.rsqrt(ms + RMS_EPS) * g
    return y * (1.0 + scale) + shift


def _ada_kernel(c_ref, w_ref, b_ref, o_ref):
    c = c_ref[...]
    s = c * _sigmoid(c)
    o_ref[0] = jnp.dot(s, w_ref[0], precision=HIGHEST, preferred_element_type=F32) + b_ref[0]


def _ada_all(cond, ada_w, ada_b):
    depth, d, n6 = ada_w.shape
    rows = cond.shape[0]
    nt = 512
    return pl.pallas_call(
        _ada_kernel,
        grid=(depth, n6 // nt),
        in_specs=[
            pl.BlockSpec((rows, d), lambda i, n: (0, 0)),
            pl.BlockSpec((1, d, nt), lambda i, n: (i, 0, n)),
            pl.BlockSpec((1, 1, nt), lambda i, n: (i, 0, n)),
        ],
        out_specs=pl.BlockSpec((1, rows, nt), lambda i, n: (i, 0, n)),
        out_shape=jax.ShapeDtypeStruct((depth, rows, n6), F32),
        name="ada_mod",
        compiler_params=_cparams(("arbitrary", "arbitrary")),
    )(cond, ada_w, ada_b.reshape(depth, 1, n6))


def _rwkv_proj_kernel(nl, seq, ctx_len, has_vres, *refs):
    if has_vres:
        (x_ref, xp_ref, xn_ref, mod_ref, gn_ref, mix_ref, wr_ref, wk_ref, wv_ref, g1_ref, g2_ref,
         w1_ref, w2_ref, w0_ref, a1_ref, a2_ref, a0_ref, kkw_ref, kaw_ref, hs_ref, hst_ref,
         v0_ref, v1_ref, v2_ref, vf_ref,
         r_o, v_o, g_o, kk_o, lw0_o, lw1_o, b0_o, b1_o, kd0_o, kd1_o, ext_scr, sh_scr) = refs
    else:
        (x_ref, xp_ref, xn_ref, mod_ref, gn_ref, mix_ref, wr_ref, wk_ref, wv_ref, g1_ref, g2_ref,
         w1_ref, w2_ref, w0_ref, a1_ref, a2_ref, a0_ref, kkw_ref, kaw_ref, hs_ref, hst_ref,
         r_o, v_o, g_o, kk_o, lw0_o, lw1_o, b0_o, b1_o, kd0_o, kd1_o, ext_scr, sh_scr) = refs
    tt = x_ref.shape[1]
    d = x_ref.shape[2]
    hw = GRID_W
    j = pl.program_id(1)
    shift = mod_ref[0, 0, 0:1, :]
    scale = mod_ref[0, 0, 1:2, :]
    gn = gn_ref[...]
    ext_scr[0:hw, :] = _norm_mod(xp_ref[0], gn, shift, scale)
    ext_scr[hw:hw + tt, :] = _norm_mod(x_ref[0], gn, shift, scale)
    ext_scr[hw + tt:, :] = _norm_mod(xn_ref[0], gn, shift, scale)

    i = lax.broadcasted_iota(I32, (tt, 1), 0)
    q = d // 4

    @pl.when(j < nl)
    def _latent_shift():
        t = j * tt + i
        col = i % hw
        sh_scr[:, 0:q] = jnp.where(col != 0, ext_scr[hw - 1:hw - 1 + tt, 0:q], 0.0)
        sh_scr[:, q:2 * q] = jnp.where(col != hw - 1, ext_scr[hw + 1:hw + 1 + tt, q:2 * q], 0.0)
        sh_scr[:, 2 * q:3 * q] = jnp.where(t >= hw, ext_scr[0:tt, 2 * q:3 * q], 0.0)
        sh_scr[:, 3 * q:] = jnp.where(t < seq - hw, ext_scr[2 * hw:2 * hw + tt, 3 * q:], 0.0)

    @pl.when(j >= nl)
    def _context_shift():
        t = (j - nl) * tt + i
        hd = d // 2
        sh_scr[:, 0:hd] = jnp.where(t != 0, ext_scr[hw - 1:hw - 1 + tt, 0:hd], 0.0)
        sh_scr[:, hd:] = jnp.where(t != ctx_len - 1, ext_scr[hw + 1:hw + 1 + tt, hd:], 0.0)

    h = ext_scr[hw:hw + tt, :]
    xx = sh_scr[...] - h

    def mixed(n):
        return (h + xx * mix_ref[n:n + 1, :]).astype(BF16)

    xr, xw, xk, xv, xa, xg = [mixed(n) for n in range(6)]
    r = _dot(xr, wr_ref[...])
    k = _dot(xk, wk_ref[...])
    v = _dot(xv, wv_ref[...])
    if has_vres:
        lor = _dot(_dot(xv, v1_ref[...]).astype(BF16), v2_ref[...])
        v = v + (vf_ref[0] - v) * _sigmoid(v0_ref[...] + lor)
    g = _dot(_sigmoid(_dot(xg, g1_ref[...])).astype(BF16), g2_ref[...])
    r_o[0] = r
    v_o[0] = v
    g_o[0] = g

    kkr = k * kkw_ref[...]
    ss = _dot3(kkr * kkr, hs_ref[...])
    inv = 1.0 / jnp.maximum(jnp.sqrt(ss), 1e-12)
    kk = kkr * _dot3(inv, hst_ref[...])
    kk_o[0] = kk
    kaw = kaw_ref[...]
    for dr, (lw_o, b_o, kd_o) in enumerate(((lw0_o, b0_o, kd0_o), (lw1_o, b1_o, kd1_o))):
        wpre = w0_ref[dr:dr + 1, :] + _dot(jnp.tanh(_dot(xw, w1_ref[dr])).astype(BF16), w2_ref[dr])
        lw_o[0] = -EXP_NEG_HALF * _sigmoid(wpre)
        a = _sigmoid(a0_ref[dr:dr + 1, :] + _dot(_dot(xa, a1_ref[dr]).astype(BF16), a2_ref[dr]))
        b_o[0] = kk * a
        kd_o[0] = k * (1.0 + (a - 1.0) * kaw)


def _rwkv_proj(xc, mod, gn, p, vres, v_first, nl, seq, ctx_len):
    bsz, tc, d = xc.shape
    tt = TOKEN_TILE
    nt = tc // tt
    hb = tt // GRID_W
    nhb = tc // GRID_W
    has_vres = vres is not None

    def full(a):
        nd = a.ndim
        return pl.BlockSpec(a.shape, lambda b, j, _n=nd: (0,) * _n)

    tile = pl.BlockSpec((1, tt, d), lambda b, j: (b, j, 0))
    ins = [xc, xc, xc, mod, gn, p['mix'], p['wr'], p['wk'], p['wv'], p['g1'], p['g2'],
           p['w1'], p['w2'], p['w0'], p['a1'], p['a2'], p['a0'], p['k_k'], p['k_a'], p['hs'], p['hst']]
    specs = [
        tile,
        pl.BlockSpec((1, GRID_W, d), lambda b, j: (b, jnp.maximum(j * hb - 1, 0), 0)),
        pl.BlockSpec((1, GRID_W, d), lambda b, j: (b, jnp.minimum((j + 1) * hb, nhb - 1), 0)),
        pl.BlockSpec((1, 1, 6, d), lambda b, j: (b, (j >= nl).astype(I32), 0, 0)),
    ] + [full(a) for a in ins[4:]]
    if has_vres:
        ins += [vres[0], vres[1], vres[2], v_first]
        specs += [full(vres[0]), full(vres[1]), full(vres[2]), tile]
    out_sds = jax.ShapeDtypeStruct((bsz, tc, d), F32)
    return pl.pallas_call(
        functools.partial(_rwkv_proj_kernel, nl, seq, ctx_len, has_vres),
        grid=(bsz, nt),
        in_specs=specs,
        out_specs=[tile] * 10,
        out_shape=[out_sds] * 10,
        scratch_shapes=[pltpu.VMEM((tt + 2 * GRID_W, d), F32), pltpu.VMEM((tt, d), F32)],
        name="rwkv_proj",
        compiler_params=_cparams(("parallel", "arbitrary")),
    )(*ins)


def _wkv_chains(chains, masks):
    nc = len(chains)
    ks = range(nc)
    L = chains[0][0].shape[0]
    r, v, kk, lw, b, kd, s_prev, rev = [[ch[i] for ch in chains] for i in range(8)]
    mk = [masks[1] if rv else masks[0] for rv in rev]
    tri, m0, m1, strict, incl, eye2, bd = [[m[i] for m in mk] for i in range(7)]

    def stack(k, x):
        xb = x.astype(BF16)
        return jnp.concatenate([xb * m0[k], xb * m1[k]], axis=0)

    c = [_dot3_left(tri[k], lw[k]) for k in ks]
    ctot = [c[k][0:1, :] if rev[k] else c[k][L - 1:L, :] for k in ks]
    e_c = [jnp.exp(c[k]) for k in ks]
    e_nc = [jnp.exp(-c[k]) for k in ks]
    e_tc = [jnp.exp(ctot[k] - c[k]) for k in ks]
    ah = [-kk[k] * jnp.exp(c[k] - lw[k]) for k in ks]
    rh = [r[k] * e_c[k] for k in ks]
    lhs = [jnp.concatenate([ah[k], rh[k]], axis=0).astype(BF16) for k in ks]
    rhs = [jnp.concatenate([stack(k, b[k] * e_nc[k]), stack(k, kd[k] * e_nc[k])], axis=0) for k in ks]
    aa = [_dot_nt(lhs[k], rhs[k]) for k in ks]
    a_ab = [jnp.where(strict[k], aa[k][0:L, 0:2 * L], 0.0) for k in ks]
    a_ak = [jnp.where(strict[k], aa[k][0:L, 2 * L:4 * L], 0.0).astype(BF16) for k in ks]
    a_r = [jnp.where(jnp.concatenate([incl[k], incl[k]], axis=1), aa[k][L:2 * L, :], 0.0).astype(BF16)
           for k in ks]

    n_dbl = int(math.log2(L))
    tm = [eye2[k] + a_ab[k] for k in ks]
    pw = [_dot(a_ab[k].astype(BF16), stack(k, a_ab[k])) for k in ks]
    for _ in range(n_dbl - 2):
        z = [_dot(pw[k].astype(BF16), jnp.concatenate([stack(k, tm[k]), stack(k, pw[k])], axis=1)) for k in ks]
        tm = [tm[k] + z[k][:, 0:2 * L] for k in ks]
        pw = [z[k][:, 2 * L:4 * L] for k in ks]
    tm = [tm[k] + _dot(pw[k].astype(BF16), stack(k, tm[k])) for k in ks]

    ss0 = [_dot_nt(lhs[k], s_prev[k].astype(BF16)) for k in ks]
    vs = [stack(k, v[k]) for k in ks]
    wmat = [ss0[k][0:L] + _dot(a_ak[k], vs[k]) for k in ks]
    u = [_dot(tm[k].astype(BF16), stack(k, wmat[k])) for k in ks]
    y = [ss0[k][L:2 * L] + _dot(a_r[k], jnp.concatenate([stack(k, u[k]), vs[k]], axis=0)) for k in ks]
    uv = [jnp.concatenate([u[k], v[k]], axis=0).astype(BF16) for k in ks]
    bk = [jnp.concatenate([b[k] * e_tc[k], kd[k] * e_tc[k]], axis=0).astype(BF16) for k in ks]
    upd = [_dot_tn(uv[k], bk[k]) for k in ks]
    s_new = [s_prev[k] * jnp.exp(ctot[k]) + jnp.where(bd[k], upd[k], 0.0) for k in ks]
    return y, s_new


def _wkv_masks(L, w2, reverse):
    n = w2 // 2
    row = lax.broadcasted_iota(I32, (L, L), 0)
    colm = lax.broadcasted_iota(I32, (L, L), 1)
    tri = jnp.where((colm >= row) if reverse else (colm <= row), 1.0, 0.0).astype(BF16)
    lane = lax.broadcasted_iota(I32, (1, w2), 1)
    m0 = jnp.where(lane < n, 1.0, 0.0).astype(BF16)
    m1 = jnp.where(lane < n, 0.0, 1.0).astype(BF16)
    t_i = lax.broadcasted_iota(I32, (L, 2 * L), 0)
    s_i = lax.broadcasted_iota(I32, (L, 2 * L), 1) % L
    strict = (s_i > t_i) if reverse else (s_i < t_i)
    incl = (s_i >= t_i) if reverse else (s_i <= t_i)
    eye2 = jnp.where(s_i == t_i, 1.0, 0.0)
    ri = lax.broadcasted_iota(I32, (w2, w2), 0)
    ci = lax.broadcasted_iota(I32, (w2, w2), 1)
    bd = (ri < n) == (ci < n)
    return tri, m0, m1, strict, incl, eye2, bd


def _wkv_kernel(rf, vf, kkf, lwf, bf, kdf, rr, vr, kkr, lwr, br, kdr, yf_o, yr_o, s_scr):
    @pl.when(pl.program_id(2) == 0)
    def _init():
        s_scr[...] = jnp.zeros_like(s_scr)

    L = rf.shape[1]
    masks = (_wkv_masks(L, LANES, False), _wkv_masks(L, LANES, True))
    chains = []
    for hp in range(rf.shape[2] // LANES):
        ls = slice(hp * LANES, (hp + 1) * LANES)
        chains.append((rf[0, :, ls], vf[0, :, ls], kkf[0, :, ls], lwf[0, :, ls], bf[0, :, ls],
                       kdf[0, :, ls], s_scr[0, hp], False))
        chains.append((rr[0, :, ls], vr[0, :, ls], kkr[0, :, ls], lwr[0, :, ls], br[0, :, ls],
                       kdr[0, :, ls], s_scr[1, hp], True))
    ys, ss = _wkv_chains(chains, masks)
    for hp in range(rf.shape[2] // LANES):
        ls = slice(hp * LANES, (hp + 1) * LANES)
        yf_o[0, :, ls] = ys[2 * hp]
        yr_o[0, :, ls] = ys[2 * hp + 1]
        s_scr[0, hp] = ss[2 * hp]
        s_scr[1, hp] = ss[2 * hp + 1]


def _wkv(r, v, kk, lw0, b0, kd0, lw1, b1, kd1, seq, ctx_len):
    bsz, tc, d = r.shape
    L = WKV_CHUNK
    nlc = seq // L
    ncc = ctx_len // L
    nch = nlc + ncc
    gp = min(WKV_PAIRS_PER_STEP, d // LANES)
    width = gp * LANES

    def fwd_map(b, h, c):
        return (b, jnp.where(c < ncc, nlc + c, c - ncc), h)

    def rev_map(b, h, c):
        return (b, jnp.where(c < ncc, nlc + ncc - 1 - c, nlc - 1 - (c - ncc)), h)

    fs = pl.BlockSpec((1, L, width), fwd_map)
    rs = pl.BlockSpec((1, L, width), rev_map)
    sds = jax.ShapeDtypeStruct((bsz, tc, d), F32)
    return pl.pallas_call(
        _wkv_kernel,
        grid=(bsz, d // width, nch),
        in_specs=[fs] * 6 + [rs] * 6,
        out_specs=[fs, rs],
        out_shape=[sds, sds],
        scratch_shapes=[pltpu.VMEM((2, gp, LANES, LANES), F32)],
        name="wkv_scan",
        compiler_params=_cparams(("parallel", "parallel", "arbitrary")),
    )(r, v, kk, lw0, b0, kd0, r, v, kk, lw1, b1, kd1)


def _rwkv_out_kernel(head, x_ref, yf_ref, yr_ref, r_ref, v_ref, g_ref, kd0_ref, kd1_ref, mod_ref,
                     rk_ref, lnw_ref, lnb_ref, wo_ref, hs_ref, hst_ref, o_ref):
    hs = hs_ref[...]
    hst = hst_ref[...]
    o = yf_ref[0] + yr_ref[0]
    inv_n = 1.0 / head
    mu = _dot3(_dot3(o, hs) * inv_n, hst)
    dlt = o - mu
    var = _dot3(dlt * dlt, hs) * inv_n
    on = dlt * _dot3(lax.rsqrt(var + GN_EPS), hst)
    on = on * lnw_ref[...] + lnb_ref[...]
    kb = 0.5 * kd0_ref[0] + 0.5 * kd1_ref[0]
    bonus = _dot3(_dot3(r_ref[0] * kb * rk_ref[...], hs), hst) * v_ref[0]
    y = ((on + bonus) * g_ref[0]).astype(BF16)
    gate = mod_ref[0, 0, 2:3, :]
    o_ref[0] = x_ref[0] + gate * _dot(y, wo_ref[...])


def _rwkv_out(xc, yf, yr, r, v, g, kd0, kd1, mod, p, nl, head):
    bsz, tc, d = xc.shape
    tt = TOKEN_TILE
    tile = pl.BlockSpec((1, tt, d), lambda b, j: (b, j, 0))

    def full(a):
        nd = a.ndim
        return pl.BlockSpec(a.shape, lambda b, j, _n=nd: (0,) * _n)

    consts = [p['r_k'], p['ln_w'], p['ln_b'], p['wo'], p['hs'], p['hst']]
    return pl.pallas_call(
        functools.partial(_rwkv_out_kernel, head),
        grid=(bsz, tc // tt),
        in_specs=[tile] * 8 + [pl.BlockSpec((1, 1, 6, d), lambda b, j: (b, (j >= nl).astype(I32), 0, 0))]
        + [full(a) for a in consts],
        out_specs=tile,
        out_shape=jax.ShapeDtypeStruct((bsz, tc, d), F32),
        input_output_aliases={0: 0},
        name="rwkv_out",
        compiler_params=_cparams(("parallel", "arbitrary")),
    )(xc, yf, yr, r, v, g, kd0, kd1, mod, *consts)


def _pool_kernel(nl, seq, ctx_len, x_ref, xp_ref, xn_ref, mod_ref, gn_ref, w_ref, sc_ref, o_ref, ext_scr):
    tt = x_ref.shape[1]
    d = x_ref.shape[2]
    ph = POOL_HALO
    j = pl.program_id(1)
    nct = ctx_len // tt
    shift = mod_ref[0, 0, 0:1, :]
    scale = mod_ref[0, 0, 1:2, :]
    gn = gn_ref[...]
    first = jnp.logical_or(j == 0, j == nl)
    last = jnp.logical_or(j == nl - 1, j == nl + nct - 1)
    hp = _norm_mod(xp_ref[0], gn, shift, scale)
    hn = _norm_mod(xn_ref[0], gn, shift, scale)
    ext_scr[0:ph, :] = jnp.where(first, 0.0, hp)
    ext_scr[ph:ph + tt, :] = _norm_mod(x_ref[0], gn, shift, scale)
    ext_scr[ph + tt:, :] = jnp.where(last, 0.0, hn)

    i = lax.broadcasted_iota(I32, (tt, 1), 0)
    t = jnp.where(j < nl, j * tt + i, (j - nl) * tt + i)
    tseg = jnp.where(j < nl, seq, ctx_len)
    ng = len(POOL_WINDOWS)
    dg = d // ng
    gate = mod_ref[0, 0, 2:3, :]
    for gi, win in enumerate(POOL_WINDOWS):
        half = win // 2
        cs = slice(gi * dg, (gi + 1) * dg)
        acc = ext_scr[ph - half:ph - half + tt, cs]
        for o in range(1, win):
            acc = acc + ext_scr[ph - half + o:ph - half + o + tt, cs]
        cnt = (jnp.minimum(t + half, tseg) - jnp.maximum(t - half, 0)).astype(F32)
        hg = ext_scr[ph:ph + tt, cs]
        dlt = (acc / cnt - hg).astype(BF16)
        y = _dot(dlt, w_ref[gi]) * sc_ref[:, cs]
        o_ref[0, :, cs] = x_ref[0, :, cs] + gate[:, cs] * y


def _pool(xc, mod, gn, w, sc, nl, seq, ctx_len, n_tiles):
    bsz, tc, d = xc.shape
    tt = TOKEN_TILE
    hb = tt // POOL_HALO
    nhb = tc // POOL_HALO
    tile = pl.BlockSpec((1, tt, d), lambda b, j: (b, j, 0))
    return pl.pallas_call(
        functools.partial(_pool_kernel, nl, seq, ctx_len),
        grid=(bsz, n_tiles),
        in_specs=[
            tile,
            pl.BlockSpec((1, POOL_HALO, d), lambda b, j: (b, jnp.maximum(j * hb - 1, 0), 0)),
            pl.BlockSpec((1, POOL_HALO, d), lambda b, j: (b, jnp.minimum((j + 1) * hb, nhb - 1), 0)),
            pl.BlockSpec((1, 1, 6, d), lambda b, j: (b, (j >= nl).astype(I32), 0, 0)),
            pl.BlockSpec(gn.shape, lambda b, j: (0, 0)),
            pl.BlockSpec(w.shape, lambda b, j: (0, 0, 0)),
            pl.BlockSpec(sc.shape, lambda b, j: (0, 0)),
        ],
        out_specs=tile,
        out_shape=jax.ShapeDtypeStruct((bsz, n_tiles * tt, d), F32),
        scratch_shapes=[pltpu.VMEM((tt + 2 * POOL_HALO, d), F32)],
        name="pool_mix",
        compiler_params=_cparams(("parallel", "arbitrary")),
    )(xc, xc, xc, mod, gn, w, sc)


def _router_kernel(x_ref, mod_ref, gn_ref, rt_ref, h_o, aff_o):
    h = _norm_mod(x_ref[0], gn_ref[...], mod_ref[0, 0, 3:4, :], mod_ref[0, 0, 4:5, :])
    h_o[0] = h.astype(BF16)
    logits = lax.dot_general(rt_ref[...], h, (((1,), (1,)), ((), ())),
                             precision=HIGHEST, preferred_element_type=F32)
    m = jnp.max(logits, axis=0, keepdims=True)
    ex = jnp.exp(logits - m)
    aff_o[0] = ex / jnp.sum(ex, axis=0, keepdims=True)


def _router(xc, mod, gn, router_t, nl, n_tiles):
    bsz, tc, d = xc.shape
    tt = TOKEN_TILE
    e = router_t.shape[0]
    tile = pl.BlockSpec((1, tt, d), lambda b, j: (b, j, 0))
    return pl.pallas_call(
        _router_kernel,
        grid=(bsz, n_tiles),
        in_specs=[
            tile,
            pl.BlockSpec((1, 1, 6, d), lambda b, j: (b, (j >= nl).astype(I32), 0, 0)),
            pl.BlockSpec(gn.shape, lambda b, j: (0, 0)),
            pl.BlockSpec(router_t.shape, lambda b, j: (0, 0)),
        ],
        out_specs=[tile, pl.BlockSpec((1, e, tt), lambda b, j: (b, 0, j))],
        out_shape=[jax.ShapeDtypeStruct((bsz, n_tiles * tt, d), BF16),
                   jax.ShapeDtypeStruct((bsz, e, n_tiles * tt), F32)],
        name="moe_router",
        compiler_params=_cparams(("parallel", "arbitrary")),
    )(xc, mod, gn, router_t)


def _lane_cumsum_excl(x, utri):
    e, t = x.shape
    off = jnp.zeros((e, 1), F32)
    parts = []
    for kb in range(t // LANES):
        blk = x[:, kb * LANES:(kb + 1) * LANES]
        inc = _dot(blk.astype(BF16), utri)
        parts.append(inc - blk + off)
        off = off + inc[:, LANES - 1:LANES]
    return jnp.concatenate(parts, axis=1), off


def _select_kernel(cap, slot_stride, aff_ref, pos_o, post_o, gate_o):
    a = aff_ref[0]
    e, t = a.shape
    bits = pltpu.bitcast(a, I32)

    def body(it, thr):
        cand = thr | lax.shift_left(jnp.int32(1), jnp.int32(29) - it)
        cnt = jnp.sum(jnp.where(bits >= cand, 1.0, 0.0), axis=1, keepdims=True)
        return jnp.where(cnt >= cap, cand, thr)

    thr = lax.fori_loop(0, 30, body, jnp.zeros((e, 1), I32))
    gt = jnp.where(bits > thr, 1.0, 0.0)
    eq = jnp.where(bits == thr, 1.0, 0.0)
    ri = lax.broadcasted_iota(I32, (LANES, LANES), 0)
    ci = lax.broadcasted_iota(I32, (LANES, LANES), 1)
    utri = jnp.where(ri <= ci, 1.0, 0.0).astype(BF16)
    n_gt = jnp.sum(gt, axis=1, keepdims=True)
    eq_rank, _ = _lane_cumsum_excl(eq, utri)
    sel = gt + eq * jnp.where(eq_rank < cap - n_gt, 1.0, 0.0)
    pos, _ = _lane_cumsum_excl(sel, utri)
    base = pl.program_id(0) * slot_stride
    posm = jnp.where(sel > 0.5, pos.astype(I32) + base, -1)
    pos_o[0] = posm
    post_o[0] = jnp.transpose(posm.astype(F32)).astype(I32)
    gate_o[0] = jnp.where(sel > 0.5, a, 0.0)


def _select(aff, cap, slot_stride):
    bsz, e, t = aff.shape
    blk = pl.BlockSpec((1, e, t), lambda b: (b, 0, 0))
    return pl.pallas_call(
        functools.partial(_select_kernel, cap, slot_stride),
        grid=(bsz,),
        in_specs=[blk],
        out_specs=[blk, pl.BlockSpec((1, t, e), lambda b: (b, 0, 0)), blk],
        out_shape=[jax.ShapeDtypeStruct((bsz, e, t), I32),
                   jax.ShapeDtypeStruct((bsz, t, e), I32),
                   jax.ShapeDtypeStruct((bsz, e, t), F32)],
        name="moe_select",
        compiler_params=_cparams(("parallel",)),
    )(aff)


def _ffn_kernel(cap, win, tk, fk, lo_ref, h_ref, pos_ref, gate_ref, w1_ref, w3_ref, w2_ref, ys_o,
                xin_scr, gsl_scr):
    t = h_ref.shape[1]
    f = w1_ref.shape[2]
    nkt = t // tk
    pos = pos_ref[0, 0]
    g_hi, g_mid, g_lo = _split3(gate_ref[0, 0])
    prow = lax.broadcasted_iota(I32, (SUBLANES, t), 0)
    g8 = jnp.where(prow == 0, g_hi.astype(F32),
                   jnp.where(prow == 1, g_mid.astype(F32),
                             jnp.where(prow == 2, g_lo.astype(F32), 0.0))).astype(BF16)
    slot = lax.broadcasted_iota(I32, (win, tk), 0)
    xin_scr[...] = jnp.zeros_like(xin_scr)
    gsl_scr[...] = jnp.zeros_like(gsl_scr)
    base = (pl.program_id(1) * pl.num_programs(0) + pl.program_id(0)) * nkt
    for kt in range(nkt):
        ts = slice(kt * tk, (kt + 1) * tk)
        lo = lo_ref[base + kt]
        start = pl.multiple_of(jnp.minimum((lo // SUBLANES) * SUBLANES, cap - win), SUBLANES)
        onehot = jnp.where(pos[:, ts] - start == slot, 1.0, 0.0).astype(BF16)
        xin_scr[pl.ds(start, win), :] += _dot(onehot, h_ref[0, ts, :])
        gsl_scr[pl.ds(start, win), :] += _dot_nt(onehot, g8[:, ts])
    gate_slot = jnp.sum(gsl_scr[...], axis=1, keepdims=True)
    xb = xin_scr[...].astype(BF16)
    y = jnp.zeros(xin_scr.shape, F32)
    n_chunks = f // fk

    def up(kf):
        fs = slice(kf * fk, (kf + 1) * fk)
        return _dot(xb, w1_ref[0, :, fs]), _dot(xb, w3_ref[0, :, fs])

    nxt = up(0)
    for kf in range(n_chunks):
        h1, h3 = nxt
        if kf + 1 < n_chunks:
            nxt = up(kf + 1)
        hid = (h1 * _sigmoid(h1) * h3).astype(BF16)
        y = y + _dot(hid, w2_ref[0, kf * fk:(kf + 1) * fk, :])
    ys_o[0, 0] = (y * gate_slot).astype(BF16)


def _ffn(h, posm, gate, w1, w3, w2, cap):
    gsz, t, d = h.shape
    e, _, f = w1.shape
    tk = GATHER_TILE
    fk = min(f, 512)
    win = min(cap, tk + SUBLANES)
    assert t % tk == 0 and (cap - win) % SUBLANES == 0
    nkt = t // tk
    first = jnp.min(jnp.where(posm >= 0, posm, cap).reshape(gsz, e, nkt, tk), axis=3)
    first = jnp.where(first >= cap, 0, first).reshape(-1)
    return pl.pallas_call(
        functools.partial(_ffn_kernel, cap, win, tk, fk),
        grid_spec=pltpu.PrefetchScalarGridSpec(
            num_scalar_prefetch=1,
            grid=(e, gsz),
            in_specs=[
                pl.BlockSpec((1, t, d), lambda ei, gi, lo: (gi, 0, 0)),
                pl.BlockSpec((1, 1, 1, t), lambda ei, gi, lo: (gi, ei, 0, 0)),
                pl.BlockSpec((1, 1, 1, t), lambda ei, gi, lo: (gi, ei, 0, 0)),
                pl.BlockSpec((1, d, f), lambda ei, gi, lo: (ei, 0, 0)),
                pl.BlockSpec((1, d, f), lambda ei, gi, lo: (ei, 0, 0)),
                pl.BlockSpec((1, f, d), lambda ei, gi, lo: (ei, 0, 0)),
            ],
            out_specs=pl.BlockSpec((1, 1, cap, d), lambda ei, gi, lo: (gi, ei, 0, 0)),
            scratch_shapes=[pltpu.VMEM((cap, d), F32), pltpu.VMEM((cap, SUBLANES), F32)],
        ),
        out_shape=jax.ShapeDtypeStruct((gsz, e, cap, d), BF16),
        name="moe_ffn",
        compiler_params=_cparams(("arbitrary", "arbitrary")),
    )(first, h, posm, gate, w1, w3, w2)


def _combine_kernel(win, tail, lo_ref, hi_ref, x_ref, post_ref, ys_ref, mod_ref, o_ref):
    tt = x_ref.shape[1]
    e = ys_ref.shape[1]
    cap = ys_ref.shape[2]
    base = (pl.program_id(0) * pl.num_programs(1) + pl.program_id(1)) * e
    slot = lax.broadcasted_iota(I32, (tt, win), 1)
    def window_start(ei):
        lo = lo_ref[base + ei]
        return pl.multiple_of(jnp.minimum((lo // BF16_ROWS) * BF16_ROWS, cap - win - tail), BF16_ROWS)

    acc = jnp.zeros(x_ref.shape[1:], F32)
    for ei in range(e):
        start = window_start(ei)
        onehot = jnp.where(post_ref[0, :, ei:ei + 1] - start == slot, 1.0, 0.0).astype(BF16)
        acc = acc + _dot(onehot, ys_ref[0, ei, pl.ds(start, win), :])
    o_ref[0] = x_ref[0] + mod_ref[0, 0, 5:6, :] * acc
    for ei in range(e if tail else 0):
        tstart = pl.multiple_of(window_start(ei) + win, BF16_ROWS)

        @pl.when(hi_ref[base + ei] > tstart)
        def _tail():
            tslot = lax.broadcasted_iota(I32, (tt, tail), 1)
            oh = jnp.where(post_ref[0, :, ei:ei + 1] - tstart == tslot, 1.0, 0.0).astype(BF16)
            o_ref[0] += mod_ref[0, 0, 5:6, :] * _dot(oh, ys_ref[0, ei, pl.ds(tstart, tail), :])


def _combine(xc, post, ys, mod, tile0, n_tiles, shared_slots):
    bsz, tc, d = xc.shape
    tt = TOKEN_TILE
    _, e, cap, _ = ys.shape
    is_ctx = 1 if tile0 > 0 else 0
    ys_map = (lambda b, j, lo, hi: (0, 0, 0, 0)) if shared_slots else (lambda b, j, lo, hi: (b, 0, 0, 0))
    win = min(cap, tt)
    tail = BF16_ROWS if cap >= tt + BF16_ROWS else 0
    assert tail > 0 or win == cap
    assert (cap - win - tail) % BF16_ROWS == 0
    pr = post.reshape(bsz, n_tiles, tt, e)
    first = jnp.min(jnp.where(pr >= 0, pr, cap), axis=2)
    first = jnp.where(first >= cap, 0, first).reshape(-1)
    last = (jnp.max(pr, axis=2) + 1).reshape(-1)
    return pl.pallas_call(
        functools.partial(_combine_kernel, win, tail),
        grid_spec=pltpu.PrefetchScalarGridSpec(
            num_scalar_prefetch=2,
            grid=(bsz, n_tiles),
            in_specs=[
                pl.BlockSpec((1, tt, d), lambda b, j, lo, hi: (b, tile0 + j, 0)),
                pl.BlockSpec((1, tt, e), lambda b, j, lo, hi: (b, j, 0)),
                pl.BlockSpec((1, e, cap, d), ys_map),
                pl.BlockSpec((1, 1, 6, d), lambda b, j, lo, hi: (b, is_ctx, 0, 0)),
            ],
            out_specs=pl.BlockSpec((1, tt, d), lambda b, j, lo, hi: (b, tile0 + j, 0)),
        ),
        out_shape=jax.ShapeDtypeStruct((bsz, tc, d), F32),
        input_output_aliases={2: 0},
        name="moe_combine",
        compiler_params=_cparams(("parallel", "arbitrary")),
    )(first, last, xc, post, ys, mod)


def _moe(xc, mod, gn, router_t, w1, w3, w2, nl, nct, seq, ctx_len, do_ctx):
    bsz, tc, d = xc.shape
    e = router_t.shape[0]
    n_tiles = nl + nct if do_ctx else nl
    h, aff = _router(xc, mod, gn, router_t, nl, n_tiles)
    cap_l = CAPACITY_FACTOR * seq // e
    pos, post, gate = _select(aff[:, :, :seq], cap_l, 0)
    ys = _ffn(h[:, :seq, :], pos.reshape(bsz, e, 1, seq), gate.reshape(bsz, e, 1, seq), w1, w3, w2, cap_l)
    xc = _combine(xc, post, ys, mod, 0, nl, False)
    if do_ctx:
        cap_c = CAPACITY_FACTOR * ctx_len // e
        pos, post, gate = _select(aff[:, :, seq:], cap_c, cap_c)
        pos = jnp.transpose(pos, (1, 0, 2)).reshape(1, e, 1, bsz * ctx_len)
        gate = jnp.transpose(gate, (1, 0, 2)).reshape(1, e, 1, bsz * ctx_len)
        h_ctx = h[:, seq:, :].reshape(1, bsz * ctx_len, d)
        ys = _ffn(h_ctx, pos, gate, w1, w3, w2, bsz * cap_c)
        xc = _combine(xc, post, ys, mod, nl, nct, True)
    return xc


def _final_kernel(x_ref, g_ref, o_ref):
    x = x_ref[0]
    ms = jnp.mean(x * x, axis=-1, keepdims=True)
    o_ref[0] = x * lax.rsqrt(ms + RMS_EPS) * g_ref[...]


def _final_norm(xc, g, seq):
    bsz, tc, d = xc.shape
    tt = TOKEN_TILE
    tile = pl.BlockSpec((1, tt, d), lambda b, j: (b, j, 0))
    return pl.pallas_call(
        _final_kernel,
        grid=(bsz, seq // tt),
        in_specs=[tile, pl.BlockSpec(g.shape, lambda b, j: (0, 0))],
        out_specs=tile,
        out_shape=jax.ShapeDtypeStruct((bsz, seq, d), F32),
        name="final_norm",
        compiler_params=_cparams(("parallel", "arbitrary")),
    )(xc, g)


def kernel(x, c, ctx, c_ctx, ada_w, ada_b, norm1_g, norm2_g, rwkv_mix, rwkv_wrkv, rwkv_w0, rwkv_w1, rwkv_w2, rwkv_a0, rwkv_a1, rwkv_a2, rwkv_v0, rwkv_v1, rwkv_v2, rwkv_g1, rwkv_g2, rwkv_kk, rwkv_ka, rwkv_rk, rwkv_lnw, rwkv_lnb, rwkv_wo, pool_w, pool_scale, moe_router, moe_w1, moe_w3, moe_w2, final_g):
    bsz, seq, d = x.shape
    ctx_len = ctx.shape[1]
    depth = ada_w.shape[0]
    n_heads, head = rwkv_rk.shape[1], rwkv_rk.shape[2]
    n_mixers = 2
    tt = TOKEN_TILE
    assert seq % tt == 0 and ctx_len % tt == 0 and tt % GRID_W == 0
    assert head == WKV_CHUNK and 2 * head == LANES and d % LANES == 0
    nl, nct = seq // tt, ctx_len // tt

    xc = jnp.concatenate([x, ctx], axis=1)

    rows = -(-(bsz + 1) // SUBLANES) * SUBLANES
    cond = jnp.zeros((rows, d), F32).at[:bsz].set(c).at[bsz].set(c_ctx)
    m_all = _ada_all(cond, ada_w, ada_b)
    m_lat = m_all[:, :bsz].reshape(depth, bsz, 1, 6, d)
    m_ctx = jnp.broadcast_to(m_all[:, bsz].reshape(depth, 1, 1, 6, d), (depth, bsz, 1, 6, d))
    mods = jnp.concatenate([m_lat, m_ctx], axis=2)

    head_of = jnp.arange(d) // head
    hs = (head_of[:, None] == jnp.arange(HEAD_COLS)[None, :]).astype(BF16)
    hst = jnp.transpose(hs)

    v_first = None
    for i in range(depth):
        last = i == depth - 1
        is_rwkv = i % n_mixers == 0
        jn = i // n_mixers
        mod = mods[i]
        gn1 = norm1_g[i].reshape(1, d)
        if is_rwkv:
            p = {
                'mix': rwkv_mix[jn],
                'wr': rwkv_wrkv[jn, 0].astype(BF16), 'wk': rwkv_wrkv[jn, 1].astype(BF16),
                'wv': rwkv_wrkv[jn, 2].astype(BF16),
                'g1': rwkv_g1[jn].astype(BF16), 'g2': rwkv_g2[jn].astype(BF16),
                'w1': rwkv_w1[jn].astype(BF16), 'w2': rwkv_w2[jn].astype(BF16), 'w0': rwkv_w0[jn],
                'a1': rwkv_a1[jn].astype(BF16), 'a2': rwkv_a2[jn].astype(BF16), 'a0': rwkv_a0[jn],
                'k_k': rwkv_kk[jn].reshape(1, d), 'k_a': rwkv_ka[jn].reshape(1, d),
                'r_k': rwkv_rk[jn].reshape(1, d), 'ln_w': rwkv_lnw[jn].reshape(1, d),
                'ln_b': rwkv_lnb[jn].reshape(1, d), 'wo': rwkv_wo[jn].astype(BF16),
                'hs': hs, 'hst': hst,
            }
            vres = None if jn == 0 else (rwkv_v0[jn - 1].reshape(1, d), rwkv_v1[jn - 1].astype(BF16),
                                         rwkv_v2[jn - 1].astype(BF16))
            r, v, g, kk, lw0, lw1, b0, b1, kd0, kd1 = _rwkv_proj(xc, mod, gn1, p, vres, v_first, nl, seq, ctx_len)
            if v_first is None:
                v_first = v
            yf, yr = _wkv(r, v, kk, lw0, b0, kd0, lw1, b1, kd1, seq, ctx_len)
            xc = _rwkv_out(xc, yf, yr, r, v, g, kd0, kd1, mod, p, nl, head)
        else:
            n_tiles = nl if last else nl + nct
            xc = _pool(xc, mod, gn1, pool_w[jn].astype(BF16), pool_scale[jn].reshape(1, d),
                       nl, seq, ctx_len, n_tiles)
        xc = _moe(xc, mod, norm2_g[i].reshape(1, d), jnp.transpose(moe_router[i]),
                  moe_w1[i].astype(BF16), moe_w3[i].astype(BF16), moe_w2[i].astype(BF16),
                  nl, nct, seq, ctx_len, not last)
    return _final_norm(xc, final_g.reshape(1, d), seq)
```

```python
import functools
import math

import jax
import jax.numpy as jnp
from jax import lax
from jax.experimental import pallas as pl
from jax.experimental.pallas import tpu as pltpu

F32 = jnp.float32
BF16 = jnp.bfloat16
I32 = jnp.int32
HIGHEST = lax.Precision.HIGHEST

GRID_W = 64
POOL_WINDOWS = (2, 4, 8, 16)
CAPACITY_FACTOR = 2
RMS_EPS = 1e-6
GN_EPS = 64e-5
EXP_NEG_HALF = math.exp(-0.5)

LANES = 128
SUBLANES = 8
BF16_ROWS = 16
VMEM_LIMIT_BYTES = 56 * 1024 * 1024

TOKEN_TILE = 256
GATHER_TILE = 256
WKV_CHUNK = 64
WKV_PAIRS_PER_STEP = 8
POOL_HALO = 8
HEAD_COLS = LANES


def _cparams(sem):
    return pltpu.CompilerParams(dimension_semantics=sem, vmem_limit_bytes=VMEM_LIMIT_BYTES)


def _dot(a, b):
    return jnp.dot(a, b, preferred_element_type=F32)


def _dot_nt(a, b):
    return lax.dot_general(a, b, (((1,), (1,)), ((), ())), preferred_element_type=F32)


def _dot_tn(a, b):
    return lax.dot_general(a, b, (((0,), (0,)), ((), ())), preferred_element_type=F32)


def _split3(x):
    hi = x.astype(BF16)
    r1 = x - hi.astype(F32)
    mid = r1.astype(BF16)
    lo = (r1 - mid.astype(F32)).astype(BF16)
    return hi, mid, lo


def _dot_split(x, m, pieces):
    hi = x.astype(BF16)
    out = _dot(hi, m)
    for _ in range(pieces - 1):
        x = x - hi.astype(F32)
        hi = x.astype(BF16)
        out = out + _dot(hi, m)
    return out


def _dot3_left(m, x):
    w = x.shape[1]
    z = _dot(m, jnp.concatenate(_split3(x), axis=1))
    return z[:, 0:w] + z[:, w:2 * w] + z[:, 2 * w:3 * w]


def _sigmoid(x):
    return 0.5 * jnp.tanh(0.5 * x) + 0.5


def _norm_mod(x, g, shift, scale):
    ms = jnp.mean(x * x, axis=-1, keepdims=True)
    y = x * lax.rsqrt(ms + RMS_EPS) * g
    return y * (1.0 + scale) + shift


def _ada_kernel(c_ref, w_ref, b_ref, o_ref):
    c = c_ref[...]
    s = c * _sigmoid(c)
    o_ref[0] = jnp.dot(s, w_ref[0], precision=HIGHEST, preferred_element_type=F32) + b_ref[0]


def _ada_all(cond, ada_w, ada_b):
    depth, d, n6 = ada_w.shape
    rows = cond.shape[0]
    nt = 512
    return pl.pallas_call(
        _ada_kernel,
        grid=(depth, n6 // nt),
        in_specs=[
            pl.BlockSpec((rows, d), lambda i, n: (0, 0)),
            pl.BlockSpec((1, d, nt), lambda i, n: (i, 0, n)),
            pl.BlockSpec((1, 1, nt), lambda i, n: (i, 0, n)),
        ],
        out_specs=pl.BlockSpec((1, rows, nt), lambda i, n: (i, 0, n)),
        out_shape=jax.ShapeDtypeStruct((depth, rows, n6), F32),
        name="ada_mod",
        compiler_params=_cparams(("arbitrary", "arbitrary")),
    )(cond, ada_w, ada_b.reshape(depth, 1, n6))


def _rwkv_proj_kernel(nl, seq, ctx_len, has_vres, *refs):
    if has_vres:
        (x_ref, xp_ref, xn_ref, mod_ref, gn_ref, mix_ref, wr_ref, wk_ref, wv_ref, g1_ref, g2_ref,
         w1_ref, w2_ref, w0_ref, a1_ref, a2_ref, a0_ref, kkw_ref, kaw_ref, hs_ref, hst_ref,
         v0_ref, v1_ref, v2_ref, vf_ref,
         r_o, v_o, g_o, kk_o, lw0_o, lw1_o, b0_o, b1_o, kd0_o, kd1_o, ext_scr, sh_scr) = refs
    else:
        (x_ref, xp_ref, xn_ref, mod_ref, gn_ref, mix_ref, wr_ref, wk_ref, wv_ref, g1_ref, g2_ref,
         w1_ref, w2_ref, w0_ref, a1_ref, a2_ref, a0_ref, kkw_ref, kaw_ref, hs_ref, hst_ref,
         r_o, v_o, g_o, kk_o, lw0_o, lw1_o, b0_o, b1_o, kd0_o, kd1_o, ext_scr, sh_scr) = refs
    tt = x_ref.shape[1]
    d = x_ref.shape[2]
    hw = GRID_W
    j = pl.program_id(1)
    shift = mod_ref[0, 0, 0:1, :]
    scale = mod_ref[0, 0, 1:2, :]
    gn = gn_ref[...]
    ext_scr[0:hw, :] = _norm_mod(xp_ref[0], gn, shift, scale)
    ext_scr[hw:hw + tt, :] = _norm_mod(x_ref[0], gn, shift, scale)
    ext_scr[hw + tt:, :] = _norm_mod(xn_ref[0], gn, shift, scale)

    i = lax.broadcasted_iota(I32, (tt, 1), 0)
    q = d // 4

    @pl.when(j < nl)
    def _latent_shift():
        t = j * tt + i
        col = i % hw
        sh_scr[:, 0:q] = jnp.where(col != 0, ext_scr[hw - 1:hw - 1 + tt, 0:q], 0.0)
        sh_scr[:, q:2 * q] = jnp.where(col != hw - 1, ext_scr[hw + 1:hw + 1 + tt, q:2 * q], 0.0)
        sh_scr[:, 2 * q:3 * q] = jnp.where(t >= hw, ext_scr[0:tt, 2 * q:3 * q], 0.0)
        sh_scr[:, 3 * q:] = jnp.where(t < seq - hw, ext_scr[2 * hw:2 * hw + tt, 3 * q:], 0.0)

    @pl.when(j >= nl)
    def _context_shift():
        t = (j - nl) * tt + i
        hd = d // 2
        sh_scr[:, 0:hd] = jnp.where(t != 0, ext_scr[hw - 1:hw - 1 + tt, 0:hd], 0.0)
        sh_scr[:, hd:] = jnp.where(t != ctx_len - 1, ext_scr[hw + 1:hw + 1 + tt, hd:], 0.0)

    h = ext_scr[hw:hw + tt, :]
    xx = sh_scr[...] - h

    def mixed(n):
        return (h + xx * mix_ref[n:n + 1, :]).astype(BF16)

    xr, xw, xk, xv, xa, xg = [mixed(n) for n in range(6)]
    r = _dot(xr, wr_ref[...])
    k = _dot(xk, wk_ref[...])
    v = _dot(xv, wv_ref[...])
    if has_vres:
        lor = _dot(_dot(xv, v1_ref[...]).astype(BF16), v2_ref[...])
        v = v + (vf_ref[0] - v) * _sigmoid(v0_ref[...] + lor)
    g = _dot(_sigmoid(_dot(xg, g1_ref[...])).astype(BF16), g2_ref[...])
    r_o[0] = r
    v_o[0] = v
    g_o[0] = g

    kkr = k * kkw_ref[...]
    ss = _dot_split(kkr * kkr, hs_ref[...], 1)
    inv = 1.0 / jnp.maximum(jnp.sqrt(ss), 1e-12)
    kk = kkr * _dot_split(inv, hst_ref[...], 2)
    kk_o[0] = kk
    kaw = kaw_ref[...]
    for dr, (lw_o, b_o, kd_o) in enumerate(((lw0_o, b0_o, kd0_o), (lw1_o, b1_o, kd1_o))):
        wpre = w0_ref[dr:dr + 1, :] + _dot(jnp.tanh(_dot(xw, w1_ref[dr])).astype(BF16), w2_ref[dr])
        lw_o[0] = -EXP_NEG_HALF * _sigmoid(wpre)
        a = _sigmoid(a0_ref[dr:dr + 1, :] + _dot(_dot(xa, a1_ref[dr]).astype(BF16), a2_ref[dr]))
        b_o[0] = kk * a
        kd_o[0] = k * (1.0 + (a - 1.0) * kaw)


def _rwkv_proj(xc, mod, gn, p, vres, v_first, nl, seq, ctx_len):
    bsz, tc, d = xc.shape
    tt = TOKEN_TILE
    nt = tc // tt
    hb = tt // GRID_W
    nhb = tc // GRID_W
    has_vres = vres is not None

    def full(a):
        nd = a.ndim
        return pl.BlockSpec(a.shape, lambda b, j, _n=nd: (0,) * _n)

    tile = pl.BlockSpec((1, tt, d), lambda b, j: (b, j, 0))
    ins = [xc, xc, xc, mod, gn, p['mix'], p['wr'], p['wk'], p['wv'], p['g1'], p['g2'],
           p['w1'], p['w2'], p['w0'], p['a1'], p['a2'], p['a0'], p['k_k'], p['k_a'], p['hs'], p['hst']]
    specs = [
        tile,
        pl.BlockSpec((1, GRID_W, d), lambda b, j: (b, jnp.maximum(j * hb - 1, 0), 0)),
        pl.BlockSpec((1, GRID_W, d), lambda b, j: (b, jnp.minimum((j + 1) * hb, nhb - 1), 0)),
        pl.BlockSpec((1, 1, 6, d), lambda b, j: (b, (j >= nl).astype(I32), 0, 0)),
    ] + [full(a) for a in ins[4:]]
    if has_vres:
        ins += [vres[0], vres[1], vres[2], v_first]
        specs += [full(vres[0]), full(vres[1]), full(vres[2]), tile]
    out_sds = jax.ShapeDtypeStruct((bsz, tc, d), F32)
    return pl.pallas_call(
        functools.partial(_rwkv_proj_kernel, nl, seq, ctx_len, has_vres),
        grid=(bsz, nt),
        in_specs=specs,
        out_specs=[tile] * 10,
        out_shape=[out_sds] * 10,
        scratch_shapes=[pltpu.VMEM((tt + 2 * GRID_W, d), F32), pltpu.VMEM((tt, d), F32)],
        name="rwkv_proj",
        compiler_params=_cparams(("parallel", "arbitrary")),
    )(*ins)


def _wkv_chains(chains, masks):
    nc = len(chains)
    ks = range(nc)
    L = chains[0][0].shape[0]
    r, v, kk, lw, b, kd, s_prev, rev = [[ch[i] for ch in chains] for i in range(8)]
    mk = [masks[1] if rv else masks[0] for rv in rev]
    tri, m0, m1, strict, incl, eye2, bd = [[m[i] for m in mk] for i in range(7)]

    def stack(k, x):
        xb = x.astype(BF16)
        return jnp.concatenate([xb * m0[k], xb * m1[k]], axis=0)

    c = [_dot3_left(tri[k], lw[k]) for k in ks]
    ctot = [c[k][0:1, :] if rev[k] else c[k][L - 1:L, :] for k in ks]
    e_c = [jnp.exp(c[k]) for k in ks]
    e_nc = [jnp.exp(-c[k]) for k in ks]
    e_tc = [jnp.exp(ctot[k] - c[k]) for k in ks]
    ah = [-kk[k] * jnp.exp(c[k] - lw[k]) for k in ks]
    rh = [r[k] * e_c[k] for k in ks]
    lhs = [jnp.concatenate([ah[k], rh[k]], axis=0).astype(BF16) for k in ks]
    rhs = [jnp.concatenate([stack(k, b[k] * e_nc[k]), stack(k, kd[k] * e_nc[k])], axis=0) for k in ks]
    aa = [_dot_nt(lhs[k], rhs[k]) for k in ks]
    a_ab = [jnp.where(strict[k], aa[k][0:L, 0:2 * L], 0.0) for k in ks]
    a_ak = [jnp.where(strict[k], aa[k][0:L, 2 * L:4 * L], 0.0).astype(BF16) for k in ks]
    a_r = [jnp.where(jnp.concatenate([incl[k], incl[k]], axis=1), aa[k][L:2 * L, :], 0.0).astype(BF16)
           for k in ks]

    n_dbl = int(math.log2(L))
    tm = [eye2[k] + a_ab[k] for k in ks]
    pw = [_dot(a_ab[k].astype(BF16), stack(k, a_ab[k])) for k in ks]
    for _ in range(n_dbl - 2):
        z = [_dot(pw[k].astype(BF16), jnp.concatenate([stack(k, tm[k]), stack(k, pw[k])], axis=1)) for k in ks]
        tm = [tm[k] + z[k][:, 0:2 * L] for k in ks]
        pw = [z[k][:, 2 * L:4 * L] for k in ks]
    tm = [tm[k] + _dot(pw[k].astype(BF16), stack(k, tm[k])) for k in ks]

    ss0 = [_dot_nt(lhs[k], s_prev[k].astype(BF16)) for k in ks]
    vs = [stack(k, v[k]) for k in ks]
    wmat = [ss0[k][0:L] + _dot(a_ak[k], vs[k]) for k in ks]
    u = [_dot(tm[k].astype(BF16), stack(k, wmat[k])) for k in ks]
    y = [ss0[k][L:2 * L] + _dot(a_r[k], jnp.concatenate([stack(k, u[k]), vs[k]], axis=0)) for k in ks]
    uv = [jnp.concatenate([u[k], v[k]], axis=0).astype(BF16) for k in ks]
    bk = [jnp.concatenate([b[k] * e_tc[k], kd[k] * e_tc[k]], axis=0).astype(BF16) for k in ks]
    upd = [_dot_tn(uv[k], bk[k]) for k in ks]
    s_new = [s_prev[k] * jnp.exp(ctot[k]) + jnp.where(bd[k], upd[k], 0.0) for k in ks]
    return y, s_new


def _wkv_masks(L, w2, reverse):
    n = w2 // 2
    row = lax.broadcasted_iota(I32, (L, L), 0)
    colm = lax.broadcasted_iota(I32, (L, L), 1)
    tri = jnp.where((colm >= row) if reverse else (colm <= row), 1.0, 0.0).astype(BF16)
    lane = lax.broadcasted_iota(I32, (1, w2), 1)
    m0 = jnp.where(lane < n, 1.0, 0.0).astype(BF16)
    m1 = jnp.where(lane < n, 0.0, 1.0).astype(BF16)
    t_i = lax.broadcasted_iota(I32, (L, 2 * L), 0)
    s_i = lax.broadcasted_iota(I32, (L, 2 * L), 1) % L
    strict = (s_i > t_i) if reverse else (s_i < t_i)
    incl = (s_i >= t_i) if reverse else (s_i <= t_i)
    eye2 = jnp.where(s_i == t_i, 1.0, 0.0)
    ri = lax.broadcasted_iota(I32, (w2, w2), 0)
    ci = lax.broadcasted_iota(I32, (w2, w2), 1)
    bd = (ri < n) == (ci < n)
    return tri, m0, m1, strict, incl, eye2, bd


def _wkv_kernel(rf, vf, kkf, lwf, bf, kdf, rr, vr, kkr, lwr, br, kdr, yf_o, yr_o, s_scr):
    @pl.when(pl.program_id(2) == 0)
    def _init():
        s_scr[...] = jnp.zeros_like(s_scr)

    L = rf.shape[1]
    masks = (_wkv_masks(L, LANES, False), _wkv_masks(L, LANES, True))
    chains = []
    for hp in range(rf.shape[2] // LANES):
        ls = slice(hp * LANES, (hp + 1) * LANES)
        chains.append((rf[0, :, ls], vf[0, :, ls], kkf[0, :, ls], lwf[0, :, ls], bf[0, :, ls],
                       kdf[0, :, ls], s_scr[0, hp], False))
        chains.append((rr[0, :, ls], vr[0, :, ls], kkr[0, :, ls], lwr[0, :, ls], br[0, :, ls],
                       kdr[0, :, ls], s_scr[1, hp], True))
    ys, ss = _wkv_chains(chains, masks)
    for hp in range(rf.shape[2] // LANES):
        ls = slice(hp * LANES, (hp + 1) * LANES)
        yf_o[0, :, ls] = ys[2 * hp]
        yr_o[0, :, ls] = ys[2 * hp + 1]
        s_scr[0, hp] = ss[2 * hp]
        s_scr[1, hp] = ss[2 * hp + 1]


def _wkv(r, v, kk, lw0, b0, kd0, lw1, b1, kd1, seq, ctx_len):
    bsz, tc, d = r.shape
    L = WKV_CHUNK
    nlc = seq // L
    ncc = ctx_len // L
    nch = nlc + ncc
    gp = min(WKV_PAIRS_PER_STEP, d // LANES)
    width = gp * LANES

    def fwd_map(b, h, c):
        return (b, jnp.where(c < ncc, nlc + c, c - ncc), h)

    def rev_map(b, h, c):
        return (b, jnp.where(c < ncc, nlc + ncc - 1 - c, nlc - 1 - (c - ncc)), h)

    fs = pl.BlockSpec((1, L, width), fwd_map)
    rs = pl.BlockSpec((1, L, width), rev_map)
    sds = jax.ShapeDtypeStruct((bsz, tc, d), F32)
    return pl.pallas_call(
        _wkv_kernel,
        grid=(bsz, d // width, nch),
        in_specs=[fs] * 6 + [rs] * 6,
        out_specs=[fs, rs],
        out_shape=[sds, sds],
        scratch_shapes=[pltpu.VMEM((2, gp, LANES, LANES), F32)],
        name="wkv_scan",
        compiler_params=_cparams(("parallel", "parallel", "arbitrary")),
    )(r, v, kk, lw0, b0, kd0, r, v, kk, lw1, b1, kd1)


def _rwkv_out_kernel(head, x_ref, yf_ref, yr_ref, r_ref, v_ref, g_ref, kd0_ref, kd1_ref, mod_ref,
                     rk_ref, lnw_ref, lnb_ref, wo_ref, hs_ref, hst_ref, o_ref):
    hs = hs_ref[...]
    hst = hst_ref[...]
    o = yf_ref[0] + yr_ref[0]
    inv_n = 1.0 / head
    mu = _dot_split(_dot_split(o, hs, 2) * inv_n, hst, 2)
    dlt = o - mu
    var = _dot_split(dlt * dlt, hs, 1) * inv_n
    on = dlt * _dot_split(lax.rsqrt(var + GN_EPS), hst, 2)
    on = on * lnw_ref[...] + lnb_ref[...]
    kb = 0.5 * kd0_ref[0] + 0.5 * kd1_ref[0]
    bonus = _dot_split(_dot_split(r_ref[0] * kb * rk_ref[...], hs, 1), hst, 2) * v_ref[0]
    y = ((on + bonus) * g_ref[0]).astype(BF16)
    gate = mod_ref[0, 0, 2:3, :]
    o_ref[0] = x_ref[0] + gate * _dot(y, wo_ref[...])


def _rwkv_out(xc, yf, yr, r, v, g, kd0, kd1, mod, p, nl, head):
    bsz, tc, d = xc.shape
    tt = TOKEN_TILE
    tile = pl.BlockSpec((1, tt, d), lambda b, j: (b, j, 0))

    def full(a):
        nd = a.ndim
        return pl.BlockSpec(a.shape, lambda b, j, _n=nd: (0,) * _n)

    consts = [p['r_k'], p['ln_w'], p['ln_b'], p['wo'], p['hs'], p['hst']]
    return pl.pallas_call(
        functools.partial(_rwkv_out_kernel, head),
        grid=(bsz, tc // tt),
        in_specs=[tile] * 8 + [pl.BlockSpec((1, 1, 6, d), lambda b, j: (b, (j >= nl).astype(I32), 0, 0))]
        + [full(a) for a in consts],
        out_specs=tile,
        out_shape=jax.ShapeDtypeStruct((bsz, tc, d), F32),
        input_output_aliases={0: 0},
        name="rwkv_out",
        compiler_params=_cparams(("parallel", "arbitrary")),
    )(xc, yf, yr, r, v, g, kd0, kd1, mod, *consts)


def _pool_kernel(nl, seq, ctx_len, x_ref, xp_ref, xn_ref, mod_ref, gn_ref, w_ref, sc_ref, o_ref, ext_scr):
    tt = x_ref.shape[1]
    d = x_ref.shape[2]
    ph = POOL_HALO
    j = pl.program_id(1)
    nct = ctx_len // tt
    shift = mod_ref[0, 0, 0:1, :]
    scale = mod_ref[0, 0, 1:2, :]
    gn = gn_ref[...]
    first = jnp.logical_or(j == 0, j == nl)
    last = jnp.logical_or(j == nl - 1, j == nl + nct - 1)
    hp = _norm_mod(xp_ref[0], gn, shift, scale)
    hn = _norm_mod(xn_ref[0], gn, shift, scale)
    ext_scr[0:ph, :] = jnp.where(first, 0.0, hp)
    ext_scr[ph:ph + tt, :] = _norm_mod(x_ref[0], gn, shift, scale)
    ext_scr[ph + tt:, :] = jnp.where(last, 0.0, hn)

    i = lax.broadcasted_iota(I32, (tt, 1), 0)
    t = jnp.where(j < nl, j * tt + i, (j - nl) * tt + i)
    tseg = jnp.where(j < nl, seq, ctx_len)
    ng = len(POOL_WINDOWS)
    dg = d // ng
    gate = mod_ref[0, 0, 2:3, :]
    for gi, win in enumerate(POOL_WINDOWS):
        half = win // 2
        cs = slice(gi * dg, (gi + 1) * dg)
        acc = ext_scr[ph - half:ph - half + tt, cs]
        for o in range(1, win):
            acc = acc + ext_scr[ph - half + o:ph - half + o + tt, cs]
        cnt = (jnp.minimum(t + half, tseg) - jnp.maximum(t - half, 0)).astype(F32)
        hg = ext_scr[ph:ph + tt, cs]
        dlt = (acc / cnt - hg).astype(BF16)
        y = _dot(dlt, w_ref[gi]) * sc_ref[:, cs]
        o_ref[0, :, cs] = x_ref[0, :, cs] + gate[:, cs] * y


def _pool(xc, mod, gn, w, sc, nl, seq, ctx_len, n_tiles):
    bsz, tc, d = xc.shape
    tt = TOKEN_TILE
    hb = tt // POOL_HALO
    nhb = tc // POOL_HALO
    tile = pl.BlockSpec((1, tt, d), lambda b, j: (b, j, 0))
    return pl.pallas_call(
        functools.partial(_pool_kernel, nl, seq, ctx_len),
        grid=(bsz, n_tiles),
        in_specs=[
            tile,
            pl.BlockSpec((1, POOL_HALO, d), lambda b, j: (b, jnp.maximum(j * hb - 1, 0), 0)),
            pl.BlockSpec((1, POOL_HALO, d), lambda b, j: (b, jnp.minimum((j + 1) * hb, nhb - 1), 0)),
            pl.BlockSpec((1, 1, 6, d), lambda b, j: (b, (j >= nl).astype(I32), 0, 0)),
            pl.BlockSpec(gn.shape, lambda b, j: (0, 0)),
            pl.BlockSpec(w.shape, lambda b, j: (0, 0, 0)),
            pl.BlockSpec(sc.shape, lambda b, j: (0, 0)),
        ],
        out_specs=tile,
        out_shape=jax.ShapeDtypeStruct((bsz, n_tiles * tt, d), F32),
        scratch_shapes=[pltpu.VMEM((tt + 2 * POOL_HALO, d), F32)],
        name="pool_mix",
        compiler_params=_cparams(("parallel", "arbitrary")),
    )(xc, xc, xc, mod, gn, w, sc)


def _router_kernel(x_ref, mod_ref, gn_ref, rt_ref, h_o, aff_o):
    h = _norm_mod(x_ref[0], gn_ref[...], mod_ref[0, 0, 3:4, :], mod_ref[0, 0, 4:5, :])
    h_o[0] = h.astype(BF16)
    logits = lax.dot_general(rt_ref[...], h, (((1,), (1,)), ((), ())),
                             precision=HIGHEST, preferred_element_type=F32)
    m = jnp.max(logits, axis=0, keepdims=True)
    ex = jnp.exp(logits - m)
    aff_o[0] = ex / jnp.sum(ex, axis=0, keepdims=True)


def _router(xc, mod, gn, router_t, tile0, n_tiles):
    bsz, tc, d = xc.shape
    tt = TOKEN_TILE
    e = router_t.shape[0]
    is_ctx = 1 if tile0 > 0 else 0
    return pl.pallas_call(
        _router_kernel,
        grid=(bsz, n_tiles),
        in_specs=[
            pl.BlockSpec((1, tt, d), lambda b, j: (b, tile0 + j, 0)),
            pl.BlockSpec((1, 1, 6, d), lambda b, j: (b, is_ctx, 0, 0)),
            pl.BlockSpec(gn.shape, lambda b, j: (0, 0)),
            pl.BlockSpec(router_t.shape, lambda b, j: (0, 0)),
        ],
        out_specs=[pl.BlockSpec((1, tt, d), lambda b, j: (b, j, 0)),
                   pl.BlockSpec((1, e, tt), lambda b, j: (b, 0, j))],
        out_shape=[jax.ShapeDtypeStruct((bsz, n_tiles * tt, d), BF16),
                   jax.ShapeDtypeStruct((bsz, e, n_tiles * tt), F32)],
        name="moe_router",
        compiler_params=_cparams(("parallel", "arbitrary")),
    )(xc, mod, gn, router_t)


def _lane_cumsum_excl(x, utri):
    e, t = x.shape
    off = jnp.zeros((e, 1), F32)
    parts = []
    for kb in range(t // LANES):
        blk = x[:, kb * LANES:(kb + 1) * LANES]
        inc = _dot(blk.astype(BF16), utri)
        parts.append(inc - blk + off)
        off = off + inc[:, LANES - 1:LANES]
    return jnp.concatenate(parts, axis=1), off


def _select_kernel(cap, slot_stride, aff_ref, pos_o, post_o, gate_o):
    a = aff_ref[0]
    e, t = a.shape
    bits = pltpu.bitcast(a, I32)

    def body(it, thr):
        cand = thr | lax.shift_left(jnp.int32(1), jnp.int32(29) - it)
        cnt = jnp.sum(jnp.where(bits >= cand, 1.0, 0.0), axis=1, keepdims=True)
        return jnp.where(cnt >= cap, cand, thr)

    thr = lax.fori_loop(0, 30, body, jnp.zeros((e, 1), I32))
    gt = jnp.where(bits > thr, 1.0, 0.0)
    eq = jnp.where(bits == thr, 1.0, 0.0)
    ri = lax.broadcasted_iota(I32, (LANES, LANES), 0)
    ci = lax.broadcasted_iota(I32, (LANES, LANES), 1)
    utri = jnp.where(ri <= ci, 1.0, 0.0).astype(BF16)
    n_gt = jnp.sum(gt, axis=1, keepdims=True)
    eq_rank, _ = _lane_cumsum_excl(eq, utri)
    sel = gt + eq * jnp.where(eq_rank < cap - n_gt, 1.0, 0.0)
    pos, _ = _lane_cumsum_excl(sel, utri)
    base = pl.program_id(0) * slot_stride
    posm = jnp.where(sel > 0.5, pos.astype(I32) + base, -1)
    pos_o[0] = posm
    post_o[0] = jnp.transpose(posm.astype(F32)).astype(I32)
    gate_o[0] = jnp.where(sel > 0.5, a, 0.0)


def _select(aff, cap, slot_stride):
    bsz, e, t = aff.shape
    blk = pl.BlockSpec((1, e, t), lambda b: (b, 0, 0))
    return pl.pallas_call(
        functools.partial(_select_kernel, cap, slot_stride),
        grid=(bsz,),
        in_specs=[blk],
        out_specs=[blk, pl.BlockSpec((1, t, e), lambda b: (b, 0, 0)), blk],
        out_shape=[jax.ShapeDtypeStruct((bsz, e, t), I32),
                   jax.ShapeDtypeStruct((bsz, t, e), I32),
                   jax.ShapeDtypeStruct((bsz, e, t), F32)],
        name="moe_select",
        compiler_params=_cparams(("parallel",)),
    )(aff)


def _ffn_kernel(cap, win, tk, fk, lo_ref, h_ref, pos_ref, gate_ref, w1_ref, w3_ref, w2_ref, ys_o,
                xin_scr, gsl_scr):
    t = h_ref.shape[1]
    f = w1_ref.shape[3]
    nkt = t // tk
    pos = pos_ref[0, 0]
    g_hi, g_mid, g_lo = _split3(gate_ref[0, 0])
    prow = lax.broadcasted_iota(I32, (SUBLANES, t), 0)
    g8 = jnp.where(prow == 0, g_hi.astype(F32),
                   jnp.where(prow == 1, g_mid.astype(F32),
                             jnp.where(prow == 2, g_lo.astype(F32), 0.0))).astype(BF16)
    slot = lax.broadcasted_iota(I32, (win, tk), 0)
    xin_scr[...] = jnp.zeros_like(xin_scr)
    gsl_scr[...] = jnp.zeros_like(gsl_scr)
    base = (pl.program_id(1) * pl.num_programs(0) + pl.program_id(0)) * nkt
    for kt in range(nkt):
        ts = slice(kt * tk, (kt + 1) * tk)
        lo = lo_ref[base + kt]
        start = pl.multiple_of(jnp.minimum((lo // SUBLANES) * SUBLANES, cap - win), SUBLANES)
        onehot = jnp.where(pos[:, ts] - start == slot, 1.0, 0.0).astype(BF16)
        xin_scr[pl.ds(start, win), :] += _dot(onehot, h_ref[0, ts, :])
        gsl_scr[pl.ds(start, win), :] += _dot_nt(onehot, g8[:, ts])
    gate_slot = jnp.sum(gsl_scr[...], axis=1, keepdims=True)
    xb = xin_scr[...].astype(BF16)
    y = jnp.zeros(xin_scr.shape, F32)
    n_chunks = f // fk

    def up(kf):
        fs = slice(kf * fk, (kf + 1) * fk)
        return _dot(xb, w1_ref[0, 0, :, fs]), _dot(xb, w3_ref[0, 0, :, fs])

    nxt = up(0)
    for kf in range(n_chunks):
        h1, h3 = nxt
        if kf + 1 < n_chunks:
            nxt = up(kf + 1)
        hid = (h1 * _sigmoid(h1) * h3).astype(BF16)
        y = y + _dot(hid, w2_ref[0, 0, kf * fk:(kf + 1) * fk, :])
    ys_o[0, 0] = (y * gate_slot).astype(BF16)


def _ffn(h, posm, gate, layer, w1, w3, w2, cap):
    gsz, t, d = h.shape
    _, e, _, f = w1.shape
    tk = GATHER_TILE
    fk = min(f, 512)
    win = min(cap, tk + SUBLANES)
    assert t % tk == 0 and (cap - win) % SUBLANES == 0
    nkt = t // tk
    first = jnp.min(jnp.where(posm >= 0, posm, cap).reshape(gsz, e, nkt, tk), axis=3)
    first = jnp.where(first >= cap, 0, first).reshape(-1)
    return pl.pallas_call(
        functools.partial(_ffn_kernel, cap, win, tk, fk),
        grid_spec=pltpu.PrefetchScalarGridSpec(
            num_scalar_prefetch=1,
            grid=(e, gsz),
            in_specs=[
                pl.BlockSpec((1, t, d), lambda ei, gi, lo: (gi, 0, 0)),
                pl.BlockSpec((1, 1, 1, t), lambda ei, gi, lo: (gi, ei, 0, 0)),
                pl.BlockSpec((1, 1, 1, t), lambda ei, gi, lo: (gi, ei, 0, 0)),
                pl.BlockSpec((1, 1, d, f), lambda ei, gi, lo: (layer, ei, 0, 0)),
                pl.BlockSpec((1, 1, d, f), lambda ei, gi, lo: (layer, ei, 0, 0)),
                pl.BlockSpec((1, 1, f, d), lambda ei, gi, lo: (layer, ei, 0, 0)),
            ],
            out_specs=pl.BlockSpec((1, 1, cap, d), lambda ei, gi, lo: (gi, ei, 0, 0)),
            scratch_shapes=[pltpu.VMEM((cap, d), F32), pltpu.VMEM((cap, SUBLANES), F32)],
        ),
        out_shape=jax.ShapeDtypeStruct((gsz, e, cap, d), BF16),
        name="moe_ffn",
        compiler_params=_cparams(("arbitrary", "arbitrary")),
    )(first, h, posm, gate, w1, w3, w2)


def _combine_kernel(win, tail, lo_ref, hi_ref, x_ref, post_ref, ys_ref, mod_ref, o_ref):
    tt = x_ref.shape[1]
    e = ys_ref.shape[1]
    cap = ys_ref.shape[2]
    base = (pl.program_id(0) * pl.num_programs(1) + pl.program_id(1)) * e
    slot = lax.broadcasted_iota(I32, (tt, win), 1)
    def window_start(ei):
        lo = lo_ref[base + ei]
        return pl.multiple_of(jnp.minimum((lo // BF16_ROWS) * BF16_ROWS, cap - win - tail), BF16_ROWS)

    acc = jnp.zeros(x_ref.shape[1:], F32)
    for ei in range(e):
        start = window_start(ei)
        onehot = jnp.where(post_ref[0, :, ei:ei + 1] - start == slot, 1.0, 0.0).astype(BF16)
        acc = acc + _dot(onehot, ys_ref[0, ei, pl.ds(start, win), :])
    o_ref[0] = x_ref[0] + mod_ref[0, 0, 5:6, :] * acc
    for ei in range(e if tail else 0):
        tstart = pl.multiple_of(window_start(ei) + win, BF16_ROWS)

        @pl.when(hi_ref[base + ei] > tstart)
        def _tail():
            tslot = lax.broadcasted_iota(I32, (tt, tail), 1)
            oh = jnp.where(post_ref[0, :, ei:ei + 1] - tstart == tslot, 1.0, 0.0).astype(BF16)
            o_ref[0] += mod_ref[0, 0, 5:6, :] * _dot(oh, ys_ref[0, ei, pl.ds(tstart, tail), :])


def _combine(xc, post, ys, mod, tile0, n_tiles, shared_slots):
    bsz, tc, d = xc.shape
    tt = TOKEN_TILE
    _, e, cap, _ = ys.shape
    is_ctx = 1 if tile0 > 0 else 0
    ys_map = (lambda b, j, lo, hi: (0, 0, 0, 0)) if shared_slots else (lambda b, j, lo, hi: (b, 0, 0, 0))
    win = min(cap, tt)
    tail = BF16_ROWS if cap >= tt + BF16_ROWS else 0
    assert tail > 0 or win == cap
    assert (cap - win - tail) % BF16_ROWS == 0
    pr = post.reshape(bsz, n_tiles, tt, e)
    first = jnp.min(jnp.where(pr >= 0, pr, cap), axis=2)
    first = jnp.where(first >= cap, 0, first).reshape(-1)
    last = (jnp.max(pr, axis=2) + 1).reshape(-1)
    return pl.pallas_call(
        functools.partial(_combine_kernel, win, tail),
        grid_spec=pltpu.PrefetchScalarGridSpec(
            num_scalar_prefetch=2,
            grid=(bsz, n_tiles),
            in_specs=[
                pl.BlockSpec((1, tt, d), lambda b, j, lo, hi: (b, tile0 + j, 0)),
                pl.BlockSpec((1, tt, e), lambda b, j, lo, hi: (b, j, 0)),
                pl.BlockSpec((1, e, cap, d), ys_map),
                pl.BlockSpec((1, 1, 6, d), lambda b, j, lo, hi: (b, is_ctx, 0, 0)),
            ],
            out_specs=pl.BlockSpec((1, tt, d), lambda b, j, lo, hi: (b, tile0 + j, 0)),
        ),
        out_shape=jax.ShapeDtypeStruct((bsz, tc, d), F32),
        input_output_aliases={2: 0},
        name="moe_combine",
        compiler_params=_cparams(("parallel", "arbitrary")),
    )(first, last, xc, post, ys, mod)


def _moe(xc, mod, gn, router_t, layer, w1, w3, w2, nl, nct, seq, ctx_len, do_ctx):
    bsz, tc, d = xc.shape
    e = router_t.shape[0]
    h, aff = _router(xc, mod, gn, router_t, 0, nl)
    cap_l = CAPACITY_FACTOR * seq // e
    pos, post, gate = _select(aff, cap_l, 0)
    ys = _ffn(h, pos.reshape(bsz, e, 1, seq), gate.reshape(bsz, e, 1, seq), layer, w1, w3, w2, cap_l)
    xc = _combine(xc, post, ys, mod, 0, nl, False)
    if do_ctx:
        h_ctx, aff_ctx = _router(xc, mod, gn, router_t, nl, nct)
        cap_c = CAPACITY_FACTOR * ctx_len // e
        pos, post, gate = _select(aff_ctx, cap_c, cap_c)
        pos = jnp.transpose(pos, (1, 0, 2)).reshape(1, e, 1, bsz * ctx_len)
        gate = jnp.transpose(gate, (1, 0, 2)).reshape(1, e, 1, bsz * ctx_len)
        ys = _ffn(h_ctx.reshape(1, bsz * ctx_len, d), pos, gate, layer, w1, w3, w2, bsz * cap_c)
        xc = _combine(xc, post, ys, mod, nl, nct, True)
    return xc


def _final_kernel(x_ref, g_ref, o_ref):
    x = x_ref[0]
    ms = jnp.mean(x * x, axis=-1, keepdims=True)
    o_ref[0] = x * lax.rsqrt(ms + RMS_EPS) * g_ref[...]


def _final_norm(xc, g, seq):
    bsz, tc, d = xc.shape
    tt = TOKEN_TILE
    tile = pl.BlockSpec((1, tt, d), lambda b, j: (b, j, 0))
    return pl.pallas_call(
        _final_kernel,
        grid=(bsz, seq // tt),
        in_specs=[tile, pl.BlockSpec(g.shape, lambda b, j: (0, 0))],
        out_specs=tile,
        out_shape=jax.ShapeDtypeStruct((bsz, seq, d), F32),
        name="final_norm",
        compiler_params=_cparams(("parallel", "arbitrary")),
    )(xc, g)


def kernel(x, c, ctx, c_ctx, ada_w, ada_b, norm1_g, norm2_g, rwkv_mix, rwkv_wrkv, rwkv_w0, rwkv_w1, rwkv_w2, rwkv_a0, rwkv_a1, rwkv_a2, rwkv_v0, rwkv_v1, rwkv_v2, rwkv_g1, rwkv_g2, rwkv_kk, rwkv_ka, rwkv_rk, rwkv_lnw, rwkv_lnb, rwkv_wo, pool_w, pool_scale, moe_router, moe_w1, moe_w3, moe_w2, final_g):
    bsz, seq, d = x.shape
    ctx_len = ctx.shape[1]
    depth = ada_w.shape[0]
    n_heads, head = rwkv_rk.shape[1], rwkv_rk.shape[2]
    n_mixers = 2
    tt = TOKEN_TILE
    assert seq % tt == 0 and ctx_len % tt == 0 and tt % GRID_W == 0
    assert head == WKV_CHUNK and 2 * head == LANES and d % LANES == 0
    nl, nct = seq // tt, ctx_len // tt

    xc = jnp.concatenate([x, ctx], axis=1)

    rows = -(-(bsz + 1) // SUBLANES) * SUBLANES
    cond = jnp.zeros((rows, d), F32).at[:bsz].set(c).at[bsz].set(c_ctx)
    m_all = _ada_all(cond, ada_w, ada_b)
    m_lat = m_all[:, :bsz].reshape(depth, bsz, 1, 6, d)
    m_ctx = jnp.broadcast_to(m_all[:, bsz].reshape(depth, 1, 1, 6, d), (depth, bsz, 1, 6, d))
    mods = jnp.concatenate([m_lat, m_ctx], axis=2)

    head_of = jnp.arange(d) // head
    hs = (head_of[:, None] == jnp.arange(HEAD_COLS)[None, :]).astype(BF16)
    hst = jnp.transpose(hs)

    w1_all, w3_all, w2_all = moe_w1.astype(BF16), moe_w3.astype(BF16), moe_w2.astype(BF16)
    v_first = None
    for i in range(depth):
        last = i == depth - 1
        is_rwkv = i % n_mixers == 0
        jn = i // n_mixers
        mod = mods[i]
        gn1 = norm1_g[i].reshape(1, d)
        if is_rwkv:
            p = {
                'mix': rwkv_mix[jn],
                'wr': rwkv_wrkv[jn, 0].astype(BF16), 'wk': rwkv_wrkv[jn, 1].astype(BF16),
                'wv': rwkv_wrkv[jn, 2].astype(BF16),
                'g1': rwkv_g1[jn].astype(BF16), 'g2': rwkv_g2[jn].astype(BF16),
                'w1': rwkv_w1[jn].astype(BF16), 'w2': rwkv_w2[jn].astype(BF16), 'w0': rwkv_w0[jn],
                'a1': rwkv_a1[jn].astype(BF16), 'a2': rwkv_a2[jn].astype(BF16), 'a0': rwkv_a0[jn],
                'k_k': rwkv_kk[jn].reshape(1, d), 'k_a': rwkv_ka[jn].reshape(1, d),
                'r_k': rwkv_rk[jn].reshape(1, d), 'ln_w': rwkv_lnw[jn].reshape(1, d),
                'ln_b': rwkv_lnb[jn].reshape(1, d), 'wo': rwkv_wo[jn].astype(BF16),
                'hs': hs, 'hst': hst,
            }
            vres = None if jn == 0 else (rwkv_v0[jn - 1].reshape(1, d), rwkv_v1[jn - 1].astype(BF16),
                                         rwkv_v2[jn - 1].astype(BF16))
            r, v, g, kk, lw0, lw1, b0, b1, kd0, kd1 = _rwkv_proj(xc, mod, gn1, p, vres, v_first, nl, seq, ctx_len)
            if v_first is None:
                v_first = v
            yf, yr = _wkv(r, v, kk, lw0, b0, kd0, lw1, b1, kd1, seq, ctx_len)
            xc = _rwkv_out(xc, yf, yr, r, v, g, kd0, kd1, mod, p, nl, head)
        else:
            n_tiles = nl if last else nl + nct
            xc = _pool(xc, mod, gn1, pool_w[jn].astype(BF16), pool_scale[jn].reshape(1, d),
                       nl, seq, ctx_len, n_tiles)
        xc = _moe(xc, mod, norm2_g[i].reshape(1, d), jnp.transpose(moe_router[i]),
                  i, w1_all, w3_all, w2_all, nl, nct, seq, ctx_len, not last)
    return _final_norm(xc, final_g.reshape(1, d), seq)
```

```python
import functools
import math

import jax
import jax.numpy as jnp
from jax import lax
from jax.experimental import pallas as pl
from jax.experimental.pallas import tpu as pltpu

F32 = jnp.float32
BF16 = jnp.bfloat16
I32 = jnp.int32
HIGHEST = lax.Precision.HIGHEST

GRID_W = 64
POOL_WINDOWS = (2, 4, 8, 16)
CAPACITY_FACTOR = 2
RMS_EPS = 1e-6
GN_EPS = 64e-5
EXP_NEG_HALF = math.exp(-0.5)

LANES = 128
SUBLANES = 8
BF16_ROWS = 16
VMEM_LIMIT_BYTES = 56 * 1024 * 1024

TOKEN_TILE = 256
GATHER_TILE = 256
GATHER_FAST_ROWS = 72
COMBINE_PACK = 4
COMBINE_FAST_SLOTS = 64
WKV_CHUNK = 64
WKV_PAIRS_PER_STEP = 8
POOL_HALO = 8
HEAD_COLS = LANES


def _cparams(sem):
    return pltpu.CompilerParams(dimension_semantics=sem, vmem_limit_bytes=VMEM_LIMIT_BYTES)


def _dot(a, b):
    return jnp.dot(a, b, preferred_element_type=F32)


def _dot_nt(a, b):
    return lax.dot_general(a, b, (((1,), (1,)), ((), ())), preferred_element_type=F32)


def _dot_tn(a, b):
    return lax.dot_general(a, b, (((0,), (0,)), ((), ())), preferred_element_type=F32)


def _split3(x):
    hi = x.astype(BF16)
    r1 = x - hi.astype(F32)
    mid = r1.astype(BF16)
    lo = (r1 - mid.astype(F32)).astype(BF16)
    return hi, mid, lo


def _dot_split(x, m, pieces):
    hi = x.astype(BF16)
    out = _dot(hi, m)
    for _ in range(pieces - 1):
        x = x - hi.astype(F32)
        hi = x.astype(BF16)
        out = out + _dot(hi, m)
    return out


def _dot3_left(m, x):
    w = x.shape[1]
    z = _dot(m, jnp.concatenate(_split3(x), axis=1))
    return z[:, 0:w] + z[:, w:2 * w] + z[:, 2 * w:3 * w]


def _sigmoid(x):
    return 0.5 * jnp.tanh(0.5 * x) + 0.5


def _norm_mod(x, g, shift, scale):
    ms = jnp.mean(x * x, axis=-1, keepdims=True)
    return x * lax.rsqrt(ms + RMS_EPS) * (g * (1.0 + scale)) + shift


def _ada_kernel(c_ref, w_ref, b_ref, o_ref):
    c = c_ref[...]
    s = c * _sigmoid(c)
    o_ref[0] = jnp.dot(s, w_ref[0], precision=HIGHEST, preferred_element_type=F32) + b_ref[0]


def _ada_all(cond, ada_w, ada_b):
    depth, d, n6 = ada_w.shape
    rows = cond.shape[0]
    nt = 512
    return pl.pallas_call(
        _ada_kernel,
        grid=(depth, n6 // nt),
        in_specs=[
            pl.BlockSpec((rows, d), lambda i, n: (0, 0)),
            pl.BlockSpec((1, d, nt), lambda i, n: (i, 0, n)),
            pl.BlockSpec((1, 1, nt), lambda i, n: (i, 0, n)),
        ],
        out_specs=pl.BlockSpec((1, rows, nt), lambda i, n: (i, 0, n)),
        out_shape=jax.ShapeDtypeStruct((depth, rows, n6), F32),
        name="ada_mod",
        compiler_params=_cparams(("arbitrary", "arbitrary")),
    )(cond, ada_w, ada_b.reshape(depth, 1, n6))


def _rwkv_proj_kernel(nl, seq, ctx_len, has_vres, *refs):
    if has_vres:
        (x_ref, xp_ref, xn_ref, mod_ref, gn_ref, mix_ref, wr_ref, wk_ref, wv_ref, g1_ref, g2_ref,
         w1_ref, w2_ref, w0_ref, a1_ref, a2_ref, a0_ref, kkw_ref, kaw_ref, hs_ref, hst_ref,
         v0_ref, v1_ref, v2_ref, vf_ref,
         r_o, v_o, g_o, kk_o, lw0_o, lw1_o, b0_o, b1_o, kd0_o, kd1_o, ext_scr, sh_scr) = refs
    else:
        (x_ref, xp_ref, xn_ref, mod_ref, gn_ref, mix_ref, wr_ref, wk_ref, wv_ref, g1_ref, g2_ref,
         w1_ref, w2_ref, w0_ref, a1_ref, a2_ref, a0_ref, kkw_ref, kaw_ref, hs_ref, hst_ref,
         r_o, v_o, g_o, kk_o, lw0_o, lw1_o, b0_o, b1_o, kd0_o, kd1_o, ext_scr, sh_scr) = refs
    tt = x_ref.shape[1]
    d = x_ref.shape[2]
    hw = GRID_W
    j = pl.program_id(1)
    shift = mod_ref[0, 0, 0:1, :]
    scale = mod_ref[0, 0, 1:2, :]
    gn = gn_ref[...]
    ext_scr[0:hw, :] = _norm_mod(xp_ref[0], gn, shift, scale)
    ext_scr[hw:hw + tt, :] = _norm_mod(x_ref[0], gn, shift, scale)
    ext_scr[hw + tt:, :] = _norm_mod(xn_ref[0], gn, shift, scale)

    i = lax.broadcasted_iota(I32, (tt, 1), 0)
    q = d // 4

    @pl.when(j < nl)
    def _latent_shift():
        t = j * tt + i
        col = i % hw
        sh_scr[:, 0:q] = jnp.where(col != 0, ext_scr[hw - 1:hw - 1 + tt, 0:q], 0.0)
        sh_scr[:, q:2 * q] = jnp.where(col != hw - 1, ext_scr[hw + 1:hw + 1 + tt, q:2 * q], 0.0)
        sh_scr[:, 2 * q:3 * q] = jnp.where(t >= hw, ext_scr[0:tt, 2 * q:3 * q], 0.0)
        sh_scr[:, 3 * q:] = jnp.where(t < seq - hw, ext_scr[2 * hw:2 * hw + tt, 3 * q:], 0.0)

    @pl.when(j >= nl)
    def _context_shift():
        t = (j - nl) * tt + i
        hd = d // 2
        sh_scr[:, 0:hd] = jnp.where(t != 0, ext_scr[hw - 1:hw - 1 + tt, 0:hd], 0.0)
        sh_scr[:, hd:] = jnp.where(t != ctx_len - 1, ext_scr[hw + 1:hw + 1 + tt, hd:], 0.0)

    h = ext_scr[hw:hw + tt, :]
    xx = sh_scr[...] - h

    def mixed(n):
        return (h + xx * mix_ref[n:n + 1, :]).astype(BF16)

    xr, xw, xk, xv, xa, xg = [mixed(n) for n in range(6)]
    r = _dot(xr, wr_ref[...])
    k = _dot(xk, wk_ref[...])
    v = _dot(xv, wv_ref[...])
    if has_vres:
        lor = _dot(_dot(xv, v1_ref[...]).astype(BF16), v2_ref[...])
        v = v + (vf_ref[0] - v) * _sigmoid(v0_ref[...] + lor)
    g = _dot(_sigmoid(_dot(xg, g1_ref[...])).astype(BF16), g2_ref[...])
    r_o[0] = r
    v_o[0] = v
    g_o[0] = g

    kkr = k * kkw_ref[...]
    ss = _dot_split(kkr * kkr, hs_ref[...], 1)
    inv = 1.0 / jnp.maximum(jnp.sqrt(ss), 1e-12)
    kk = kkr * _dot_split(inv, hst_ref[...], 2)
    kk_o[0] = kk
    kaw = kaw_ref[...]
    for dr, (lw_o, b_o, kd_o) in enumerate(((lw0_o, b0_o, kd0_o), (lw1_o, b1_o, kd1_o))):
        wpre = w0_ref[dr:dr + 1, :] + _dot(jnp.tanh(_dot(xw, w1_ref[dr])).astype(BF16), w2_ref[dr])
        lw_o[0] = -EXP_NEG_HALF * _sigmoid(wpre)
        a = _sigmoid(a0_ref[dr:dr + 1, :] + _dot(_dot(xa, a1_ref[dr]).astype(BF16), a2_ref[dr]))
        b_o[0] = kk * a
        kd_o[0] = k * (1.0 + (a - 1.0) * kaw)


def _rwkv_proj(xc, mod, gn, p, vres, v_first, nl, seq, ctx_len):
    bsz, tc, d = xc.shape
    tt = TOKEN_TILE
    nt = tc // tt
    hb = tt // GRID_W
    nhb = tc // GRID_W
    has_vres = vres is not None

    def full(a):
        nd = a.ndim
        return pl.BlockSpec(a.shape, lambda b, j, _n=nd: (0,) * _n)

    tile = pl.BlockSpec((1, tt, d), lambda b, j: (b, j, 0))
    ins = [xc, xc, xc, mod, gn, p['mix'], p['wr'], p['wk'], p['wv'], p['g1'], p['g2'],
           p['w1'], p['w2'], p['w0'], p['a1'], p['a2'], p['a0'], p['k_k'], p['k_a'], p['hs'], p['hst']]
    specs = [
        tile,
        pl.BlockSpec((1, GRID_W, d), lambda b, j: (b, jnp.maximum(j * hb - 1, 0), 0)),
        pl.BlockSpec((1, GRID_W, d), lambda b, j: (b, jnp.minimum((j + 1) * hb, nhb - 1), 0)),
        pl.BlockSpec((1, 1, 6, d), lambda b, j: (b, (j >= nl).astype(I32), 0, 0)),
    ] + [full(a) for a in ins[4:]]
    if has_vres:
        ins += [vres[0], vres[1], vres[2], v_first]
        specs += [full(vres[0]), full(vres[1]), full(vres[2]), tile]
    out_sds = jax.ShapeDtypeStruct((bsz, tc, d), F32)
    return pl.pallas_call(
        functools.partial(_rwkv_proj_kernel, nl, seq, ctx_len, has_vres),
        grid=(bsz, nt),
        in_specs=specs,
        out_specs=[tile] * 10,
        out_shape=[out_sds] * 10,
        scratch_shapes=[pltpu.VMEM((tt + 2 * GRID_W, d), F32), pltpu.VMEM((tt, d), F32)],
        name="rwkv_proj",
        compiler_params=_cparams(("parallel", "arbitrary")),
    )(*ins)


def _wkv_chains(chains, masks):
    nc = len(chains)
    ks = range(nc)
    L = chains[0][0].shape[0]
    r, v, kk, lw, b, kd, s_prev, rev = [[ch[i] for ch in chains] for i in range(8)]
    mk = [masks[1] if rv else masks[0] for rv in rev]
    tri, m0, m1, strict, incl, eye2, bd = [[m[i] for m in mk] for i in range(7)]

    def stack(k, x):
        xb = x.astype(BF16)
        return jnp.concatenate([xb * m0[k], xb * m1[k]], axis=0)

    c = [_dot3_left(tri[k], lw[k]) for k in ks]
    ctot = [c[k][0:1, :] if rev[k] else c[k][L - 1:L, :] for k in ks]
    e_c = [jnp.exp(c[k]) for k in ks]
    e_nc = [jnp.exp(-c[k]) for k in ks]
    e_tc = [jnp.exp(ctot[k] - c[k]) for k in ks]
    ah = [-kk[k] * jnp.exp(c[k] - lw[k]) for k in ks]
    rh = [r[k] * e_c[k] for k in ks]
    lhs = [jnp.concatenate([ah[k], rh[k]], axis=0).astype(BF16) for k in ks]
    rhs = [jnp.concatenate([stack(k, b[k] * e_nc[k]), stack(k, kd[k] * e_nc[k])], axis=0) for k in ks]
    aa = [_dot_nt(lhs[k], rhs[k]) for k in ks]
    a_ab = [jnp.where(strict[k], aa[k][0:L, 0:2 * L], 0.0) for k in ks]
    a_ak = [jnp.where(strict[k], aa[k][0:L, 2 * L:4 * L], 0.0).astype(BF16) for k in ks]
    a_r = [jnp.where(jnp.concatenate([incl[k], incl[k]], axis=1), aa[k][L:2 * L, :], 0.0).astype(BF16)
           for k in ks]

    n_dbl = int(math.log2(L))
    tm = [eye2[k] + a_ab[k] for k in ks]
    pw = [_dot(a_ab[k].astype(BF16), stack(k, a_ab[k])) for k in ks]
    for _ in range(n_dbl - 2):
        z = [_dot(pw[k].astype(BF16), jnp.concatenate([stack(k, tm[k]), stack(k, pw[k])], axis=1)) for k in ks]
        tm = [tm[k] + z[k][:, 0:2 * L] for k in ks]
        pw = [z[k][:, 2 * L:4 * L] for k in ks]
    tm = [tm[k] + _dot(pw[k].astype(BF16), stack(k, tm[k])) for k in ks]

    ss0 = [_dot_nt(lhs[k], s_prev[k].astype(BF16)) for k in ks]
    vs = [stack(k, v[k]) for k in ks]
    wmat = [ss0[k][0:L] + _dot(a_ak[k], vs[k]) for k in ks]
    u = [_dot(tm[k].astype(BF16), stack(k, wmat[k])) for k in ks]
    y = [ss0[k][L:2 * L] + _dot(a_r[k], jnp.concatenate([stack(k, u[k]), vs[k]], axis=0)) for k in ks]
    uv = [jnp.concatenate([u[k], v[k]], axis=0).astype(BF16) for k in ks]
    bk = [jnp.concatenate([b[k] * e_tc[k], kd[k] * e_tc[k]], axis=0).astype(BF16) for k in ks]
    upd = [_dot_tn(uv[k], bk[k]) for k in ks]
    s_new = [s_prev[k] * jnp.exp(ctot[k]) + jnp.where(bd[k], upd[k], 0.0) for k in ks]
    return y, s_new


def _wkv_masks(L, w2, reverse):
    n = w2 // 2
    row = lax.broadcasted_iota(I32, (L, L), 0)
    colm = lax.broadcasted_iota(I32, (L, L), 1)
    tri = jnp.where((colm >= row) if reverse else (colm <= row), 1.0, 0.0).astype(BF16)
    lane = lax.broadcasted_iota(I32, (1, w2), 1)
    m0 = jnp.where(lane < n, 1.0, 0.0).astype(BF16)
    m1 = jnp.where(lane < n, 0.0, 1.0).astype(BF16)
    t_i = lax.broadcasted_iota(I32, (L, 2 * L), 0)
    s_i = lax.broadcasted_iota(I32, (L, 2 * L), 1) % L
    strict = (s_i > t_i) if reverse else (s_i < t_i)
    incl = (s_i >= t_i) if reverse else (s_i <= t_i)
    eye2 = jnp.where(s_i == t_i, 1.0, 0.0)
    ri = lax.broadcasted_iota(I32, (w2, w2), 0)
    ci = lax.broadcasted_iota(I32, (w2, w2), 1)
    bd = (ri < n) == (ci < n)
    return tri, m0, m1, strict, incl, eye2, bd


def _wkv_kernel(rf, vf, kkf, lwf, bf, kdf, rr, vr, kkr, lwr, br, kdr, yf_o, yr_o, s_scr):
    @pl.when(pl.program_id(2) == 0)
    def _init():
        s_scr[...] = jnp.zeros_like(s_scr)

    L = rf.shape[1]
    masks = (_wkv_masks(L, LANES, False), _wkv_masks(L, LANES, True))
    chains = []
    for hp in range(rf.shape[2] // LANES):
        ls = slice(hp * LANES, (hp + 1) * LANES)
        chains.append((rf[0, :, ls], vf[0, :, ls], kkf[0, :, ls], lwf[0, :, ls], bf[0, :, ls],
                       kdf[0, :, ls], s_scr[0, hp], False))
        chains.append((rr[0, :, ls], vr[0, :, ls], kkr[0, :, ls], lwr[0, :, ls], br[0, :, ls],
                       kdr[0, :, ls], s_scr[1, hp], True))
    ys, ss = _wkv_chains(chains, masks)
    for hp in range(rf.shape[2] // LANES):
        ls = slice(hp * LANES, (hp + 1) * LANES)
        yf_o[0, :, ls] = ys[2 * hp]
        yr_o[0, :, ls] = ys[2 * hp + 1]
        s_scr[0, hp] = ss[2 * hp]
        s_scr[1, hp] = ss[2 * hp + 1]


def _wkv(r, v, kk, lw0, b0, kd0, lw1, b1, kd1, seq, ctx_len):
    bsz, tc, d = r.shape
    L = WKV_CHUNK
    nlc = seq // L
    ncc = ctx_len // L
    nch = nlc + ncc
    gp = min(WKV_PAIRS_PER_STEP, d // LANES)
    width = gp * LANES

    def fwd_map(b, h, c):
        return (b, jnp.where(c < ncc, nlc + c, c - ncc), h)

    def rev_map(b, h, c):
        return (b, jnp.where(c < ncc, nlc + ncc - 1 - c, nlc - 1 - (c - ncc)), h)

    fs = pl.BlockSpec((1, L, width), fwd_map)
    rs = pl.BlockSpec((1, L, width), rev_map)
    sds = jax.ShapeDtypeStruct((bsz, tc, d), F32)
    return pl.pallas_call(
        _wkv_kernel,
        grid=(bsz, d // width, nch),
        in_specs=[fs] * 6 + [rs] * 6,
        out_specs=[fs, rs],
        out_shape=[sds, sds],
        scratch_shapes=[pltpu.VMEM((2, gp, LANES, LANES), F32)],
        name="wkv_scan",
        compiler_params=_cparams(("parallel", "parallel", "arbitrary")),
    )(r, v, kk, lw0, b0, kd0, r, v, kk, lw1, b1, kd1)


def _rwkv_out_kernel(head, x_ref, yf_ref, yr_ref, r_ref, v_ref, g_ref, kd0_ref, kd1_ref, mod_ref,
                     rk_ref, lnw_ref, lnb_ref, wo_ref, hs_ref, hst_ref, o_ref):
    hs = hs_ref[...]
    hst = hst_ref[...]
    o = yf_ref[0] + yr_ref[0]
    inv_n = 1.0 / head
    mu = _dot_split(_dot_split(o, hs, 2) * inv_n, hst, 2)
    dlt = o - mu
    var = _dot_split(dlt * dlt, hs, 1) * inv_n
    on = dlt * _dot_split(lax.rsqrt(var + GN_EPS), hst, 2)
    on = on * lnw_ref[...] + lnb_ref[...]
    kb = 0.5 * kd0_ref[0] + 0.5 * kd1_ref[0]
    bonus = _dot_split(_dot_split(r_ref[0] * kb * rk_ref[...], hs, 1), hst, 2) * v_ref[0]
    y = ((on + bonus) * g_ref[0]).astype(BF16)
    gate = mod_ref[0, 0, 2:3, :]
    o_ref[0] = x_ref[0] + gate * _dot(y, wo_ref[...])


def _rwkv_out(xc, yf, yr, r, v, g, kd0, kd1, mod, p, nl, head):
    bsz, tc, d = xc.shape
    tt = TOKEN_TILE
    tile = pl.BlockSpec((1, tt, d), lambda b, j: (b, j, 0))

    def full(a):
        nd = a.ndim
        return pl.BlockSpec(a.shape, lambda b, j, _n=nd: (0,) * _n)

    consts = [p['r_k'], p['ln_w'], p['ln_b'], p['wo'], p['hs'], p['hst']]
    return pl.pallas_call(
        functools.partial(_rwkv_out_kernel, head),
        grid=(bsz, tc // tt),
        in_specs=[tile] * 8 + [pl.BlockSpec((1, 1, 6, d), lambda b, j: (b, (j >= nl).astype(I32), 0, 0))]
        + [full(a) for a in consts],
        out_specs=tile,
        out_shape=jax.ShapeDtypeStruct((bsz, tc, d), F32),
        input_output_aliases={0: 0},
        name="rwkv_out",
        compiler_params=_cparams(("parallel", "arbitrary")),
    )(xc, yf, yr, r, v, g, kd0, kd1, mod, *consts)


def _pool_kernel(nl, seq, ctx_len, x_ref, xp_ref, xn_ref, mod_ref, gn_ref, w_ref, sc_ref, o_ref, ext_scr):
    tt = x_ref.shape[1]
    d = x_ref.shape[2]
    ph = POOL_HALO
    j = pl.program_id(1)
    nct = ctx_len // tt
    shift = mod_ref[0, 0, 0:1, :]
    scale = mod_ref[0, 0, 1:2, :]
    gn = gn_ref[...]
    first = jnp.logical_or(j == 0, j == nl)
    last = jnp.logical_or(j == nl - 1, j == nl + nct - 1)
    hp = _norm_mod(xp_ref[0], gn, shift, scale)
    hn = _norm_mod(xn_ref[0], gn, shift, scale)
    ext_scr[0:ph, :] = jnp.where(first, 0.0, hp)
    ext_scr[ph:ph + tt, :] = _norm_mod(x_ref[0], gn, shift, scale)
    ext_scr[ph + tt:, :] = jnp.where(last, 0.0, hn)

    i = lax.broadcasted_iota(I32, (tt, 1), 0)
    t = jnp.where(j < nl, j * tt + i, (j - nl) * tt + i)
    tseg = jnp.where(j < nl, seq, ctx_len)
    ng = len(POOL_WINDOWS)
    dg = d // ng
    gate = mod_ref[0, 0, 2:3, :]
    for gi, win in enumerate(POOL_WINDOWS):
        half = win // 2
        cs = slice(gi * dg, (gi + 1) * dg)
        acc = ext_scr[ph - half:ph - half + tt, cs]
        for o in range(1, win):
            acc = acc + ext_scr[ph - half + o:ph - half + o + tt, cs]
        cnt = (jnp.minimum(t + half, tseg) - jnp.maximum(t - half, 0)).astype(F32)
        hg = ext_scr[ph:ph + tt, cs]
        dlt = (acc / cnt - hg).astype(BF16)
        y = _dot(dlt, w_ref[gi]) * sc_ref[:, cs]
        o_ref[0, :, cs] = x_ref[0, :, cs] + gate[:, cs] * y


def _pool(xc, mod, gn, w, sc, nl, seq, ctx_len, n_tiles):
    bsz, tc, d = xc.shape
    tt = TOKEN_TILE
    hb = tt // POOL_HALO
    nhb = tc // POOL_HALO
    tile = pl.BlockSpec((1, tt, d), lambda b, j: (b, j, 0))
    return pl.pallas_call(
        functools.partial(_pool_kernel, nl, seq, ctx_len),
        grid=(bsz, n_tiles),
        in_specs=[
            tile,
            pl.BlockSpec((1, POOL_HALO, d), lambda b, j: (b, jnp.maximum(j * hb - 1, 0), 0)),
            pl.BlockSpec((1, POOL_HALO, d), lambda b, j: (b, jnp.minimum((j + 1) * hb, nhb - 1), 0)),
            pl.BlockSpec((1, 1, 6, d), lambda b, j: (b, (j >= nl).astype(I32), 0, 0)),
            pl.BlockSpec(gn.shape, lambda b, j: (0, 0)),
            pl.BlockSpec(w.shape, lambda b, j: (0, 0, 0)),
            pl.BlockSpec(sc.shape, lambda b, j: (0, 0)),
        ],
        out_specs=tile,
        out_shape=jax.ShapeDtypeStruct((bsz, n_tiles * tt, d), F32),
        scratch_shapes=[pltpu.VMEM((tt + 2 * POOL_HALO, d), F32)],
        name="pool_mix",
        compiler_params=_cparams(("parallel", "arbitrary")),
    )(xc, xc, xc, mod, gn, w, sc)


def _router_kernel(x_ref, mod_ref, gn_ref, rt_ref, h_o, aff_o):
    h = _norm_mod(x_ref[0], gn_ref[...], mod_ref[0, 0, 3:4, :], mod_ref[0, 0, 4:5, :])
    h_o[0] = h.astype(BF16)
    logits = lax.dot_general(rt_ref[...], h, (((1,), (1,)), ((), ())),
                             precision=HIGHEST, preferred_element_type=F32)
    m = jnp.max(logits, axis=0, keepdims=True)
    ex = jnp.exp(logits - m)
    aff_o[0] = ex / jnp.sum(ex, axis=0, keepdims=True)


def _router(xc, mod, gn, router_t, tile0, n_tiles):
    bsz, tc, d = xc.shape
    tt = TOKEN_TILE
    e = router_t.shape[0]
    is_ctx = 1 if tile0 > 0 else 0
    return pl.pallas_call(
        _router_kernel,
        grid=(bsz, n_tiles),
        in_specs=[
            pl.BlockSpec((1, tt, d), lambda b, j: (b, tile0 + j, 0)),
            pl.BlockSpec((1, 1, 6, d), lambda b, j: (b, is_ctx, 0, 0)),
            pl.BlockSpec(gn.shape, lambda b, j: (0, 0)),
            pl.BlockSpec(router_t.shape, lambda b, j: (0, 0)),
        ],
        out_specs=[pl.BlockSpec((1, tt, d), lambda b, j: (b, j, 0)),
                   pl.BlockSpec((1, e, tt), lambda b, j: (b, 0, j))],
        out_shape=[jax.ShapeDtypeStruct((bsz, n_tiles * tt, d), BF16),
                   jax.ShapeDtypeStruct((bsz, e, n_tiles * tt), F32)],
        name="moe_router",
        compiler_params=_cparams(("parallel", "arbitrary")),
    )(xc, mod, gn, router_t)


def _lane_cumsum_excl(x, utri):
    e, t = x.shape
    off = jnp.zeros((e, 1), F32)
    parts = []
    for kb in range(t // LANES):
        blk = x[:, kb * LANES:(kb + 1) * LANES]
        inc = _dot(blk.astype(BF16), utri)
        parts.append(inc - blk + off)
        off = off + inc[:, LANES - 1:LANES]
    return jnp.concatenate(parts, axis=1), off


def _select_kernel(cap, slot_stride, aff_ref, pos_o, post_o, gate_o):
    a = aff_ref[0]
    e, t = a.shape
    bits = pltpu.bitcast(a, I32)

    def body(it, thr):
        cand = thr | lax.shift_left(jnp.int32(1), jnp.int32(29) - it)
        cnt = jnp.sum(jnp.where(bits >= cand, 1.0, 0.0), axis=1, keepdims=True)
        return jnp.where(cnt >= cap, cand, thr)

    thr = lax.fori_loop(0, 30, body, jnp.zeros((e, 1), I32))
    gt = jnp.where(bits > thr, 1.0, 0.0)
    eq = jnp.where(bits == thr, 1.0, 0.0)
    ri = lax.broadcasted_iota(I32, (LANES, LANES), 0)
    ci = lax.broadcasted_iota(I32, (LANES, LANES), 1)
    utri = jnp.where(ri <= ci, 1.0, 0.0).astype(BF16)
    n_gt = jnp.sum(gt, axis=1, keepdims=True)
    eq_rank, _ = _lane_cumsum_excl(eq, utri)
    sel = gt + eq * jnp.where(eq_rank < cap - n_gt, 1.0, 0.0)
    pos, _ = _lane_cumsum_excl(sel, utri)
    base = pl.program_id(0) * slot_stride
    posm = jnp.where(sel > 0.5, pos.astype(I32) + base, -1)
    pos_o[0] = posm
    post_o[0] = jnp.transpose(posm.astype(F32)).astype(I32)
    gate_o[0] = jnp.where(sel > 0.5, a, 0.0)


def _select(aff, cap, slot_stride):
    bsz, e, t = aff.shape
    blk = pl.BlockSpec((1, e, t), lambda b: (b, 0, 0))
    return pl.pallas_call(
        functools.partial(_select_kernel, cap, slot_stride),
        grid=(bsz,),
        in_specs=[blk],
        out_specs=[blk, pl.BlockSpec((1, t, e), lambda b: (b, 0, 0)), blk],
        out_shape=[jax.ShapeDtypeStruct((bsz, e, t), I32),
                   jax.ShapeDtypeStruct((bsz, t, e), I32),
                   jax.ShapeDtypeStruct((bsz, e, t), F32)],
        name="moe_select",
        compiler_params=_cparams(("parallel",)),
    )(aff)


def _ffn_kernel(cap, win, fwin, tk, fk, lo_ref, hi_ref, h_ref, pos_ref, gate_ref, w1_ref, w3_ref, w2_ref,
                ys_o, xin_scr, gsl_scr):
    t = h_ref.shape[1]
    f = w1_ref.shape[3]
    nkt = t // tk
    pos = pos_ref[0, 0]
    g_hi, g_mid, g_lo = _split3(gate_ref[0, 0])
    prow = lax.broadcasted_iota(I32, (SUBLANES, t), 0)
    g8 = jnp.where(prow == 0, g_hi.astype(F32),
                   jnp.where(prow == 1, g_mid.astype(F32),
                             jnp.where(prow == 2, g_lo.astype(F32), 0.0))).astype(BF16)
    xin_scr[...] = jnp.zeros_like(xin_scr)
    gsl_scr[...] = jnp.zeros_like(gsl_scr)
    base = (pl.program_id(1) * pl.num_programs(0) + pl.program_id(0)) * nkt

    def gather(kt, rows, enabled):
        ts = slice(kt * tk, (kt + 1) * tk)
        lo = lo_ref[base + kt]
        start = pl.multiple_of(jnp.minimum((lo // SUBLANES) * SUBLANES, cap - rows), SUBLANES)
        slot = lax.broadcasted_iota(I32, (rows, tk), 0)
        rel = jnp.where(enabled, pos[:, ts] - start, -1)
        onehot = jnp.where(rel == slot, 1.0, 0.0).astype(BF16)
        xin_scr[pl.ds(start, rows), :] += _dot(onehot, h_ref[0, ts, :])
        gsl_scr[pl.ds(start, rows), :] += _dot_nt(onehot, g8[:, ts])

    def fits_fast(kt):
        lo = lo_ref[base + kt]
        start = jnp.minimum((lo // SUBLANES) * SUBLANES, cap - fwin)
        return hi_ref[base + kt] <= start + fwin

    for kt in range(nkt):
        gather(kt, fwin, fits_fast(kt) if fwin < win else True)
    for kt in range(nkt if fwin < win else 0):
        @pl.when(jnp.logical_not(fits_fast(kt)))
        def _wide():
            gather(kt, win, True)
    gate_slot = jnp.sum(gsl_scr[...], axis=1, keepdims=True)
    xb = xin_scr[...].astype(BF16)
    y = jnp.zeros(xin_scr.shape, F32)
    n_chunks = f // fk

    def up(kf):
        fs = slice(kf * fk, (kf + 1) * fk)
        return _dot(xb, w1_ref[0, 0, :, fs]), _dot(xb, w3_ref[0, 0, :, fs])

    nxt = up(0)
    for kf in range(n_chunks):
        h1, h3 = nxt
        if kf + 1 < n_chunks:
            nxt = up(kf + 1)
        hid = (h1 * _sigmoid(h1) * h3).astype(BF16)
        y = y + _dot(hid, w2_ref[0, 0, kf * fk:(kf + 1) * fk, :])
    ys_o[0, 0] = (y * gate_slot).astype(BF16)


def _ffn(h, posm, gate, layer, w1, w3, w2, cap):
    gsz, t, d = h.shape
    _, e, _, f = w1.shape
    tk = GATHER_TILE
    fk = min(f, 512)
    win = min(cap, tk + SUBLANES)
    fwin = min(win, GATHER_FAST_ROWS)
    assert t % tk == 0 and (cap - win) % SUBLANES == 0 and (cap - fwin) % SUBLANES == 0
    nkt = t // tk
    pr = posm.reshape(gsz, e, nkt, tk)
    first = jnp.min(jnp.where(pr >= 0, pr, cap), axis=3)
    first = jnp.where(first >= cap, 0, first).reshape(-1)
    last = (jnp.max(pr, axis=3) + 1).reshape(-1)
    return pl.pallas_call(
        functools.partial(_ffn_kernel, cap, win, fwin, tk, fk),
        grid_spec=pltpu.PrefetchScalarGridSpec(
            num_scalar_prefetch=2,
            grid=(e, gsz),
            in_specs=[
                pl.BlockSpec((1, t, d), lambda ei, gi, lo, hi: (gi, 0, 0)),
                pl.BlockSpec((1, 1, 1, t), lambda ei, gi, lo, hi: (gi, ei, 0, 0)),
                pl.BlockSpec((1, 1, 1, t), lambda ei, gi, lo, hi: (gi, ei, 0, 0)),
                pl.BlockSpec((1, 1, d, f), lambda ei, gi, lo, hi: (layer, ei, 0, 0)),
                pl.BlockSpec((1, 1, d, f), lambda ei, gi, lo, hi: (layer, ei, 0, 0)),
                pl.BlockSpec((1, 1, f, d), lambda ei, gi, lo, hi: (layer, ei, 0, 0)),
            ],
            out_specs=pl.BlockSpec((1, 1, cap, d), lambda ei, gi, lo, hi: (gi, ei, 0, 0)),
            scratch_shapes=[pltpu.VMEM((cap, d), F32), pltpu.VMEM((cap, SUBLANES), F32)],
        ),
        out_shape=jax.ShapeDtypeStruct((gsz, e, cap, d), BF16),
        name="moe_ffn",
        compiler_params=_cparams(("arbitrary", "arbitrary")),
    )(first, last, h, posm, gate, w1, w3, w2)


def _combine_kernel(wide, pack, fslots, lo_ref, hi_ref, x_ref, post_ref, ys_ref, mod_ref, o_ref):
    tt = x_ref.shape[1]
    e = ys_ref.shape[1]
    cap = ys_ref.shape[2]
    base = (pl.program_id(0) * pl.num_programs(1) + pl.program_id(1)) * e
    gate = mod_ref[0, 0, 5:6, :]

    def window_start(ei, rows):
        lo = lo_ref[base + ei]
        return pl.multiple_of(jnp.minimum((lo // BF16_ROWS) * BF16_ROWS, cap - rows), BF16_ROWS)

    def wide_term(ei):
        start = window_start(ei, wide)
        slot = lax.broadcasted_iota(I32, (tt, wide), 1)
        onehot = jnp.where(post_ref[0, :, ei:ei + 1] - start == slot, 1.0, 0.0).astype(BF16)
        return _dot(onehot, ys_ref[0, ei, pl.ds(start, wide), :])

    if not pack:
        acc = wide_term(0)
        for ei in range(1, e):
            acc = acc + wide_term(ei)
        o_ref[0] = x_ref[0] + gate * acc
        return

    def fits_fast(ei):
        return hi_ref[base + ei] <= window_start(ei, fslots) + fslots

    lane = lax.broadcasted_iota(I32, (tt, pack * fslots), 1)
    acc = jnp.zeros(x_ref.shape[1:], F32)
    for g in range(e // pack):
        tgt = None
        rows = []
        for q in range(pack):
            ei = g * pack + q
            start = window_start(ei, fslots)
            rel = jnp.where(fits_fast(ei), post_ref[0, :, ei:ei + 1] - start + q * fslots, -1)
            tgt = rel if tgt is None else jnp.where(lane >= q * fslots, rel, tgt)
            rows.append(ys_ref[0, ei, pl.ds(start, fslots), :])
        onehot = jnp.where(tgt == lane, 1.0, 0.0).astype(BF16)
        acc = acc + _dot(onehot, jnp.concatenate(rows, axis=0))
    o_ref[0] = x_ref[0] + gate * acc
    for ei in range(e):
        @pl.when(jnp.logical_not(fits_fast(ei)))
        def _wide():
            o_ref[0] += gate * wide_term(ei)


def _combine(xc, post, ys, mod, tile0, n_tiles, shared_slots):
    bsz, tc, d = xc.shape
    tt = TOKEN_TILE
    _, e, cap, _ = ys.shape
    is_ctx = 1 if tile0 > 0 else 0
    ys_map = (lambda b, j, lo, hi: (0, 0, 0, 0)) if shared_slots else (lambda b, j, lo, hi: (b, 0, 0, 0))
    wide = min(cap, tt + BF16_ROWS)
    fslots = COMBINE_FAST_SLOTS
    pack = COMBINE_PACK if (e % COMBINE_PACK == 0 and cap > wide) else 0
    assert (cap - wide) % BF16_ROWS == 0 and (not pack or (cap - fslots) % BF16_ROWS == 0)
    pr = post.reshape(bsz, n_tiles, tt, e)
    first = jnp.min(jnp.where(pr >= 0, pr, cap), axis=2)
    first = jnp.where(first >= cap, 0, first).reshape(-1)
    last = (jnp.max(pr, axis=2) + 1).reshape(-1)
    return pl.pallas_call(
        functools.partial(_combine_kernel, wide, pack, fslots),
        grid_spec=pltpu.PrefetchScalarGridSpec(
            num_scalar_prefetch=2,
            grid=(bsz, n_tiles),
            in_specs=[
                pl.BlockSpec((1, tt, d), lambda b, j, lo, hi: (b, tile0 + j, 0)),
                pl.BlockSpec((1, tt, e), lambda b, j, lo, hi: (b, j, 0)),
                pl.BlockSpec((1, e, cap, d), ys_map),
                pl.BlockSpec((1, 1, 6, d), lambda b, j, lo, hi: (b, is_ctx, 0, 0)),
            ],
            out_specs=pl.BlockSpec((1, tt, d), lambda b, j, lo, hi: (b, tile0 + j, 0)),
        ),
        out_shape=jax.ShapeDtypeStruct((bsz, tc, d), F32),
        input_output_aliases={2: 0},
        name="moe_combine",
        compiler_params=_cparams(("parallel", "arbitrary")),
    )(first, last, xc, post, ys, mod)


def _moe(xc, mod, gn, router_t, layer, w1, w3, w2, nl, nct, seq, ctx_len, do_ctx):
    bsz, tc, d = xc.shape
    e = router_t.shape[0]
    h, aff = _router(xc, mod, gn, router_t, 0, nl)
    cap_l = CAPACITY_FACTOR * seq // e
    pos, post, gate = _select(aff, cap_l, 0)
    ys = _ffn(h, pos.reshape(bsz, e, 1, seq), gate.reshape(bsz, e, 1, seq), layer, w1, w3, w2, cap_l)
    xc = _combine(xc, post, ys, mod, 0, nl, False)
    if do_ctx:
        h_ctx, aff_ctx = _router(xc, mod, gn, router_t, nl, nct)
        cap_c = CAPACITY_FACTOR * ctx_len // e
        pos, post, gate = _select(aff_ctx, cap_c, cap_c)
        pos = jnp.transpose(pos, (1, 0, 2)).reshape(1, e, 1, bsz * ctx_len)
        gate = jnp.transpose(gate, (1, 0, 2)).reshape(1, e, 1, bsz * ctx_len)
        ys = _ffn(h_ctx.reshape(1, bsz * ctx_len, d), pos, gate, layer, w1, w3, w2, bsz * cap_c)
        xc = _combine(xc, post, ys, mod, nl, nct, True)
    return xc


def _final_kernel(x_ref, g_ref, o_ref):
    x = x_ref[0]
    ms = jnp.mean(x * x, axis=-1, keepdims=True)
    o_ref[0] = x * lax.rsqrt(ms + RMS_EPS) * g_ref[...]


def _final_norm(xc, g, seq):
    bsz, tc, d = xc.shape
    tt = TOKEN_TILE
    tile = pl.BlockSpec((1, tt, d), lambda b, j: (b, j, 0))
    return pl.pallas_call(
        _final_kernel,
        grid=(bsz, seq // tt),
        in_specs=[tile, pl.BlockSpec(g.shape, lambda b, j: (0, 0))],
        out_specs=tile,
        out_shape=jax.ShapeDtypeStruct((bsz, seq, d), F32),
        name="final_norm",
        compiler_params=_cparams(("parallel", "arbitrary")),
    )(xc, g)


def kernel(x, c, ctx, c_ctx, ada_w, ada_b, norm1_g, norm2_g, rwkv_mix, rwkv_wrkv, rwkv_w0, rwkv_w1, rwkv_w2, rwkv_a0, rwkv_a1, rwkv_a2, rwkv_v0, rwkv_v1, rwkv_v2, rwkv_g1, rwkv_g2, rwkv_kk, rwkv_ka, rwkv_rk, rwkv_lnw, rwkv_lnb, rwkv_wo, pool_w, pool_scale, moe_router, moe_w1, moe_w3, moe_w2, final_g):
    bsz, seq, d = x.shape
    ctx_len = ctx.shape[1]
    depth = ada_w.shape[0]
    n_heads, head = rwkv_rk.shape[1], rwkv_rk.shape[2]
    n_mixers = 2
    tt = TOKEN_TILE
    assert seq % tt == 0 and ctx_len % tt == 0 and tt % GRID_W == 0
    assert head == WKV_CHUNK and 2 * head == LANES and d % LANES == 0
    nl, nct = seq // tt, ctx_len // tt

    xc = jnp.concatenate([x, ctx], axis=1)

    rows = -(-(bsz + 1) // SUBLANES) * SUBLANES
    cond = jnp.zeros((rows, d), F32).at[:bsz].set(c).at[bsz].set(c_ctx)
    m_all = _ada_all(cond, ada_w, ada_b)
    m_lat = m_all[:, :bsz].reshape(depth, bsz, 1, 6, d)
    m_ctx = jnp.broadcast_to(m_all[:, bsz].reshape(depth, 1, 1, 6, d), (depth, bsz, 1, 6, d))
    mods = jnp.concatenate([m_lat, m_ctx], axis=2)

    head_of = jnp.arange(d) // head
    hs = (head_of[:, None] == jnp.arange(HEAD_COLS)[None, :]).astype(BF16)
    hst = jnp.transpose(hs)

    w1_all, w3_all, w2_all = moe_w1.astype(BF16), moe_w3.astype(BF16), moe_w2.astype(BF16)
    v_first = None
    for i in range(depth):
        last = i == depth - 1
        is_rwkv = i % n_mixers == 0
        jn = i // n_mixers
        mod = mods[i]
        gn1 = norm1_g[i].reshape(1, d)
        if is_rwkv:
            p = {
                'mix': rwkv_mix[jn],
                'wr': rwkv_wrkv[jn, 0].astype(BF16), 'wk': rwkv_wrkv[jn, 1].astype(BF16),
                'wv': rwkv_wrkv[jn, 2].astype(BF16),
                'g1': rwkv_g1[jn].astype(BF16), 'g2': rwkv_g2[jn].astype(BF16),
                'w1': rwkv_w1[jn].astype(BF16), 'w2': rwkv_w2[jn].astype(BF16), 'w0': rwkv_w0[jn],
                'a1': rwkv_a1[jn].astype(BF16), 'a2': rwkv_a2[jn].astype(BF16), 'a0': rwkv_a0[jn],
                'k_k': rwkv_kk[jn].reshape(1, d), 'k_a': rwkv_ka[jn].reshape(1, d),
                'r_k': rwkv_rk[jn].reshape(1, d), 'ln_w': rwkv_lnw[jn].reshape(1, d),
                'ln_b': rwkv_lnb[jn].reshape(1, d), 'wo': rwkv_wo[jn].astype(BF16),
                'hs': hs, 'hst': hst,
            }
            vres = None if jn == 0 else (rwkv_v0[jn - 1].reshape(1, d), rwkv_v1[jn - 1].astype(BF16),
                                         rwkv_v2[jn - 1].astype(BF16))
            r, v, g, kk, lw0, lw1, b0, b1, kd0, kd1 = _rwkv_proj(xc, mod, gn1, p, vres, v_first, nl, seq, ctx_len)
            if v_first is None:
                v_first = v
            yf, yr = _wkv(r, v, kk, lw0, b0, kd0, lw1, b1, kd1, seq, ctx_len)
            xc = _rwkv_out(xc, yf, yr, r, v, g, kd0, kd1, mod, p, nl, head)
        else:
            n_tiles = nl if last else nl + nct
            xc = _pool(xc, mod, gn1, pool_w[jn].astype(BF16), pool_scale[jn].reshape(1, d),
                       nl, seq, ctx_len, n_tiles)
        xc = _moe(xc, mod, norm2_g[i].reshape(1, d), jnp.transpose(moe_router[i]),
                  i, w1_all, w3_all, w2_all, nl, nct, seq, ctx_len, not last)
    return _final_norm(xc, final_g.reshape(1, d), seq)
```

```python
import functools
import math

import jax
import jax.numpy as jnp
from jax import lax
from jax.experimental import pallas as pl
from jax.experimental.pallas import tpu as pltpu

F32 = jnp.float32
BF16 = jnp.bfloat16
I32 = jnp.int32
HIGHEST = lax.Precision.HIGHEST

GRID_W = 64
POOL_WINDOWS = (2, 4, 8, 16)
CAPACITY_FACTOR = 2
RMS_EPS = 1e-6
GN_EPS = 64e-5
EXP_NEG_HALF = math.exp(-0.5)

LANES = 128
SUBLANES = 8
BF16_ROWS = 16
VMEM_LIMIT_BYTES = 56 * 1024 * 1024

TOKEN_TILE = 256
GATHER_TILE = 256
GATHER_FAST_ROWS = 56
COMBINE_PACK = 4
COMBINE_FAST_SLOTS = 64
WKV_CHUNK = 64
WKV_PAIRS_PER_STEP = 8
POOL_HALO = 8
HEAD_COLS = LANES


def _cparams(sem):
    return pltpu.CompilerParams(dimension_semantics=sem, vmem_limit_bytes=VMEM_LIMIT_BYTES)


def _dot(a, b):
    return jnp.dot(a, b, preferred_element_type=F32)


def _dot_nt(a, b):
    return lax.dot_general(a, b, (((1,), (1,)), ((), ())), preferred_element_type=F32)


def _dot_tn(a, b):
    return lax.dot_general(a, b, (((0,), (0,)), ((), ())), preferred_element_type=F32)


def _split3(x):
    hi = x.astype(BF16)
    r1 = x - hi.astype(F32)
    mid = r1.astype(BF16)
    lo = (r1 - mid.astype(F32)).astype(BF16)
    return hi, mid, lo


def _dot_split(x, m, pieces):
    hi = x.astype(BF16)
    out = _dot(hi, m)
    for _ in range(pieces - 1):
        x = x - hi.astype(F32)
        hi = x.astype(BF16)
        out = out + _dot(hi, m)
    return out


def _dot3_left(m, x):
    w = x.shape[1]
    z = _dot(m, jnp.concatenate(_split3(x), axis=1))
    return z[:, 0:w] + z[:, w:2 * w] + z[:, 2 * w:3 * w]


def _sigmoid(x):
    return 0.5 * jnp.tanh(0.5 * x) + 0.5


def _norm_mod(x, g, shift, scale):
    ms = jnp.mean(x * x, axis=-1, keepdims=True)
    return x * lax.rsqrt(ms + RMS_EPS) * (g * (1.0 + scale)) + shift


def _ada_kernel(c_ref, w_ref, b_ref, o_ref):
    c = c_ref[...]
    s = c * _sigmoid(c)
    o_ref[0] = jnp.dot(s, w_ref[0], precision=HIGHEST, preferred_element_type=F32) + b_ref[0]


def _ada_all(cond, ada_w, ada_b):
    depth, d, n6 = ada_w.shape
    rows = cond.shape[0]
    nt = 512
    return pl.pallas_call(
        _ada_kernel,
        grid=(depth, n6 // nt),
        in_specs=[
            pl.BlockSpec((rows, d), lambda i, n: (0, 0)),
            pl.BlockSpec((1, d, nt), lambda i, n: (i, 0, n)),
            pl.BlockSpec((1, 1, nt), lambda i, n: (i, 0, n)),
        ],
        out_specs=pl.BlockSpec((1, rows, nt), lambda i, n: (i, 0, n)),
        out_shape=jax.ShapeDtypeStruct((depth, rows, n6), F32),
        name="ada_mod",
        compiler_params=_cparams(("arbitrary", "arbitrary")),
    )(cond, ada_w, ada_b.reshape(depth, 1, n6))


def _rwkv_proj_kernel(nl, seq, ctx_len, has_vres, *refs):
    if has_vres:
        (x_ref, xp_ref, xn_ref, mod_ref, gn_ref, mix_ref, wr_ref, wk_ref, wv_ref, g1_ref, g2_ref,
         w1_ref, w2_ref, w0_ref, a1_ref, a2_ref, a0_ref, kkw_ref, kaw_ref, rk_ref, hs_ref, hst_ref,
         v0_ref, v1_ref, v2_ref, vf_ref,
         r_o, v_o, g_o, kk_o, lw0_o, lw1_o, b0_o, b1_o, kd0_o, kd1_o, bs_o, ext_scr, sh_scr) = refs
    else:
        (x_ref, xp_ref, xn_ref, mod_ref, gn_ref, mix_ref, wr_ref, wk_ref, wv_ref, g1_ref, g2_ref,
         w1_ref, w2_ref, w0_ref, a1_ref, a2_ref, a0_ref, kkw_ref, kaw_ref, rk_ref, hs_ref, hst_ref,
         r_o, v_o, g_o, kk_o, lw0_o, lw1_o, b0_o, b1_o, kd0_o, kd1_o, bs_o, ext_scr, sh_scr) = refs
    tt = x_ref.shape[1]
    d = x_ref.shape[2]
    hw = GRID_W
    j = pl.program_id(1)
    shift = mod_ref[0, 0, 0:1, :]
    scale = mod_ref[0, 0, 1:2, :]
    gn = gn_ref[...]
    ext_scr[0:hw, :] = _norm_mod(xp_ref[0], gn, shift, scale)
    ext_scr[hw:hw + tt, :] = _norm_mod(x_ref[0], gn, shift, scale)
    ext_scr[hw + tt:, :] = _norm_mod(xn_ref[0], gn, shift, scale)

    i = lax.broadcasted_iota(I32, (tt, 1), 0)
    q = d // 4

    @pl.when(j < nl)
    def _latent_shift():
        t = j * tt + i
        col = i % hw
        sh_scr[:, 0:q] = jnp.where(col != 0, ext_scr[hw - 1:hw - 1 + tt, 0:q], 0.0)
        sh_scr[:, q:2 * q] = jnp.where(col != hw - 1, ext_scr[hw + 1:hw + 1 + tt, q:2 * q], 0.0)
        sh_scr[:, 2 * q:3 * q] = jnp.where(t >= hw, ext_scr[0:tt, 2 * q:3 * q], 0.0)
        sh_scr[:, 3 * q:] = jnp.where(t < seq - hw, ext_scr[2 * hw:2 * hw + tt, 3 * q:], 0.0)

    @pl.when(j >= nl)
    def _context_shift():
        t = (j - nl) * tt + i
        hd = d // 2
        sh_scr[:, 0:hd] = jnp.where(t != 0, ext_scr[hw - 1:hw - 1 + tt, 0:hd], 0.0)
        sh_scr[:, hd:] = jnp.where(t != ctx_len - 1, ext_scr[hw + 1:hw + 1 + tt, hd:], 0.0)

    h = ext_scr[hw:hw + tt, :]
    xx = sh_scr[...] - h

    def mixed(n):
        return (h + xx * mix_ref[n:n + 1, :]).astype(BF16)

    xr, xw, xk, xv, xa, xg = [mixed(n) for n in range(6)]
    r = _dot(xr, wr_ref[...])
    k = _dot(xk, wk_ref[...])
    v = _dot(xv, wv_ref[...])
    if has_vres:
        lor = _dot(_dot(xv, v1_ref[...]).astype(BF16), v2_ref[...])
        v = v + (vf_ref[0] - v) * _sigmoid(v0_ref[...] + lor)
    g = _dot(_sigmoid(_dot(xg, g1_ref[...])).astype(BF16), g2_ref[...])
    r_o[0] = r
    v_o[0] = v
    g_o[0] = g

    kkr = k * kkw_ref[...]
    ss = _dot_split(kkr * kkr, hs_ref[...], 1)
    inv = 1.0 / jnp.maximum(jnp.sqrt(ss), 1e-12)
    kk = kkr * _dot_split(inv, hst_ref[...], 2)
    kk_o[0] = kk
    kaw = kaw_ref[...]
    kb = None
    for dr, (lw_o, b_o, kd_o) in enumerate(((lw0_o, b0_o, kd0_o), (lw1_o, b1_o, kd1_o))):
        wpre = w0_ref[dr:dr + 1, :] + _dot(jnp.tanh(_dot(xw, w1_ref[dr])).astype(BF16), w2_ref[dr])
        lw_o[0] = -EXP_NEG_HALF * _sigmoid(wpre)
        a = _sigmoid(a0_ref[dr:dr + 1, :] + _dot(_dot(xa, a1_ref[dr]).astype(BF16), a2_ref[dr]))
        b_o[0] = kk * a
        kd = k * (1.0 + (a - 1.0) * kaw)
        kd_o[0] = kd
        kb = 0.5 * kd if kb is None else kb + 0.5 * kd
    bs_o[0] = _dot_split(r * kb * rk_ref[...], hs_ref[...], 1)


def _rwkv_proj(xc, mod, gn, p, vres, v_first, nl, seq, ctx_len):
    bsz, tc, d = xc.shape
    tt = TOKEN_TILE
    nt = tc // tt
    hb = tt // GRID_W
    nhb = tc // GRID_W
    has_vres = vres is not None

    def full(a):
        nd = a.ndim
        return pl.BlockSpec(a.shape, lambda b, j, _n=nd: (0,) * _n)

    tile = pl.BlockSpec((1, tt, d), lambda b, j: (b, j, 0))
    ins = [xc, xc, xc, mod, gn, p['mix'], p['wr'], p['wk'], p['wv'], p['g1'], p['g2'],
           p['w1'], p['w2'], p['w0'], p['a1'], p['a2'], p['a0'], p['k_k'], p['k_a'], p['r_k'], p['hs'], p['hst']]
    specs = [
        tile,
        pl.BlockSpec((1, GRID_W, d), lambda b, j: (b, jnp.maximum(j * hb - 1, 0), 0)),
        pl.BlockSpec((1, GRID_W, d), lambda b, j: (b, jnp.minimum((j + 1) * hb, nhb - 1), 0)),
        pl.BlockSpec((1, 1, 6, d), lambda b, j: (b, (j >= nl).astype(I32), 0, 0)),
    ] + [full(a) for a in ins[4:]]
    if has_vres:
        ins += [vres[0], vres[1], vres[2], v_first]
        specs += [full(vres[0]), full(vres[1]), full(vres[2]), tile]
    out_sds = jax.ShapeDtypeStruct((bsz, tc, d), F32)
    return pl.pallas_call(
        functools.partial(_rwkv_proj_kernel, nl, seq, ctx_len, has_vres),
        grid=(bsz, nt),
        in_specs=specs,
        out_specs=[tile] * 10 + [pl.BlockSpec((1, tt, HEAD_COLS), lambda b, j: (b, j, 0))],
        out_shape=[out_sds] * 10 + [jax.ShapeDtypeStruct((bsz, tc, HEAD_COLS), F32)],
        scratch_shapes=[pltpu.VMEM((tt + 2 * GRID_W, d), F32), pltpu.VMEM((tt, d), F32)],
        name="rwkv_proj",
        compiler_params=_cparams(("parallel", "arbitrary")),
    )(*ins)


def _wkv_chains(chains, masks):
    nc = len(chains)
    ks = range(nc)
    L = chains[0][0].shape[0]
    r, v, kk, lw, b, kd, s_prev, rev = [[ch[i] for ch in chains] for i in range(8)]
    mk = [masks[1] if rv else masks[0] for rv in rev]
    tri, m0, m1, strict, incl, eye2, bd = [[m[i] for m in mk] for i in range(7)]

    def stack(k, x):
        xb = x.astype(BF16)
        return jnp.concatenate([xb * m0[k], xb * m1[k]], axis=0)

    c = [_dot3_left(tri[k], lw[k]) for k in ks]
    ctot = [c[k][0:1, :] if rev[k] else c[k][L - 1:L, :] for k in ks]
    e_c = [jnp.exp(c[k]) for k in ks]
    e_nc = [jnp.exp(-c[k]) for k in ks]
    e_tc = [jnp.exp(ctot[k] - c[k]) for k in ks]
    ah = [-kk[k] * jnp.exp(c[k] - lw[k]) for k in ks]
    rh = [r[k] * e_c[k] for k in ks]
    lhs = [jnp.concatenate([ah[k], rh[k]], axis=0).astype(BF16) for k in ks]
    rhs = [jnp.concatenate([stack(k, b[k] * e_nc[k]), stack(k, kd[k] * e_nc[k])], axis=0) for k in ks]
    aa = [_dot_nt(lhs[k], rhs[k]) for k in ks]
    a_ab = [jnp.where(strict[k], aa[k][0:L, 0:2 * L], 0.0) for k in ks]
    a_ak = [jnp.where(strict[k], aa[k][0:L, 2 * L:4 * L], 0.0).astype(BF16) for k in ks]
    a_r = [jnp.where(jnp.concatenate([incl[k], incl[k]], axis=1), aa[k][L:2 * L, :], 0.0).astype(BF16)
           for k in ks]

    n_dbl = int(math.log2(L))
    tm = [eye2[k] + a_ab[k] for k in ks]
    pw = [_dot(a_ab[k].astype(BF16), stack(k, a_ab[k])) for k in ks]
    for _ in range(n_dbl - 2):
        z = [_dot(pw[k].astype(BF16), jnp.concatenate([stack(k, tm[k]), stack(k, pw[k])], axis=1)) for k in ks]
        tm = [tm[k] + z[k][:, 0:2 * L] for k in ks]
        pw = [z[k][:, 2 * L:4 * L] for k in ks]
    tm = [tm[k] + _dot(pw[k].astype(BF16), stack(k, tm[k])) for k in ks]

    ss0 = [_dot_nt(lhs[k], s_prev[k].astype(BF16)) for k in ks]
    vs = [stack(k, v[k]) for k in ks]
    wmat = [ss0[k][0:L] + _dot(a_ak[k], vs[k]) for k in ks]
    u = [_dot(tm[k].astype(BF16), stack(k, wmat[k])) for k in ks]
    y = [ss0[k][L:2 * L] + _dot(a_r[k], jnp.concatenate([stack(k, u[k]), vs[k]], axis=0)) for k in ks]
    uv = [jnp.concatenate([u[k], v[k]], axis=0).astype(BF16) for k in ks]
    bk = [jnp.concatenate([b[k] * e_tc[k], kd[k] * e_tc[k]], axis=0).astype(BF16) for k in ks]
    upd = [_dot_tn(uv[k], bk[k]) for k in ks]
    s_new = [s_prev[k] * jnp.exp(ctot[k]) + jnp.where(bd[k], upd[k], 0.0) for k in ks]
    return y, s_new


def _wkv_masks(L, w2, reverse):
    n = w2 // 2
    row = lax.broadcasted_iota(I32, (L, L), 0)
    colm = lax.broadcasted_iota(I32, (L, L), 1)
    tri = jnp.where((colm >= row) if reverse else (colm <= row), 1.0, 0.0).astype(BF16)
    lane = lax.broadcasted_iota(I32, (1, w2), 1)
    m0 = jnp.where(lane < n, 1.0, 0.0).astype(BF16)
    m1 = jnp.where(lane < n, 0.0, 1.0).astype(BF16)
    t_i = lax.broadcasted_iota(I32, (L, 2 * L), 0)
    s_i = lax.broadcasted_iota(I32, (L, 2 * L), 1) % L
    strict = (s_i > t_i) if reverse else (s_i < t_i)
    incl = (s_i >= t_i) if reverse else (s_i <= t_i)
    eye2 = jnp.where(s_i == t_i, 1.0, 0.0)
    ri = lax.broadcasted_iota(I32, (w2, w2), 0)
    ci = lax.broadcasted_iota(I32, (w2, w2), 1)
    bd = (ri < n) == (ci < n)
    return tri, m0, m1, strict, incl, eye2, bd


def _wkv_kernel(rf, vf, kkf, lwf, bf, kdf, rr, vr, kkr, lwr, br, kdr, yf_o, yr_o, s_scr):
    @pl.when(pl.program_id(2) == 0)
    def _init():
        s_scr[...] = jnp.zeros_like(s_scr)

    L = rf.shape[1]
    masks = (_wkv_masks(L, LANES, False), _wkv_masks(L, LANES, True))
    chains = []
    for hp in range(rf.shape[2] // LANES):
        ls = slice(hp * LANES, (hp + 1) * LANES)
        chains.append((rf[0, :, ls], vf[0, :, ls], kkf[0, :, ls], lwf[0, :, ls], bf[0, :, ls],
                       kdf[0, :, ls], s_scr[0, hp], False))
        chains.append((rr[0, :, ls], vr[0, :, ls], kkr[0, :, ls], lwr[0, :, ls], br[0, :, ls],
                       kdr[0, :, ls], s_scr[1, hp], True))
    ys, ss = _wkv_chains(chains, masks)
    for hp in range(rf.shape[2] // LANES):
        ls = slice(hp * LANES, (hp + 1) * LANES)
        yf_o[0, :, ls] = ys[2 * hp]
        yr_o[0, :, ls] = ys[2 * hp + 1]
        s_scr[0, hp] = ss[2 * hp]
        s_scr[1, hp] = ss[2 * hp + 1]


def _wkv(r, v, kk, lw0, b0, kd0, lw1, b1, kd1, seq, ctx_len):
    bsz, tc, d = r.shape
    L = WKV_CHUNK
    nlc = seq // L
    ncc = ctx_len // L
    nch = nlc + ncc
    gp = min(WKV_PAIRS_PER_STEP, d // LANES)
    width = gp * LANES

    def fwd_map(b, h, c):
        return (b, jnp.where(c < ncc, nlc + c, c - ncc), h)

    def rev_map(b, h, c):
        return (b, jnp.where(c < ncc, nlc + ncc - 1 - c, nlc - 1 - (c - ncc)), h)

    fs = pl.BlockSpec((1, L, width), fwd_map)
    rs = pl.BlockSpec((1, L, width), rev_map)
    sds = jax.ShapeDtypeStruct((bsz, tc, d), F32)
    return pl.pallas_call(
        _wkv_kernel,
        grid=(bsz, d // width, nch),
        in_specs=[fs] * 6 + [rs] * 6,
        out_specs=[fs, rs],
        out_shape=[sds, sds],
        scratch_shapes=[pltpu.VMEM((2, gp, LANES, LANES), F32)],
        name="wkv_scan",
        compiler_params=_cparams(("parallel", "parallel", "arbitrary")),
    )(r, v, kk, lw0, b0, kd0, r, v, kk, lw1, b1, kd1)


def _rwkv_out_kernel(head, x_ref, yf_ref, yr_ref, v_ref, g_ref, bs_ref, mod_ref,
                     lnw_ref, lnb_ref, wo_ref, hs_ref, hst_ref, o_ref):
    hs = hs_ref[...]
    hst = hst_ref[...]
    o = yf_ref[0] + yr_ref[0]
    inv_n = 1.0 / head
    mu = _dot_split(_dot_split(o, hs, 2) * inv_n, hst, 2)
    dlt = o - mu
    var = _dot_split(dlt * dlt, hs, 1) * inv_n
    on = dlt * _dot_split(lax.rsqrt(var + GN_EPS), hst, 2)
    on = on * lnw_ref[...] + lnb_ref[...]
    bonus = _dot_split(bs_ref[0], hst, 2) * v_ref[0]
    y = ((on + bonus) * g_ref[0]).astype(BF16)
    gate = mod_ref[0, 0, 2:3, :]
    o_ref[0] = x_ref[0] + gate * _dot(y, wo_ref[...])


def _rwkv_out(xc, yf, yr, v, g, bsum, mod, p, nl, head):
    bsz, tc, d = xc.shape
    tt = TOKEN_TILE
    tile = pl.BlockSpec((1, tt, d), lambda b, j: (b, j, 0))

    def full(a):
        nd = a.ndim
        return pl.BlockSpec(a.shape, lambda b, j, _n=nd: (0,) * _n)

    consts = [p['ln_w'], p['ln_b'], p['wo'], p['hs'], p['hst']]
    return pl.pallas_call(
        functools.partial(_rwkv_out_kernel, head),
        grid=(bsz, tc // tt),
        in_specs=[tile] * 5 + [pl.BlockSpec((1, tt, HEAD_COLS), lambda b, j: (b, j, 0)),
                               pl.BlockSpec((1, 1, 6, d), lambda b, j: (b, (j >= nl).astype(I32), 0, 0))]
        + [full(a) for a in consts],
        out_specs=tile,
        out_shape=jax.ShapeDtypeStruct((bsz, tc, d), F32),
        input_output_aliases={0: 0},
        name="rwkv_out",
        compiler_params=_cparams(("parallel", "arbitrary")),
    )(xc, yf, yr, v, g, bsum, mod, *consts)


def _pool_kernel(nl, seq, ctx_len, x_ref, xp_ref, xn_ref, mod_ref, gn_ref, w_ref, sc_ref, o_ref, ext_scr):
    tt = x_ref.shape[1]
    d = x_ref.shape[2]
    ph = POOL_HALO
    j = pl.program_id(1)
    nct = ctx_len // tt
    shift = mod_ref[0, 0, 0:1, :]
    scale = mod_ref[0, 0, 1:2, :]
    gn = gn_ref[...]
    first = jnp.logical_or(j == 0, j == nl)
    last = jnp.logical_or(j == nl - 1, j == nl + nct - 1)
    hp = _norm_mod(xp_ref[0], gn, shift, scale)
    hn = _norm_mod(xn_ref[0], gn, shift, scale)
    ext_scr[0:ph, :] = jnp.where(first, 0.0, hp)
    ext_scr[ph:ph + tt, :] = _norm_mod(x_ref[0], gn, shift, scale)
    ext_scr[ph + tt:, :] = jnp.where(last, 0.0, hn)

    i = lax.broadcasted_iota(I32, (tt, 1), 0)
    t = jnp.where(j < nl, j * tt + i, (j - nl) * tt + i)
    tseg = jnp.where(j < nl, seq, ctx_len)
    ng = len(POOL_WINDOWS)
    dg = d // ng
    gate = mod_ref[0, 0, 2:3, :]
    for gi, win in enumerate(POOL_WINDOWS):
        half = win // 2
        cs = slice(gi * dg, (gi + 1) * dg)
        acc = ext_scr[ph - half:ph - half + tt, cs]
        for o in range(1, win):
            acc = acc + ext_scr[ph - half + o:ph - half + o + tt, cs]
        cnt = (jnp.minimum(t + half, tseg) - jnp.maximum(t - half, 0)).astype(F32)
        hg = ext_scr[ph:ph + tt, cs]
        dlt = (acc / cnt - hg).astype(BF16)
        y = _dot(dlt, w_ref[gi]) * sc_ref[:, cs]
        o_ref[0, :, cs] = x_ref[0, :, cs] + gate[:, cs] * y


def _pool(xc, mod, gn, w, sc, nl, seq, ctx_len, n_tiles):
    bsz, tc, d = xc.shape
    tt = TOKEN_TILE
    hb = tt // POOL_HALO
    nhb = tc // POOL_HALO
    tile = pl.BlockSpec((1, tt, d), lambda b, j: (b, j, 0))
    return pl.pallas_call(
        functools.partial(_pool_kernel, nl, seq, ctx_len),
        grid=(bsz, n_tiles),
        in_specs=[
            tile,
            pl.BlockSpec((1, POOL_HALO, d), lambda b, j: (b, jnp.maximum(j * hb - 1, 0), 0)),
            pl.BlockSpec((1, POOL_HALO, d), lambda b, j: (b, jnp.minimum((j + 1) * hb, nhb - 1), 0)),
            pl.BlockSpec((1, 1, 6, d), lambda b, j: (b, (j >= nl).astype(I32), 0, 0)),
            pl.BlockSpec(gn.shape, lambda b, j: (0, 0)),
            pl.BlockSpec(w.shape, lambda b, j: (0, 0, 0)),
            pl.BlockSpec(sc.shape, lambda b, j: (0, 0)),
        ],
        out_specs=tile,
        out_shape=jax.ShapeDtypeStruct((bsz, n_tiles * tt, d), F32),
        scratch_shapes=[pltpu.VMEM((tt + 2 * POOL_HALO, d), F32)],
        name="pool_mix",
        compiler_params=_cparams(("parallel", "arbitrary")),
    )(xc, xc, xc, mod, gn, w, sc)


def _router_kernel(x_ref, mod_ref, gn_ref, rth_ref, rtl_ref, h_o, aff_o):
    h = _norm_mod(x_ref[0], gn_ref[...], mod_ref[0, 0, 3:4, :], mod_ref[0, 0, 4:5, :])
    hb = h.astype(BF16)
    h_o[0] = hb
    hl = (h - hb.astype(F32)).astype(BF16)
    logits = _dot_nt(rth_ref[...], hb) + (_dot_nt(rth_ref[...], hl) + _dot_nt(rtl_ref[...], hb))
    m = jnp.max(logits, axis=0, keepdims=True)
    ex = jnp.exp(logits - m)
    aff_o[0] = ex / jnp.sum(ex, axis=0, keepdims=True)


def _router(xc, mod, gn, router_t, tile0, n_tiles):
    bsz, tc, d = xc.shape
    tt = TOKEN_TILE
    e = router_t.shape[0]
    is_ctx = 1 if tile0 > 0 else 0
    rt_hi = router_t.astype(BF16)
    return pl.pallas_call(
        _router_kernel,
        grid=(bsz, n_tiles),
        in_specs=[
            pl.BlockSpec((1, tt, d), lambda b, j: (b, tile0 + j, 0)),
            pl.BlockSpec((1, 1, 6, d), lambda b, j: (b, is_ctx, 0, 0)),
            pl.BlockSpec(gn.shape, lambda b, j: (0, 0)),
            pl.BlockSpec(router_t.shape, lambda b, j: (0, 0)),
            pl.BlockSpec(router_t.shape, lambda b, j: (0, 0)),
        ],
        out_specs=[pl.BlockSpec((1, tt, d), lambda b, j: (b, j, 0)),
                   pl.BlockSpec((1, e, tt), lambda b, j: (b, 0, j))],
        out_shape=[jax.ShapeDtypeStruct((bsz, n_tiles * tt, d), BF16),
                   jax.ShapeDtypeStruct((bsz, e, n_tiles * tt), F32)],
        name="moe_router",
        compiler_params=_cparams(("parallel", "arbitrary")),
    )(xc, mod, gn, rt_hi, (router_t - rt_hi.astype(F32)).astype(BF16))


def _lane_cumsum_excl(x, utri):
    e, t = x.shape
    off = jnp.zeros((e, 1), F32)
    parts = []
    for kb in range(t // LANES):
        blk = x[:, kb * LANES:(kb + 1) * LANES]
        inc = _dot(blk.astype(BF16), utri)
        parts.append(inc - blk + off)
        off = off + inc[:, LANES - 1:LANES]
    return jnp.concatenate(parts, axis=1), off


def _select_kernel(cap, slot_stride, aff_ref, pos_o, post_o, gate_o):
    a = aff_ref[0]
    e, t = a.shape
    bits = pltpu.bitcast(a, I32)

    def body(it, thr):
        cand = thr | lax.shift_left(jnp.int32(1), jnp.int32(29) - it)
        cnt = jnp.sum(jnp.where(bits >= cand, 1.0, 0.0), axis=1, keepdims=True)
        return jnp.where(cnt >= cap, cand, thr)

    thr = lax.fori_loop(0, 30, body, jnp.zeros((e, 1), I32))
    gt = jnp.where(bits > thr, 1.0, 0.0)
    eq = jnp.where(bits == thr, 1.0, 0.0)
    ri = lax.broadcasted_iota(I32, (LANES, LANES), 0)
    ci = lax.broadcasted_iota(I32, (LANES, LANES), 1)
    utri = jnp.where(ri <= ci, 1.0, 0.0).astype(BF16)
    n_gt = jnp.sum(gt, axis=1, keepdims=True)
    eq_rank, _ = _lane_cumsum_excl(eq, utri)
    sel = gt + eq * jnp.where(eq_rank < cap - n_gt, 1.0, 0.0)
    pos, _ = _lane_cumsum_excl(sel, utri)
    base = pl.program_id(0) * slot_stride
    posm = jnp.where(sel > 0.5, pos.astype(I32) + base, -1)
    pos_o[0] = posm
    post_o[0] = jnp.transpose(posm.astype(F32)).astype(I32)
    gate_o[0] = jnp.where(sel > 0.5, a, 0.0)


def _select(aff, cap, slot_stride):
    bsz, e, t = aff.shape
    blk = pl.BlockSpec((1, e, t), lambda b: (b, 0, 0))
    return pl.pallas_call(
        functools.partial(_select_kernel, cap, slot_stride),
        grid=(bsz,),
        in_specs=[blk],
        out_specs=[blk, pl.BlockSpec((1, t, e), lambda b: (b, 0, 0)), blk],
        out_shape=[jax.ShapeDtypeStruct((bsz, e, t), I32),
                   jax.ShapeDtypeStruct((bsz, t, e), I32),
                   jax.ShapeDtypeStruct((bsz, e, t), F32)],
        name="moe_select",
        compiler_params=_cparams(("parallel",)),
    )(aff)


def _ffn_kernel(cap, win, fwin, tk, fk, lo_ref, hi_ref, h_ref, pos_ref, gate_ref, w1_ref, w3_ref, w2_ref,
                ys_o, xin_scr, gsl_scr):
    t = h_ref.shape[1]
    f = w1_ref.shape[3]
    nkt = t // tk
    pos = pos_ref[0, 0]
    g_hi, g_mid, g_lo = _split3(gate_ref[0, 0])
    prow = lax.broadcasted_iota(I32, (SUBLANES, t), 0)
    g8 = jnp.where(prow == 0, g_hi.astype(F32),
                   jnp.where(prow == 1, g_mid.astype(F32),
                             jnp.where(prow == 2, g_lo.astype(F32), 0.0))).astype(BF16)
    xin_scr[...] = jnp.zeros_like(xin_scr)
    gsl_scr[...] = jnp.zeros_like(gsl_scr)
    base = (pl.program_id(1) * pl.num_programs(0) + pl.program_id(0)) * nkt

    def gather(kt, rows, enabled):
        ts = slice(kt * tk, (kt + 1) * tk)
        lo = lo_ref[base + kt]
        start = pl.multiple_of(jnp.minimum((lo // SUBLANES) * SUBLANES, cap - rows), SUBLANES)
        slot = lax.broadcasted_iota(I32, (rows, tk), 0)
        rel = jnp.where(enabled, pos[:, ts] - start, -1)
        onehot = jnp.where(rel == slot, 1.0, 0.0).astype(BF16)
        xin_scr[pl.ds(start, rows), :] += _dot(onehot, h_ref[0, ts, :])
        gsl_scr[pl.ds(start, rows), :] += _dot_nt(onehot, g8[:, ts])

    def fits_fast(kt):
        lo = lo_ref[base + kt]
        start = jnp.minimum((lo // SUBLANES) * SUBLANES, cap - fwin)
        return hi_ref[base + kt] <= start + fwin

    for kt in range(nkt):
        gather(kt, fwin, fits_fast(kt) if fwin < win else True)
    for kt in range(nkt if fwin < win else 0):
        @pl.when(jnp.logical_not(fits_fast(kt)))
        def _wide():
            gather(kt, win, True)
    gate_slot = jnp.sum(gsl_scr[...], axis=1, keepdims=True)
    xb = xin_scr[...].astype(BF16)
    y = jnp.zeros(xin_scr.shape, F32)
    n_chunks = f // fk

    def up(kf):
        fs = slice(kf * fk, (kf + 1) * fk)
        return _dot(xb, w1_ref[0, 0, :, fs]), _dot(xb, w3_ref[0, 0, :, fs])

    nxt = up(0)
    for kf in range(n_chunks):
        h1, h3 = nxt
        if kf + 1 < n_chunks:
            nxt = up(kf + 1)
        hid = (h1 * _sigmoid(h1) * h3).astype(BF16)
        y = y + _dot(hid, w2_ref[0, 0, kf * fk:(kf + 1) * fk, :])
    ys_o[0, 0] = (y * gate_slot).astype(BF16)


def _ffn(h, posm, gate, layer, w1, w3, w2, cap):
    gsz, t, d = h.shape
    _, e, _, f = w1.shape
    tk = GATHER_TILE
    fk = min(f, 512)
    win = min(cap, tk + SUBLANES)
    fwin = min(win, GATHER_FAST_ROWS)
    assert t % tk == 0 and (cap - win) % SUBLANES == 0 and (cap - fwin) % SUBLANES == 0
    nkt = t // tk
    pr = posm.reshape(gsz, e, nkt, tk)
    first = jnp.min(jnp.where(pr >= 0, pr, cap), axis=3)
    first = jnp.where(first >= cap, 0, first).reshape(-1)
    last = (jnp.max(pr, axis=3) + 1).reshape(-1)
    return pl.pallas_call(
        functools.partial(_ffn_kernel, cap, win, fwin, tk, fk),
        grid_spec=pltpu.PrefetchScalarGridSpec(
            num_scalar_prefetch=2,
            grid=(e, gsz),
            in_specs=[
                pl.BlockSpec((1, t, d), lambda ei, gi, lo, hi: (gi, 0, 0)),
                pl.BlockSpec((1, 1, 1, t), lambda ei, gi, lo, hi: (gi, ei, 0, 0)),
                pl.BlockSpec((1, 1, 1, t), lambda ei, gi, lo, hi: (gi, ei, 0, 0)),
                pl.BlockSpec((1, 1, d, f), lambda ei, gi, lo, hi: (layer, ei, 0, 0)),
                pl.BlockSpec((1, 1, d, f), lambda ei, gi, lo, hi: (layer, ei, 0, 0)),
                pl.BlockSpec((1, 1, f, d), lambda ei, gi, lo, hi: (layer, ei, 0, 0)),
            ],
            out_specs=pl.BlockSpec((1, 1, cap, d), lambda ei, gi, lo, hi: (gi, ei, 0, 0)),
            scratch_shapes=[pltpu.VMEM((cap, d), F32), pltpu.VMEM((cap, SUBLANES), F32)],
        ),
        out_shape=jax.ShapeDtypeStruct((gsz, e, cap, d), BF16),
        name="moe_ffn",
        compiler_params=_cparams(("arbitrary", "arbitrary")),
    )(first, last, h, posm, gate, w1, w3, w2)


def _combine_kernel(wide, pack, fslots, lo_ref, hi_ref, x_ref, post_ref, ys_ref, mod_ref, o_ref):
    tt = x_ref.shape[1]
    e = ys_ref.shape[1]
    cap = ys_ref.shape[2]
    base = (pl.program_id(0) * pl.num_programs(1) + pl.program_id(1)) * e
    gate = mod_ref[0, 0, 5:6, :]

    def window_start(ei, rows):
        lo = lo_ref[base + ei]
        return pl.multiple_of(jnp.minimum((lo // BF16_ROWS) * BF16_ROWS, cap - rows), BF16_ROWS)

    def wide_term(ei):
        start = window_start(ei, wide)
        slot = lax.broadcasted_iota(I32, (tt, wide), 1)
        onehot = jnp.where(post_ref[0, :, ei:ei + 1] - start == slot, 1.0, 0.0).astype(BF16)
        return _dot(onehot, ys_ref[0, ei, pl.ds(start, wide), :])

    if not pack:
        acc = wide_term(0)
        for ei in range(1, e):
            acc = acc + wide_term(ei)
        o_ref[0] = x_ref[0] + gate * acc
        return

    def fits_fast(ei):
        return hi_ref[base + ei] <= window_start(ei, fslots) + fslots

    lane = lax.broadcasted_iota(I32, (tt, pack * fslots), 1)
    acc = jnp.zeros(x_ref.shape[1:], F32)
    for g in range(e // pack):
        tgt = None
        rows = []
        for q in range(pack):
            ei = g * pack + q
            start = window_start(ei, fslots)
            rel = jnp.where(fits_fast(ei), post_ref[0, :, ei:ei + 1] - start + q * fslots, -1)
            tgt = rel if tgt is None else jnp.where(lane >= q * fslots, rel, tgt)
            rows.append(ys_ref[0, ei, pl.ds(start, fslots), :])
        onehot = jnp.where(tgt == lane, 1.0, 0.0).astype(BF16)
        acc = acc + _dot(onehot, jnp.concatenate(rows, axis=0))
    o_ref[0] = x_ref[0] + gate * acc
    for ei in range(e):
        @pl.when(jnp.logical_not(fits_fast(ei)))
        def _wide():
            o_ref[0] += gate * wide_term(ei)


def _combine(xc, post, ys, mod, tile0, n_tiles, shared_slots):
    bsz, tc, d = xc.shape
    tt = TOKEN_TILE
    _, e, cap, _ = ys.shape
    is_ctx = 1 if tile0 > 0 else 0
    ys_map = (lambda b, j, lo, hi: (0, 0, 0, 0)) if shared_slots else (lambda b, j, lo, hi: (b, 0, 0, 0))
    wide = min(cap, tt + BF16_ROWS)
    fslots = COMBINE_FAST_SLOTS
    pack = COMBINE_PACK if (e % COMBINE_PACK == 0 and cap > wide) else 0
    assert (cap - wide) % BF16_ROWS == 0 and (not pack or (cap - fslots) % BF16_ROWS == 0)
    pr = post.reshape(bsz, n_tiles, tt, e)
    first = jnp.min(jnp.where(pr >= 0, pr, cap), axis=2)
    first = jnp.where(first >= cap, 0, first).reshape(-1)
    last = (jnp.max(pr, axis=2) + 1).reshape(-1)
    return pl.pallas_call(
        functools.partial(_combine_kernel, wide, pack, fslots),
        grid_spec=pltpu.PrefetchScalarGridSpec(
            num_scalar_prefetch=2,
            grid=(bsz, n_tiles),
            in_specs=[
                pl.BlockSpec((1, tt, d), lambda b, j, lo, hi: (b, tile0 + j, 0)),
                pl.BlockSpec((1, tt, e), lambda b, j, lo, hi: (b, j, 0)),
                pl.BlockSpec((1, e, cap, d), ys_map),
                pl.BlockSpec((1, 1, 6, d), lambda b, j, lo, hi: (b, is_ctx, 0, 0)),
            ],
            out_specs=pl.BlockSpec((1, tt, d), lambda b, j, lo, hi: (b, tile0 + j, 0)),
        ),
        out_shape=jax.ShapeDtypeStruct((bsz, tc, d), F32),
        input_output_aliases={2: 0},
        name="moe_combine",
        compiler_params=_cparams(("parallel", "arbitrary")),
    )(first, last, xc, post, ys, mod)


def _moe(xc, mod, gn, router_t, layer, w1, w3, w2, nl, nct, seq, ctx_len, do_ctx):
    bsz, tc, d = xc.shape
    e = router_t.shape[0]
    h, aff = _router(xc, mod, gn, router_t, 0, nl)
    cap_l = CAPACITY_FACTOR * seq // e
    pos, post, gate = _select(aff, cap_l, 0)
    ys = _ffn(h, pos.reshape(bsz, e, 1, seq), gate.reshape(bsz, e, 1, seq), layer, w1, w3, w2, cap_l)
    xc = _combine(xc, post, ys, mod, 0, nl, False)
    if do_ctx:
        h_ctx, aff_ctx = _router(xc, mod, gn, router_t, nl, nct)
        cap_c = CAPACITY_FACTOR * ctx_len // e
        pos, post, gate = _select(aff_ctx, cap_c, cap_c)
        pos = jnp.transpose(pos, (1, 0, 2)).reshape(1, e, 1, bsz * ctx_len)
        gate = jnp.transpose(gate, (1, 0, 2)).reshape(1, e, 1, bsz * ctx_len)
        ys = _ffn(h_ctx.reshape(1, bsz * ctx_len, d), pos, gate, layer, w1, w3, w2, bsz * cap_c)
        xc = _combine(xc, post, ys, mod, nl, nct, True)
    return xc


def _final_kernel(x_ref, g_ref, o_ref):
    x = x_ref[0]
    ms = jnp.mean(x * x, axis=-1, keepdims=True)
    o_ref[0] = x * lax.rsqrt(ms + RMS_EPS) * g_ref[...]


def _final_norm(xc, g, seq):
    bsz, tc, d = xc.shape
    tt = TOKEN_TILE
    tile = pl.BlockSpec((1, tt, d), lambda b, j: (b, j, 0))
    return pl.pallas_call(
        _final_kernel,
        grid=(bsz, seq // tt),
        in_specs=[tile, pl.BlockSpec(g.shape, lambda b, j: (0, 0))],
        out_specs=tile,
        out_shape=jax.ShapeDtypeStruct((bsz, seq, d), F32),
        name="final_norm",
        compiler_params=_cparams(("parallel", "arbitrary")),
    )(xc, g)


def kernel(x, c, ctx, c_ctx, ada_w, ada_b, norm1_g, norm2_g, rwkv_mix, rwkv_wrkv, rwkv_w0, rwkv_w1, rwkv_w2, rwkv_a0, rwkv_a1, rwkv_a2, rwkv_v0, rwkv_v1, rwkv_v2, rwkv_g1, rwkv_g2, rwkv_kk, rwkv_ka, rwkv_rk, rwkv_lnw, rwkv_lnb, rwkv_wo, pool_w, pool_scale, moe_router, moe_w1, moe_w3, moe_w2, final_g):
    bsz, seq, d = x.shape
    ctx_len = ctx.shape[1]
    depth = ada_w.shape[0]
    n_heads, head = rwkv_rk.shape[1], rwkv_rk.shape[2]
    n_mixers = 2
    tt = TOKEN_TILE
    assert seq % tt == 0 and ctx_len % tt == 0 and tt % GRID_W == 0
    assert head == WKV_CHUNK and 2 * head == LANES and d % LANES == 0
    nl, nct = seq // tt, ctx_len // tt

    xc = jnp.concatenate([x, ctx], axis=1)

    rows = -(-(bsz + 1) // SUBLANES) * SUBLANES
    cond = jnp.zeros((rows, d), F32).at[:bsz].set(c).at[bsz].set(c_ctx)
    m_all = _ada_all(cond, ada_w, ada_b)
    m_lat = m_all[:, :bsz].reshape(depth, bsz, 1, 6, d)
    m_ctx = jnp.broadcast_to(m_all[:, bsz].reshape(depth, 1, 1, 6, d), (depth, bsz, 1, 6, d))
    mods = jnp.concatenate([m_lat, m_ctx], axis=2)

    head_of = jnp.arange(d) // head
    hs = (head_of[:, None] == jnp.arange(HEAD_COLS)[None, :]).astype(BF16)
    hst = jnp.transpose(hs)

    w1_all, w3_all, w2_all = moe_w1.astype(BF16), moe_w3.astype(BF16), moe_w2.astype(BF16)
    v_first = None
    for i in range(depth):
        last = i == depth - 1
        is_rwkv = i % n_mixers == 0
        jn = i // n_mixers
        mod = mods[i]
        gn1 = norm1_g[i].reshape(1, d)
        if is_rwkv:
            p = {
                'mix': rwkv_mix[jn],
                'wr': rwkv_wrkv[jn, 0].astype(BF16), 'wk': rwkv_wrkv[jn, 1].astype(BF16),
                'wv': rwkv_wrkv[jn, 2].astype(BF16),
                'g1': rwkv_g1[jn].astype(BF16), 'g2': rwkv_g2[jn].astype(BF16),
                'w1': rwkv_w1[jn].astype(BF16), 'w2': rwkv_w2[jn].astype(BF16), 'w0': rwkv_w0[jn],
                'a1': rwkv_a1[jn].astype(BF16), 'a2': rwkv_a2[jn].astype(BF16), 'a0': rwkv_a0[jn],
                'k_k': rwkv_kk[jn].reshape(1, d), 'k_a': rwkv_ka[jn].reshape(1, d),
                'r_k': rwkv_rk[jn].reshape(1, d), 'ln_w': rwkv_lnw[jn].reshape(1, d),
                'ln_b': rwkv_lnb[jn].reshape(1, d), 'wo': rwkv_wo[jn].astype(BF16),
                'hs': hs, 'hst': hst,
            }
            vres = None if jn == 0 else (rwkv_v0[jn - 1].reshape(1, d), rwkv_v1[jn - 1].astype(BF16),
                                         rwkv_v2[jn - 1].astype(BF16))
            r, v, g, kk, lw0, lw1, b0, b1, kd0, kd1, bsum = _rwkv_proj(xc, mod, gn1, p, vres, v_first,
                                                                       nl, seq, ctx_len)
            if v_first is None:
                v_first = v
            yf, yr = _wkv(r, v, kk, lw0, b0, kd0, lw1, b1, kd1, seq, ctx_len)
            xc = _rwkv_out(xc, yf, yr, v, g, bsum, mod, p, nl, head)
        else:
            n_tiles = nl if last else nl + nct
            xc = _pool(xc, mod, gn1, pool_w[jn].astype(BF16), pool_scale[jn].reshape(1, d),
                       nl, seq, ctx_len, n_tiles)
        xc = _moe(xc, mod, norm2_g[i].reshape(1, d), jnp.transpose(moe_router[i]),
                  i, w1_all, w3_all, w2_all, nl, nct, seq, ctx_len, not last)
    return _final_norm(xc, final_g.reshape(1, d), seq)
```

```python
import functools
import math

import jax
import jax.numpy as jnp
from jax import lax
from jax.experimental import pallas as pl
from jax.experimental.pallas import tpu as pltpu

F32 = jnp.float32
BF16 = jnp.bfloat16
I32 = jnp.int32
HIGHEST = lax.Precision.HIGHEST

GRID_W = 64
POOL_WINDOWS = (2, 4, 8, 16)
CAPACITY_FACTOR = 2
RMS_EPS = 1e-6
GN_EPS = 64e-5
EXP_NEG_HALF = math.exp(-0.5)

LANES = 128
SUBLANES = 8
BF16_ROWS = 16
VMEM_LIMIT_BYTES = 56 * 1024 * 1024

TOKEN_TILE = 256
GATHER_TILE = 256
GATHER_FAST_ROWS = 56
COMBINE_PACK = 4
COMBINE_FAST_SLOTS = 64
WKV_CHUNK = 64
WKV_PAIRS_PER_STEP = 8
POOL_HALO = 8
HEAD_COLS = LANES


def _cparams(sem):
    return pltpu.CompilerParams(dimension_semantics=sem, vmem_limit_bytes=VMEM_LIMIT_BYTES)


def _dot(a, b):
    return jnp.dot(a, b, preferred_element_type=F32)


def _dot_nt(a, b):
    return lax.dot_general(a, b, (((1,), (1,)), ((), ())), preferred_element_type=F32)


def _dot_tn(a, b):
    return lax.dot_general(a, b, (((0,), (0,)), ((), ())), preferred_element_type=F32)


def _split3(x):
    hi = x.astype(BF16)
    r1 = x - hi.astype(F32)
    mid = r1.astype(BF16)
    lo = (r1 - mid.astype(F32)).astype(BF16)
    return hi, mid, lo


def _dot_split(x, m, pieces):
    hi = x.astype(BF16)
    out = _dot(hi, m)
    for _ in range(pieces - 1):
        x = x - hi.astype(F32)
        hi = x.astype(BF16)
        out = out + _dot(hi, m)
    return out


def _dot3_left(m, x):
    w = x.shape[1]
    z = _dot(m, jnp.concatenate(_split3(x), axis=1))
    return z[:, 0:w] + z[:, w:2 * w] + z[:, 2 * w:3 * w]


def _sigmoid(x):
    return 0.5 * jnp.tanh(0.5 * x) + 0.5


def _norm_mod(x, g, shift, scale):
    ms = jnp.mean(x * x, axis=-1, keepdims=True)
    return x * lax.rsqrt(ms + RMS_EPS) * (g * (1.0 + scale)) + shift


def _ada_kernel(c_ref, w_ref, b_ref, o_ref):
    c = c_ref[...]
    s = c * _sigmoid(c)
    o_ref[0] = jnp.dot(s, w_ref[0], precision=HIGHEST, preferred_element_type=F32) + b_ref[0]


def _ada_all(cond, ada_w, ada_b):
    depth, d, n6 = ada_w.shape
    rows = cond.shape[0]
    nt = 512
    return pl.pallas_call(
        _ada_kernel,
        grid=(depth, n6 // nt),
        in_specs=[
            pl.BlockSpec((rows, d), lambda i, n: (0, 0)),
            pl.BlockSpec((1, d, nt), lambda i, n: (i, 0, n)),
            pl.BlockSpec((1, 1, nt), lambda i, n: (i, 0, n)),
        ],
        out_specs=pl.BlockSpec((1, rows, nt), lambda i, n: (i, 0, n)),
        out_shape=jax.ShapeDtypeStruct((depth, rows, n6), F32),
        name="ada_mod",
        compiler_params=_cparams(("arbitrary", "arbitrary")),
    )(cond, ada_w, ada_b.reshape(depth, 1, n6))


def _rwkv_proj_kernel(nl, seq, ctx_len, has_vres, *refs):
    if has_vres:
        (x_ref, xp_ref, xn_ref, mod_ref, gn_ref, mix_ref, wr_ref, wk_ref, wv_ref, g1_ref, g2_ref,
         w1_ref, w2_ref, w0_ref, a1_ref, a2_ref, a0_ref, kkw_ref, kaw_ref, rk_ref, hs_ref, hst_ref,
         v0_ref, v1_ref, v2_ref, vf_ref,
         r_o, v_o, g_o, kk_o, lw0_o, lw1_o, b0_o, b1_o, kd0_o, kd1_o, bs_o, ext_scr, sh_scr) = refs
    else:
        (x_ref, xp_ref, xn_ref, mod_ref, gn_ref, mix_ref, wr_ref, wk_ref, wv_ref, g1_ref, g2_ref,
         w1_ref, w2_ref, w0_ref, a1_ref, a2_ref, a0_ref, kkw_ref, kaw_ref, rk_ref, hs_ref, hst_ref,
         r_o, v_o, g_o, kk_o, lw0_o, lw1_o, b0_o, b1_o, kd0_o, kd1_o, bs_o, ext_scr, sh_scr) = refs
    tt = x_ref.shape[1]
    d = x_ref.shape[2]
    hw = GRID_W
    j = pl.program_id(1)
    shift = mod_ref[0, 0, 0:1, :]
    scale = mod_ref[0, 0, 1:2, :]
    gn = gn_ref[...]
    ext_scr[0:hw, :] = _norm_mod(xp_ref[0], gn, shift, scale)
    ext_scr[hw:hw + tt, :] = _norm_mod(x_ref[0], gn, shift, scale)
    ext_scr[hw + tt:, :] = _norm_mod(xn_ref[0], gn, shift, scale)

    i = lax.broadcasted_iota(I32, (tt, 1), 0)
    q = d // 4

    @pl.when(j < nl)
    def _latent_shift():
        t = j * tt + i
        col = i % hw
        sh_scr[:, 0:q] = jnp.where(col != 0, ext_scr[hw - 1:hw - 1 + tt, 0:q], 0.0)
        sh_scr[:, q:2 * q] = jnp.where(col != hw - 1, ext_scr[hw + 1:hw + 1 + tt, q:2 * q], 0.0)
        sh_scr[:, 2 * q:3 * q] = jnp.where(t >= hw, ext_scr[0:tt, 2 * q:3 * q], 0.0)
        sh_scr[:, 3 * q:] = jnp.where(t < seq - hw, ext_scr[2 * hw:2 * hw + tt, 3 * q:], 0.0)

    @pl.when(j >= nl)
    def _context_shift():
        t = (j - nl) * tt + i
        hd = d // 2
        sh_scr[:, 0:hd] = jnp.where(t != 0, ext_scr[hw - 1:hw - 1 + tt, 0:hd], 0.0)
        sh_scr[:, hd:] = jnp.where(t != ctx_len - 1, ext_scr[hw + 1:hw + 1 + tt, hd:], 0.0)

    h = ext_scr[hw:hw + tt, :]
    xx = sh_scr[...] - h

    def mixed(n):
        return (h + xx * mix_ref[n:n + 1, :]).astype(BF16)

    xr, xw, xk, xv, xa, xg = [mixed(n) for n in range(6)]
    r = _dot(xr, wr_ref[...])
    k = _dot(xk, wk_ref[...])
    v = _dot(xv, wv_ref[...])
    if has_vres:
        lor = _dot(_dot(xv, v1_ref[...]).astype(BF16), v2_ref[...])
        v = v + (vf_ref[0] - v) * _sigmoid(v0_ref[...] + lor)
    g = _dot(_sigmoid(_dot(xg, g1_ref[...])).astype(BF16), g2_ref[...])
    r_o[0] = r
    v_o[0] = v
    g_o[0] = g

    kkr = k * kkw_ref[...]
    ss = _dot_split(kkr * kkr, hs_ref[...], 1)
    inv = 1.0 / jnp.maximum(jnp.sqrt(ss), 1e-12)
    kk = kkr * _dot_split(inv, hst_ref[...], 2)
    kk_o[0] = kk
    kaw = kaw_ref[...]
    kb = None
    tw = jnp.tanh(_dot(xw, w1_ref[...])).astype(BF16)
    ta = _dot(xa, a1_ref[...]).astype(BF16)
    for dr, (lw_o, b_o, kd_o) in enumerate(((lw0_o, b0_o, kd0_o), (lw1_o, b1_o, kd1_o))):
        wpre = w0_ref[dr:dr + 1, :] + _dot(tw, w2_ref[dr])
        lw_o[0] = -EXP_NEG_HALF * _sigmoid(wpre)
        a = _sigmoid(a0_ref[dr:dr + 1, :] + _dot(ta, a2_ref[dr]))
        b_o[0] = kk * a
        kd = k * (1.0 + (a - 1.0) * kaw)
        kd_o[0] = kd
        kb = 0.5 * kd if kb is None else kb + 0.5 * kd
    bs_o[0] = _dot_split(r * kb * rk_ref[...], hs_ref[...], 1)


def _lora_in(w):
    return jnp.concatenate([w[i] for i in range(w.shape[0])], axis=1).astype(BF16)


def _lora_out(w):
    n, r, _ = w.shape
    rows = jnp.arange(n * r) // r
    return jnp.where((rows[None, :] == jnp.arange(n)[:, None])[:, :, None],
                     jnp.tile(w, (1, n, 1)), 0.0).astype(BF16)


def _rwkv_proj(xc, mod, gn, p, vres, v_first, nl, seq, ctx_len):
    bsz, tc, d = xc.shape
    tt = TOKEN_TILE
    nt = tc // tt
    hb = tt // GRID_W
    nhb = tc // GRID_W
    has_vres = vres is not None

    def full(a):
        nd = a.ndim
        return pl.BlockSpec(a.shape, lambda b, j, _n=nd: (0,) * _n)

    tile = pl.BlockSpec((1, tt, d), lambda b, j: (b, j, 0))
    ins = [xc, xc, xc, mod, gn, p['mix'], p['wr'], p['wk'], p['wv'], p['g1'], p['g2'],
           p['w1'], p['w2'], p['w0'], p['a1'], p['a2'], p['a0'], p['k_k'], p['k_a'], p['r_k'], p['hs'], p['hst']]
    specs = [
        tile,
        pl.BlockSpec((1, GRID_W, d), lambda b, j: (b, jnp.maximum(j * hb - 1, 0), 0)),
        pl.BlockSpec((1, GRID_W, d), lambda b, j: (b, jnp.minimum((j + 1) * hb, nhb - 1), 0)),
        pl.BlockSpec((1, 1, 6, d), lambda b, j: (b, (j >= nl).astype(I32), 0, 0)),
    ] + [full(a) for a in ins[4:]]
    if has_vres:
        ins += [vres[0], vres[1], vres[2], v_first]
        specs += [full(vres[0]), full(vres[1]), full(vres[2]), tile]
    out_sds = jax.ShapeDtypeStruct((bsz, tc, d), F32)
    return pl.pallas_call(
        functools.partial(_rwkv_proj_kernel, nl, seq, ctx_len, has_vres),
        grid=(bsz, nt),
        in_specs=specs,
        out_specs=[tile] * 10 + [pl.BlockSpec((1, tt, HEAD_COLS), lambda b, j: (b, j, 0))],
        out_shape=[out_sds] * 10 + [jax.ShapeDtypeStruct((bsz, tc, HEAD_COLS), F32)],
        scratch_shapes=[pltpu.VMEM((tt + 2 * GRID_W, d), F32), pltpu.VMEM((tt, d), F32)],
        name="rwkv_proj",
        compiler_params=_cparams(("parallel", "arbitrary")),
    )(*ins)


def _wkv_chains(chains, masks):
    nc = len(chains)
    ks = range(nc)
    L = chains[0][0].shape[0]
    r, v, kk, lw, b, kd, s_prev, rev = [[ch[i] for ch in chains] for i in range(8)]
    mk = [masks[1] if rv else masks[0] for rv in rev]
    tri, m0, m1, strict, incl, eye2, bd = [[m[i] for m in mk] for i in range(7)]

    def stack(k, x):
        xb = x.astype(BF16)
        return jnp.concatenate([xb * m0[k], xb * m1[k]], axis=0)

    c = [_dot3_left(tri[k], lw[k]) for k in ks]
    ctot = [c[k][0:1, :] if rev[k] else c[k][L - 1:L, :] for k in ks]
    e_c = [jnp.exp(c[k]) for k in ks]
    e_nc = [jnp.exp(-c[k]) for k in ks]
    e_tc = [jnp.exp(ctot[k] - c[k]) for k in ks]
    ah = [-kk[k] * jnp.exp(c[k] - lw[k]) for k in ks]
    rh = [r[k] * e_c[k] for k in ks]
    lhs = [jnp.concatenate([ah[k], rh[k]], axis=0).astype(BF16) for k in ks]
    rhs = [jnp.concatenate([stack(k, b[k] * e_nc[k]), stack(k, kd[k] * e_nc[k])], axis=0) for k in ks]
    aa = [_dot_nt(lhs[k], rhs[k]) for k in ks]
    a_ab = [jnp.where(strict[k], aa[k][0:L, 0:2 * L], 0.0) for k in ks]
    a_ak = [jnp.where(strict[k], aa[k][0:L, 2 * L:4 * L], 0.0).astype(BF16) for k in ks]
    a_r = [jnp.where(jnp.concatenate([incl[k], incl[k]], axis=1), aa[k][L:2 * L, :], 0.0).astype(BF16)
           for k in ks]

    n_dbl = int(math.log2(L))
    tm = [eye2[k] + a_ab[k] for k in ks]
    pw = [_dot(a_ab[k].astype(BF16), stack(k, a_ab[k])) for k in ks]
    for _ in range(n_dbl - 2):
        z = [_dot(pw[k].astype(BF16), jnp.concatenate([stack(k, tm[k]), stack(k, pw[k])], axis=1)) for k in ks]
        tm = [tm[k] + z[k][:, 0:2 * L] for k in ks]
        pw = [z[k][:, 2 * L:4 * L] for k in ks]
    tm = [tm[k] + _dot(pw[k].astype(BF16), stack(k, tm[k])) for k in ks]

    ss0 = [_dot_nt(lhs[k], s_prev[k].astype(BF16)) for k in ks]
    vs = [stack(k, v[k]) for k in ks]
    wmat = [ss0[k][0:L] + _dot(a_ak[k], vs[k]) for k in ks]
    u = [_dot(tm[k].astype(BF16), stack(k, wmat[k])) for k in ks]
    y = [ss0[k][L:2 * L] + _dot(a_r[k], jnp.concatenate([stack(k, u[k]), vs[k]], axis=0)) for k in ks]
    uv = [jnp.concatenate([u[k], v[k]], axis=0).astype(BF16) for k in ks]
    bk = [jnp.concatenate([b[k] * e_tc[k], kd[k] * e_tc[k]], axis=0).astype(BF16) for k in ks]
    upd = [_dot_tn(uv[k], bk[k]) for k in ks]
    s_new = [s_prev[k] * jnp.exp(ctot[k]) + jnp.where(bd[k], upd[k], 0.0) for k in ks]
    return y, s_new


def _wkv_masks(L, w2, reverse):
    n = w2 // 2
    row = lax.broadcasted_iota(I32, (L, L), 0)
    colm = lax.broadcasted_iota(I32, (L, L), 1)
    tri = jnp.where((colm >= row) if reverse else (colm <= row), 1.0, 0.0).astype(BF16)
    lane = lax.broadcasted_iota(I32, (1, w2), 1)
    m0 = jnp.where(lane < n, 1.0, 0.0).astype(BF16)
    m1 = jnp.where(lane < n, 0.0, 1.0).astype(BF16)
    t_i = lax.broadcasted_iota(I32, (L, 2 * L), 0)
    s_i = lax.broadcasted_iota(I32, (L, 2 * L), 1) % L
    strict = (s_i > t_i) if reverse else (s_i < t_i)
    incl = (s_i >= t_i) if reverse else (s_i <= t_i)
    eye2 = jnp.where(s_i == t_i, 1.0, 0.0)
    ri = lax.broadcasted_iota(I32, (w2, w2), 0)
    ci = lax.broadcasted_iota(I32, (w2, w2), 1)
    bd = (ri < n) == (ci < n)
    return tri, m0, m1, strict, incl, eye2, bd


def _wkv_kernel(rf, vf, kkf, lwf, bf, kdf, rr, vr, kkr, lwr, br, kdr, yf_o, yr_o, s_scr):
    @pl.when(pl.program_id(2) == 0)
    def _init():
        s_scr[...] = jnp.zeros_like(s_scr)

    L = rf.shape[1]
    masks = (_wkv_masks(L, LANES, False), _wkv_masks(L, LANES, True))
    chains = []
    for hp in range(rf.shape[2] // LANES):
        ls = slice(hp * LANES, (hp + 1) * LANES)
        chains.append((rf[0, :, ls], vf[0, :, ls], kkf[0, :, ls], lwf[0, :, ls], bf[0, :, ls],
                       kdf[0, :, ls], s_scr[0, hp], False))
        chains.append((rr[0, :, ls], vr[0, :, ls], kkr[0, :, ls], lwr[0, :, ls], br[0, :, ls],
                       kdr[0, :, ls], s_scr[1, hp], True))
    ys, ss = _wkv_chains(chains, masks)
    for hp in range(rf.shape[2] // LANES):
        ls = slice(hp * LANES, (hp + 1) * LANES)
        yf_o[0, :, ls] = ys[2 * hp]
        yr_o[0, :, ls] = ys[2 * hp + 1]
        s_scr[0, hp] = ss[2 * hp]
        s_scr[1, hp] = ss[2 * hp + 1]


def _wkv(r, v, kk, lw0, b0, kd0, lw1, b1, kd1, seq, ctx_len):
    bsz, tc, d = r.shape
    L = WKV_CHUNK
    nlc = seq // L
    ncc = ctx_len // L
    nch = nlc + ncc
    gp = min(WKV_PAIRS_PER_STEP, d // LANES)
    width = gp * LANES

    def fwd_map(b, h, c):
        return (b, jnp.where(c < ncc, nlc + c, c - ncc), h)

    def rev_map(b, h, c):
        return (b, jnp.where(c < ncc, nlc + ncc - 1 - c, nlc - 1 - (c - ncc)), h)

    fs = pl.BlockSpec((1, L, width), fwd_map)
    rs = pl.BlockSpec((1, L, width), rev_map)
    sds = jax.ShapeDtypeStruct((bsz, tc, d), F32)
    return pl.pallas_call(
        _wkv_kernel,
        grid=(bsz, d // width, nch),
        in_specs=[fs] * 6 + [rs] * 6,
        out_specs=[fs, rs],
        out_shape=[sds, sds],
        scratch_shapes=[pltpu.VMEM((2, gp, LANES, LANES), F32)],
        name="wkv_scan",
        compiler_params=_cparams(("parallel", "parallel", "arbitrary")),
    )(r, v, kk, lw0, b0, kd0, r, v, kk, lw1, b1, kd1)


def _rwkv_out_kernel(head, x_ref, yf_ref, yr_ref, v_ref, g_ref, bs_ref, mod_ref,
                     lnw_ref, lnb_ref, wo_ref, hs_ref, hst_ref, o_ref):
    hs = hs_ref[...]
    hst = hst_ref[...]
    o = yf_ref[0] + yr_ref[0]
    inv_n = 1.0 / head
    mu = _dot_split(_dot_split(o, hs, 2) * inv_n, hst, 2)
    dlt = o - mu
    var = _dot_split(dlt * dlt, hs, 1) * inv_n
    on = dlt * _dot_split(lax.rsqrt(var + GN_EPS), hst, 2)
    on = on * lnw_ref[...] + lnb_ref[...]
    bonus = _dot_split(bs_ref[0], hst, 2) * v_ref[0]
    y = ((on + bonus) * g_ref[0]).astype(BF16)
    gate = mod_ref[0, 0, 2:3, :]
    o_ref[0] = x_ref[0] + gate * _dot(y, wo_ref[...])


def _rwkv_out(xc, yf, yr, v, g, bsum, mod, p, nl, head):
    bsz, tc, d = xc.shape
    tt = TOKEN_TILE
    tile = pl.BlockSpec((1, tt, d), lambda b, j: (b, j, 0))

    def full(a):
        nd = a.ndim
        return pl.BlockSpec(a.shape, lambda b, j, _n=nd: (0,) * _n)

    consts = [p['ln_w'], p['ln_b'], p['wo'], p['hs'], p['hst']]
    return pl.pallas_call(
        functools.partial(_rwkv_out_kernel, head),
        grid=(bsz, tc // tt),
        in_specs=[tile] * 5 + [pl.BlockSpec((1, tt, HEAD_COLS), lambda b, j: (b, j, 0)),
                               pl.BlockSpec((1, 1, 6, d), lambda b, j: (b, (j >= nl).astype(I32), 0, 0))]
        + [full(a) for a in consts],
        out_specs=tile,
        out_shape=jax.ShapeDtypeStruct((bsz, tc, d), F32),
        input_output_aliases={0: 0},
        name="rwkv_out",
        compiler_params=_cparams(("parallel", "arbitrary")),
    )(xc, yf, yr, v, g, bsum, mod, *consts)


def _pool_kernel(nl, seq, ctx_len, x_ref, xp_ref, xn_ref, mod_ref, gn_ref, w_ref, sc_ref, o_ref, ext_scr):
    tt = x_ref.shape[1]
    d = x_ref.shape[2]
    ph = POOL_HALO
    j = pl.program_id(1)
    nct = ctx_len // tt
    shift = mod_ref[0, 0, 0:1, :]
    scale = mod_ref[0, 0, 1:2, :]
    gn = gn_ref[...]
    first = jnp.logical_or(j == 0, j == nl)
    last = jnp.logical_or(j == nl - 1, j == nl + nct - 1)
    hp = _norm_mod(xp_ref[0], gn, shift, scale)
    hn = _norm_mod(xn_ref[0], gn, shift, scale)
    ext_scr[0:ph, :] = jnp.where(first, 0.0, hp)
    ext_scr[ph:ph + tt, :] = _norm_mod(x_ref[0], gn, shift, scale)
    ext_scr[ph + tt:, :] = jnp.where(last, 0.0, hn)

    i = lax.broadcasted_iota(I32, (tt, 1), 0)
    t = jnp.where(j < nl, j * tt + i, (j - nl) * tt + i)
    tseg = jnp.where(j < nl, seq, ctx_len)
    ng = len(POOL_WINDOWS)
    dg = d // ng
    gate = mod_ref[0, 0, 2:3, :]
    for gi, win in enumerate(POOL_WINDOWS):
        half = win // 2
        cs = slice(gi * dg, (gi + 1) * dg)
        acc = ext_scr[ph - half:ph - half + tt, cs]
        for o in range(1, win):
            acc = acc + ext_scr[ph - half + o:ph - half + o + tt, cs]
        cnt = (jnp.minimum(t + half, tseg) - jnp.maximum(t - half, 0)).astype(F32)
        hg = ext_scr[ph:ph + tt, cs]
        dlt = (acc / cnt - hg).astype(BF16)
        y = _dot(dlt, w_ref[gi]) * sc_ref[:, cs]
        o_ref[0, :, cs] = x_ref[0, :, cs] + gate[:, cs] * y


def _pool(xc, mod, gn, w, sc, nl, seq, ctx_len, n_tiles):
    bsz, tc, d = xc.shape
    tt = TOKEN_TILE
    hb = tt // POOL_HALO
    nhb = tc // POOL_HALO
    tile = pl.BlockSpec((1, tt, d), lambda b, j: (b, j, 0))
    return pl.pallas_call(
        functools.partial(_pool_kernel, nl, seq, ctx_len),
        grid=(bsz, n_tiles),
        in_specs=[
            tile,
            pl.BlockSpec((1, POOL_HALO, d), lambda b, j: (b, jnp.maximum(j * hb - 1, 0), 0)),
            pl.BlockSpec((1, POOL_HALO, d), lambda b, j: (b, jnp.minimum((j + 1) * hb, nhb - 1), 0)),
            pl.BlockSpec((1, 1, 6, d), lambda b, j: (b, (j >= nl).astype(I32), 0, 0)),
            pl.BlockSpec(gn.shape, lambda b, j: (0, 0)),
            pl.BlockSpec(w.shape, lambda b, j: (0, 0, 0)),
            pl.BlockSpec(sc.shape, lambda b, j: (0, 0)),
        ],
        out_specs=tile,
        out_shape=jax.ShapeDtypeStruct((bsz, n_tiles * tt, d), F32),
        scratch_shapes=[pltpu.VMEM((tt + 2 * POOL_HALO, d), F32)],
        name="pool_mix",
        compiler_params=_cparams(("parallel", "arbitrary")),
    )(xc, xc, xc, mod, gn, w, sc)


def _router_kernel(x_ref, mod_ref, gn_ref, rth_ref, rtl_ref, h_o, aff_o):
    h = _norm_mod(x_ref[0], gn_ref[...], mod_ref[0, 0, 3:4, :], mod_ref[0, 0, 4:5, :])
    hb = h.astype(BF16)
    h_o[0] = hb
    hl = (h - hb.astype(F32)).astype(BF16)
    logits = _dot_nt(rth_ref[...], hb) + (_dot_nt(rth_ref[...], hl) + _dot_nt(rtl_ref[...], hb))
    m = jnp.max(logits, axis=0, keepdims=True)
    ex = jnp.exp(logits - m)
    aff_o[0] = ex / jnp.sum(ex, axis=0, keepdims=True)


def _router(xc, mod, gn, router_t, tile0, n_tiles):
    bsz, tc, d = xc.shape
    tt = TOKEN_TILE
    e = router_t.shape[0]
    is_ctx = 1 if tile0 > 0 else 0
    rt_hi = router_t.astype(BF16)
    return pl.pallas_call(
        _router_kernel,
        grid=(bsz, n_tiles),
        in_specs=[
            pl.BlockSpec((1, tt, d), lambda b, j: (b, tile0 + j, 0)),
            pl.BlockSpec((1, 1, 6, d), lambda b, j: (b, is_ctx, 0, 0)),
            pl.BlockSpec(gn.shape, lambda b, j: (0, 0)),
            pl.BlockSpec(router_t.shape, lambda b, j: (0, 0)),
            pl.BlockSpec(router_t.shape, lambda b, j: (0, 0)),
        ],
        out_specs=[pl.BlockSpec((1, tt, d), lambda b, j: (b, j, 0)),
                   pl.BlockSpec((1, e, tt), lambda b, j: (b, 0, j))],
        out_shape=[jax.ShapeDtypeStruct((bsz, n_tiles * tt, d), BF16),
                   jax.ShapeDtypeStruct((bsz, e, n_tiles * tt), F32)],
        name="moe_router",
        compiler_params=_cparams(("parallel", "arbitrary")),
    )(xc, mod, gn, rt_hi, (router_t - rt_hi.astype(F32)).astype(BF16))


def _lane_cumsum_excl(x, utri):
    e, t = x.shape
    off = jnp.zeros((e, 1), F32)
    parts = []
    for kb in range(t // LANES):
        blk = x[:, kb * LANES:(kb + 1) * LANES]
        inc = _dot(blk.astype(BF16), utri)
        parts.append(inc - blk + off)
        off = off + inc[:, LANES - 1:LANES]
    return jnp.concatenate(parts, axis=1), off


def _select_kernel(cap, slot_stride, aff_ref, pos_o, post_o, gate_o):
    a = aff_ref[0]
    e, t = a.shape
    bits = pltpu.bitcast(a, I32)

    def body(it, thr):
        cand = thr | lax.shift_left(jnp.int32(1), jnp.int32(29) - it)
        cnt = jnp.sum(jnp.where(bits >= cand, 1.0, 0.0), axis=1, keepdims=True)
        return jnp.where(cnt >= cap, cand, thr)

    thr = lax.fori_loop(0, 30, body, jnp.zeros((e, 1), I32))
    gt = jnp.where(bits > thr, 1.0, 0.0)
    eq = jnp.where(bits == thr, 1.0, 0.0)
    ri = lax.broadcasted_iota(I32, (LANES, LANES), 0)
    ci = lax.broadcasted_iota(I32, (LANES, LANES), 1)
    utri = jnp.where(ri <= ci, 1.0, 0.0).astype(BF16)
    n_gt = jnp.sum(gt, axis=1, keepdims=True)
    eq_rank, _ = _lane_cumsum_excl(eq, utri)
    sel = gt + eq * jnp.where(eq_rank < cap - n_gt, 1.0, 0.0)
    pos, _ = _lane_cumsum_excl(sel, utri)
    base = pl.program_id(0) * slot_stride
    posm = jnp.where(sel > 0.5, pos.astype(I32) + base, -1)
    pos_o[0] = posm
    post_o[0] = jnp.transpose(posm.astype(F32)).astype(I32)
    gate_o[0] = jnp.where(sel > 0.5, a, 0.0)


def _select(aff, cap, slot_stride):
    bsz, e, t = aff.shape
    blk = pl.BlockSpec((1, e, t), lambda b: (b, 0, 0))
    return pl.pallas_call(
        functools.partial(_select_kernel, cap, slot_stride),
        grid=(bsz,),
        in_specs=[blk],
        out_specs=[blk, pl.BlockSpec((1, t, e), lambda b: (b, 0, 0)), blk],
        out_shape=[jax.ShapeDtypeStruct((bsz, e, t), I32),
                   jax.ShapeDtypeStruct((bsz, t, e), I32),
                   jax.ShapeDtypeStruct((bsz, e, t), F32)],
        name="moe_select",
        compiler_params=_cparams(("parallel",)),
    )(aff)


def _ffn_kernel(cap, win, fwin, tk, fk, lo_ref, hi_ref, h_ref, pos_ref, gate_ref, w1_ref, w3_ref, w2_ref,
                ys_o, xin_scr, gsl_scr):
    t = h_ref.shape[1]
    f = w1_ref.shape[3]
    nkt = t // tk
    pos = pos_ref[0, 0]
    g_hi, g_mid, g_lo = _split3(gate_ref[0, 0])
    prow = lax.broadcasted_iota(I32, (SUBLANES, t), 0)
    g8 = jnp.where(prow == 0, g_hi.astype(F32),
                   jnp.where(prow == 1, g_mid.astype(F32),
                             jnp.where(prow == 2, g_lo.astype(F32), 0.0))).astype(BF16)
    xin_scr[...] = jnp.zeros_like(xin_scr)
    gsl_scr[...] = jnp.zeros_like(gsl_scr)
    base = (pl.program_id(1) * pl.num_programs(0) + pl.program_id(0)) * nkt

    def gather(kt, rows, enabled):
        ts = slice(kt * tk, (kt + 1) * tk)
        lo = lo_ref[base + kt]
        start = pl.multiple_of(jnp.minimum((lo // SUBLANES) * SUBLANES, cap - rows), SUBLANES)
        slot = lax.broadcasted_iota(I32, (rows, tk), 0)
        rel = jnp.where(enabled, pos[:, ts] - start, -1)
        onehot = jnp.where(rel == slot, 1.0, 0.0).astype(BF16)
        xin_scr[pl.ds(start, rows), :] += _dot(onehot, h_ref[0, ts, :])
        gsl_scr[pl.ds(start, rows), :] += _dot_nt(onehot, g8[:, ts])

    def fits_fast(kt):
        lo = lo_ref[base + kt]
        start = jnp.minimum((lo // SUBLANES) * SUBLANES, cap - fwin)
        return hi_ref[base + kt] <= start + fwin

    for kt in range(nkt):
        gather(kt, fwin, fits_fast(kt) if fwin < win else True)
    for kt in range(nkt if fwin < win else 0):
        @pl.when(jnp.logical_not(fits_fast(kt)))
        def _wide():
            gather(kt, win, True)
    gate_slot = jnp.sum(gsl_scr[...], axis=1, keepdims=True)
    xb = xin_scr[...].astype(BF16)
    y = jnp.zeros(xin_scr.shape, F32)
    n_chunks = f // fk

    def up(kf):
        fs = slice(kf * fk, (kf + 1) * fk)
        return _dot(xb, w1_ref[0, 0, :, fs]), _dot(xb, w3_ref[0, 0, :, fs])

    nxt = up(0)
    for kf in range(n_chunks):
        h1, h3 = nxt
        if kf + 1 < n_chunks:
            nxt = up(kf + 1)
        hid = (h1 * _sigmoid(h1) * h3).astype(BF16)
        y = y + _dot(hid, w2_ref[0, 0, kf * fk:(kf + 1) * fk, :])
    ys_o[0, 0] = (y * gate_slot).astype(BF16)


def _ffn(h, posm, gate, layer, w1, w3, w2, cap):
    gsz, t, d = h.shape
    _, e, _, f = w1.shape
    tk = GATHER_TILE
    fk = min(f, 512)
    win = min(cap, tk + SUBLANES)
    fwin = min(win, GATHER_FAST_ROWS)
    assert t % tk == 0 and (cap - win) % SUBLANES == 0 and (cap - fwin) % SUBLANES == 0
    nkt = t // tk
    pr = posm.reshape(gsz, e, nkt, tk)
    first = jnp.min(jnp.where(pr >= 0, pr, cap), axis=3)
    first = jnp.where(first >= cap, 0, first).reshape(-1)
    last = (jnp.max(pr, axis=3) + 1).reshape(-1)
    return pl.pallas_call(
        functools.partial(_ffn_kernel, cap, win, fwin, tk, fk),
        grid_spec=pltpu.PrefetchScalarGridSpec(
            num_scalar_prefetch=2,
            grid=(e, gsz),
            in_specs=[
                pl.BlockSpec((1, t, d), lambda ei, gi, lo, hi: (gi, 0, 0)),
                pl.BlockSpec((1, 1, 1, t), lambda ei, gi, lo, hi: (gi, ei, 0, 0)),
                pl.BlockSpec((1, 1, 1, t), lambda ei, gi, lo, hi: (gi, ei, 0, 0)),
                pl.BlockSpec((1, 1, d, f), lambda ei, gi, lo, hi: (layer, ei, 0, 0)),
                pl.BlockSpec((1, 1, d, f), lambda ei, gi, lo, hi: (layer, ei, 0, 0)),
                pl.BlockSpec((1, 1, f, d), lambda ei, gi, lo, hi: (layer, ei, 0, 0)),
            ],
            out_specs=pl.BlockSpec((1, 1, cap, d), lambda ei, gi, lo, hi: (gi, ei, 0, 0)),
            scratch_shapes=[pltpu.VMEM((cap, d), F32), pltpu.VMEM((cap, SUBLANES), F32)],
        ),
        out_shape=jax.ShapeDtypeStruct((gsz, e, cap, d), BF16),
        name="moe_ffn",
        compiler_params=_cparams(("arbitrary", "arbitrary")),
    )(first, last, h, posm, gate, w1, w3, w2)


def _combine_kernel(wide, pack, fslots, final, lo_ref, hi_ref, x_ref, post_ref, ys_ref, mod_ref, fg_ref,
                    o_ref):
    tt = x_ref.shape[1]
    e = ys_ref.shape[1]
    cap = ys_ref.shape[2]
    base = (pl.program_id(0) * pl.num_programs(1) + pl.program_id(1)) * e
    gate = mod_ref[0, 0, 5:6, :]

    def finish():
        if final:
            xv = o_ref[0]
            ms = jnp.mean(xv * xv, axis=-1, keepdims=True)
            o_ref[0] = xv * lax.rsqrt(ms + RMS_EPS) * fg_ref[...]

    def window_start(ei, rows):
        lo = lo_ref[base + ei]
        return pl.multiple_of(jnp.minimum((lo // BF16_ROWS) * BF16_ROWS, cap - rows), BF16_ROWS)

    def wide_term(ei):
        start = window_start(ei, wide)
        slot = lax.broadcasted_iota(I32, (tt, wide), 1)
        onehot = jnp.where(post_ref[0, :, ei:ei + 1] - start == slot, 1.0, 0.0).astype(BF16)
        return _dot(onehot, ys_ref[0, ei, pl.ds(start, wide), :])

    if not pack:
        acc = wide_term(0)
        for ei in range(1, e):
            acc = acc + wide_term(ei)
        o_ref[0] = x_ref[0] + gate * acc
        finish()
        return

    def fits_fast(ei):
        return hi_ref[base + ei] <= window_start(ei, fslots) + fslots

    lane = lax.broadcasted_iota(I32, (tt, pack * fslots), 1)
    acc = jnp.zeros(x_ref.shape[1:], F32)
    for g in range(e // pack):
        tgt = None
        rows = []
        for q in range(pack):
            ei = g * pack + q
            start = window_start(ei, fslots)
            rel = jnp.where(fits_fast(ei), post_ref[0, :, ei:ei + 1] - start + q * fslots, -1)
            tgt = rel if tgt is None else jnp.where(lane >= q * fslots, rel, tgt)
            rows.append(ys_ref[0, ei, pl.ds(start, fslots), :])
        onehot = jnp.where(tgt == lane, 1.0, 0.0).astype(BF16)
        acc = acc + _dot(onehot, jnp.concatenate(rows, axis=0))
    o_ref[0] = x_ref[0] + gate * acc
    for ei in range(e):
        @pl.when(jnp.logical_not(fits_fast(ei)))
        def _wide():
            o_ref[0] += gate * wide_term(ei)
    finish()


def _combine(xc, post, ys, mod, tile0, n_tiles, shared_slots, final_g=None):
    bsz, tc, d = xc.shape
    tt = TOKEN_TILE
    _, e, cap, _ = ys.shape
    is_ctx = 1 if tile0 > 0 else 0
    ys_map = (lambda b, j, lo, hi: (0, 0, 0, 0)) if shared_slots else (lambda b, j, lo, hi: (b, 0, 0, 0))
    wide = min(cap, tt + BF16_ROWS)
    fslots = COMBINE_FAST_SLOTS
    pack = COMBINE_PACK if (e % COMBINE_PACK == 0 and cap > wide) else 0
    assert (cap - wide) % BF16_ROWS == 0 and (not pack or (cap - fslots) % BF16_ROWS == 0)
    pr = post.reshape(bsz, n_tiles, tt, e)
    first = jnp.min(jnp.where(pr >= 0, pr, cap), axis=2)
    first = jnp.where(first >= cap, 0, first).reshape(-1)
    last = (jnp.max(pr, axis=2) + 1).reshape(-1)
    final = final_g is not None
    fg = final_g if final else jnp.ones((1, d), F32)
    out_tile0 = 0 if final else tile0
    out_rows = n_tiles * tt if final else tc
    return pl.pallas_call(
        functools.partial(_combine_kernel, wide, pack, fslots, final),
        grid_spec=pltpu.PrefetchScalarGridSpec(
            num_scalar_prefetch=2,
            grid=(bsz, n_tiles),
            in_specs=[
                pl.BlockSpec((1, tt, d), lambda b, j, lo, hi: (b, tile0 + j, 0)),
                pl.BlockSpec((1, tt, e), lambda b, j, lo, hi: (b, j, 0)),
                pl.BlockSpec((1, e, cap, d), ys_map),
                pl.BlockSpec((1, 1, 6, d), lambda b, j, lo, hi: (b, is_ctx, 0, 0)),
                pl.BlockSpec((1, d), lambda b, j, lo, hi: (0, 0)),
            ],
            out_specs=pl.BlockSpec((1, tt, d), lambda b, j, lo, hi: (b, out_tile0 + j, 0)),
        ),
        out_shape=jax.ShapeDtypeStruct((bsz, out_rows, d), F32),
        input_output_aliases={} if final else {2: 0},
        name="moe_combine_final" if final else "moe_combine",
        compiler_params=_cparams(("parallel", "arbitrary")),
    )(first, last, xc, post, ys, mod, fg)


def _moe(xc, mod, gn, router_t, layer, w1, w3, w2, nl, nct, seq, ctx_len, do_ctx, final_g):
    assert final_g is None or not do_ctx
    bsz, tc, d = xc.shape
    e = router_t.shape[0]
    h, aff = _router(xc, mod, gn, router_t, 0, nl)
    cap_l = CAPACITY_FACTOR * seq // e
    pos, post, gate = _select(aff, cap_l, 0)
    ys = _ffn(h, pos.reshape(bsz, e, 1, seq), gate.reshape(bsz, e, 1, seq), layer, w1, w3, w2, cap_l)
    xc = _combine(xc, post, ys, mod, 0, nl, False, final_g)
    if do_ctx:
        h_ctx, aff_ctx = _router(xc, mod, gn, router_t, nl, nct)
        cap_c = CAPACITY_FACTOR * ctx_len // e
        pos, post, gate = _select(aff_ctx, cap_c, cap_c)
        pos = jnp.transpose(pos, (1, 0, 2)).reshape(1, e, 1, bsz * ctx_len)
        gate = jnp.transpose(gate, (1, 0, 2)).reshape(1, e, 1, bsz * ctx_len)
        ys = _ffn(h_ctx.reshape(1, bsz * ctx_len, d), pos, gate, layer, w1, w3, w2, bsz * cap_c)
        xc = _combine(xc, post, ys, mod, nl, nct, True)
    return xc


def kernel(x, c, ctx, c_ctx, ada_w, ada_b, norm1_g, norm2_g, rwkv_mix, rwkv_wrkv, rwkv_w0, rwkv_w1, rwkv_w2, rwkv_a0, rwkv_a1, rwkv_a2, rwkv_v0, rwkv_v1, rwkv_v2, rwkv_g1, rwkv_g2, rwkv_kk, rwkv_ka, rwkv_rk, rwkv_lnw, rwkv_lnb, rwkv_wo, pool_w, pool_scale, moe_router, moe_w1, moe_w3, moe_w2, final_g):
    bsz, seq, d = x.shape
    ctx_len = ctx.shape[1]
    depth = ada_w.shape[0]
    n_heads, head = rwkv_rk.shape[1], rwkv_rk.shape[2]
    n_mixers = 2
    tt = TOKEN_TILE
    assert seq % tt == 0 and ctx_len % tt == 0 and tt % GRID_W == 0
    assert head == WKV_CHUNK and 2 * head == LANES and d % LANES == 0
    nl, nct = seq // tt, ctx_len // tt

    xc = jnp.concatenate([x, ctx], axis=1)

    rows = -(-(bsz + 1) // SUBLANES) * SUBLANES
    cond = jnp.zeros((rows, d), F32).at[:bsz].set(c).at[bsz].set(c_ctx)
    m_all = _ada_all(cond, ada_w, ada_b)
    m_lat = m_all[:, :bsz].reshape(depth, bsz, 1, 6, d)
    m_ctx = jnp.broadcast_to(m_all[:, bsz].reshape(depth, 1, 1, 6, d), (depth, bsz, 1, 6, d))
    mods = jnp.concatenate([m_lat, m_ctx], axis=2)

    head_of = jnp.arange(d) // head
    hs = (head_of[:, None] == jnp.arange(HEAD_COLS)[None, :]).astype(BF16)
    hst = jnp.transpose(hs)

    w1_all, w3_all, w2_all = moe_w1.astype(BF16), moe_w3.astype(BF16), moe_w2.astype(BF16)
    v_first = None
    for i in range(depth):
        last = i == depth - 1
        is_rwkv = i % n_mixers == 0
        jn = i // n_mixers
        mod = mods[i]
        gn1 = norm1_g[i].reshape(1, d)
        if is_rwkv:
            p = {
                'mix': rwkv_mix[jn],
                'wr': rwkv_wrkv[jn, 0].astype(BF16), 'wk': rwkv_wrkv[jn, 1].astype(BF16),
                'wv': rwkv_wrkv[jn, 2].astype(BF16),
                'g1': rwkv_g1[jn].astype(BF16), 'g2': rwkv_g2[jn].astype(BF16),
                'w1': _lora_in(rwkv_w1[jn]), 'w2': _lora_out(rwkv_w2[jn]), 'w0': rwkv_w0[jn],
                'a1': _lora_in(rwkv_a1[jn]), 'a2': _lora_out(rwkv_a2[jn]), 'a0': rwkv_a0[jn],
                'k_k': rwkv_kk[jn].reshape(1, d), 'k_a': rwkv_ka[jn].reshape(1, d),
                'r_k': rwkv_rk[jn].reshape(1, d), 'ln_w': rwkv_lnw[jn].reshape(1, d),
                'ln_b': rwkv_lnb[jn].reshape(1, d), 'wo': rwkv_wo[jn].astype(BF16),
                'hs': hs, 'hst': hst,
            }
            vres = None if jn == 0 else (rwkv_v0[jn - 1].reshape(1, d), rwkv_v1[jn - 1].astype(BF16),
                                         rwkv_v2[jn - 1].astype(BF16))
            r, v, g, kk, lw0, lw1, b0, b1, kd0, kd1, bsum = _rwkv_proj(xc, mod, gn1, p, vres, v_first,
                                                                       nl, seq, ctx_len)
            if v_first is None:
                v_first = v
            yf, yr = _wkv(r, v, kk, lw0, b0, kd0, lw1, b1, kd1, seq, ctx_len)
            xc = _rwkv_out(xc, yf, yr, v, g, bsum, mod, p, nl, head)
        else:
            n_tiles = nl if last else nl + nct
            xc = _pool(xc, mod, gn1, pool_w[jn].astype(BF16), pool_scale[jn].reshape(1, d),
                       nl, seq, ctx_len, n_tiles)
        xc = _moe(xc, mod, norm2_g[i].reshape(1, d), jnp.transpose(moe_router[i]),
                  i, w1_all, w3_all, w2_all, nl, nct, seq, ctx_len, not last,
                  final_g.reshape(1, d) if last else None)
    return xc
```

```python
import functools
import math

import jax
import jax.numpy as jnp
from jax import lax
from jax.experimental import pallas as pl
from jax.experimental.pallas import tpu as pltpu

F32 = jnp.float32
BF16 = jnp.bfloat16
I32 = jnp.int32
HIGHEST = lax.Precision.HIGHEST

GRID_W = 64
POOL_WINDOWS = (2, 4, 8, 16)
CAPACITY_FACTOR = 2
RMS_EPS = 1e-6
GN_EPS = 64e-5
EXP_NEG_HALF = math.exp(-0.5)

LANES = 128
SUBLANES = 8
BF16_ROWS = 16
VMEM_LIMIT_BYTES = 56 * 1024 * 1024

KK_NORM_FLOOR = 1e-12

TOKEN_TILE = 256
ADA_COL_TILE = 512
FFN_CHUNK = 512
GATHER_TILE = 256
GATHER_FAST_ROWS = 56
COMBINE_PACK = 4
COMBINE_FAST_SLOTS = 64
WKV_CHUNK = 64
WKV_PAIRS_PER_STEP = 8
POOL_HALO = 8
HEAD_COLS = LANES


def _cparams(sem):
    return pltpu.CompilerParams(dimension_semantics=sem, vmem_limit_bytes=VMEM_LIMIT_BYTES)


def _dot(a, b):
    return jnp.dot(a, b, preferred_element_type=F32)


def _dot_nt(a, b):
    return lax.dot_general(a, b, (((1,), (1,)), ((), ())), preferred_element_type=F32)


def _dot_tn(a, b):
    return lax.dot_general(a, b, (((0,), (0,)), ((), ())), preferred_element_type=F32)


def _split3(x):
    hi = x.astype(BF16)
    r1 = x - hi.astype(F32)
    mid = r1.astype(BF16)
    lo = (r1 - mid.astype(F32)).astype(BF16)
    return hi, mid, lo


def _dot_split(x, m, pieces):
    hi = x.astype(BF16)
    out = _dot(hi, m)
    for _ in range(pieces - 1):
        x = x - hi.astype(F32)
        hi = x.astype(BF16)
        out = out + _dot(hi, m)
    return out


def _dot3_left(m, x):
    w = x.shape[1]
    z = _dot(m, jnp.concatenate(_split3(x), axis=1))
    return z[:, 0:w] + z[:, w:2 * w] + z[:, 2 * w:3 * w]


def _sigmoid(x):
    return 0.5 * jnp.tanh(0.5 * x) + 0.5


def _norm_mod(x, g, shift, scale):
    ms = jnp.mean(x * x, axis=-1, keepdims=True)
    return x * lax.rsqrt(ms + RMS_EPS) * (g * (1.0 + scale)) + shift


def _ada_kernel(c_ref, w_ref, b_ref, o_ref):
    c = c_ref[...]
    s = c * _sigmoid(c)
    o_ref[0] = jnp.dot(s, w_ref[0], precision=HIGHEST, preferred_element_type=F32) + b_ref[0]


def _ada_all(cond, ada_w, ada_b):
    depth, d, n6 = ada_w.shape
    rows = cond.shape[0]
    nt = min(n6, ADA_COL_TILE)
    return pl.pallas_call(
        _ada_kernel,
        grid=(depth, n6 // nt),
        in_specs=[
            pl.BlockSpec((rows, d), lambda i, n: (0, 0)),
            pl.BlockSpec((1, d, nt), lambda i, n: (i, 0, n)),
            pl.BlockSpec((1, 1, nt), lambda i, n: (i, 0, n)),
        ],
        out_specs=pl.BlockSpec((1, rows, nt), lambda i, n: (i, 0, n)),
        out_shape=jax.ShapeDtypeStruct((depth, rows, n6), F32),
        name="ada_mod",
        compiler_params=_cparams(("arbitrary", "arbitrary")),
    )(cond, ada_w, ada_b.reshape(depth, 1, n6))


def _rwkv_proj_kernel(nl, seq, ctx_len, has_vres, *refs):
    if has_vres:
        (x_ref, xp_ref, xn_ref, mod_ref, gn_ref, mix_ref, wr_ref, wk_ref, wv_ref, g1_ref, g2_ref,
         w1_ref, w2_ref, w0_ref, a1_ref, a2_ref, a0_ref, kkw_ref, kaw_ref, rk_ref, hs_ref, hst_ref,
         v0_ref, v1_ref, v2_ref, vf_ref,
         r_o, v_o, g_o, kk_o, lw0_o, lw1_o, b0_o, b1_o, kd0_o, kd1_o, bs_o, ext_scr, sh_scr) = refs
    else:
        (x_ref, xp_ref, xn_ref, mod_ref, gn_ref, mix_ref, wr_ref, wk_ref, wv_ref, g1_ref, g2_ref,
         w1_ref, w2_ref, w0_ref, a1_ref, a2_ref, a0_ref, kkw_ref, kaw_ref, rk_ref, hs_ref, hst_ref,
         r_o, v_o, g_o, kk_o, lw0_o, lw1_o, b0_o, b1_o, kd0_o, kd1_o, bs_o, ext_scr, sh_scr) = refs
    tt = x_ref.shape[1]
    d = x_ref.shape[2]
    hw = GRID_W
    j = pl.program_id(1)
    shift = mod_ref[0, 0, 0:1, :]
    scale = mod_ref[0, 0, 1:2, :]
    gn = gn_ref[...]
    ext_scr[0:hw, :] = _norm_mod(xp_ref[0], gn, shift, scale)
    ext_scr[hw:hw + tt, :] = _norm_mod(x_ref[0], gn, shift, scale)
    ext_scr[hw + tt:, :] = _norm_mod(xn_ref[0], gn, shift, scale)

    i = lax.broadcasted_iota(I32, (tt, 1), 0)
    q = d // 4

    @pl.when(j < nl)
    def _latent_shift():
        t = j * tt + i
        col = i % hw
        sh_scr[:, 0:q] = jnp.where(col != 0, ext_scr[hw - 1:hw - 1 + tt, 0:q], 0.0)
        sh_scr[:, q:2 * q] = jnp.where(col != hw - 1, ext_scr[hw + 1:hw + 1 + tt, q:2 * q], 0.0)
        sh_scr[:, 2 * q:3 * q] = jnp.where(t >= hw, ext_scr[0:tt, 2 * q:3 * q], 0.0)
        sh_scr[:, 3 * q:] = jnp.where(t < seq - hw, ext_scr[2 * hw:2 * hw + tt, 3 * q:], 0.0)

    @pl.when(j >= nl)
    def _context_shift():
        t = (j - nl) * tt + i
        hd = d // 2
        sh_scr[:, 0:hd] = jnp.where(t != 0, ext_scr[hw - 1:hw - 1 + tt, 0:hd], 0.0)
        sh_scr[:, hd:] = jnp.where(t != ctx_len - 1, ext_scr[hw + 1:hw + 1 + tt, hd:], 0.0)

    h = ext_scr[hw:hw + tt, :]
    xx = sh_scr[...] - h

    def mixed(n):
        return (h + xx * mix_ref[n:n + 1, :]).astype(BF16)

    xr, xw, xk, xv, xa, xg = [mixed(n) for n in range(6)]
    r = _dot(xr, wr_ref[...])
    k = _dot(xk, wk_ref[...])
    v = _dot(xv, wv_ref[...])
    if has_vres:
        lor = _dot(_dot(xv, v1_ref[...]).astype(BF16), v2_ref[...])
        v = v + (vf_ref[0] - v) * _sigmoid(v0_ref[...] + lor)
    g = _dot(_sigmoid(_dot(xg, g1_ref[...])).astype(BF16), g2_ref[...])
    r_o[0] = r
    v_o[0] = v
    g_o[0] = g

    kkr = k * kkw_ref[...]
    ss = _dot_split(kkr * kkr, hs_ref[...], 1)
    inv = 1.0 / jnp.maximum(jnp.sqrt(ss), KK_NORM_FLOOR)
    kk = kkr * _dot_split(inv, hst_ref[...], 2)
    kk_o[0] = kk
    kaw = kaw_ref[...]
    kb = None
    tw = jnp.tanh(_dot(xw, w1_ref[...])).astype(BF16)
    ta = _dot(xa, a1_ref[...]).astype(BF16)
    for dr, (lw_o, b_o, kd_o) in enumerate(((lw0_o, b0_o, kd0_o), (lw1_o, b1_o, kd1_o))):
        wpre = w0_ref[dr:dr + 1, :] + _dot(tw, w2_ref[dr])
        lw_o[0] = -EXP_NEG_HALF * _sigmoid(wpre)
        a = _sigmoid(a0_ref[dr:dr + 1, :] + _dot(ta, a2_ref[dr]))
        b_o[0] = kk * a
        kd = k * (1.0 + (a - 1.0) * kaw)
        kd_o[0] = kd
        kb = 0.5 * kd if kb is None else kb + 0.5 * kd
    bs_o[0] = _dot_split(r * kb * rk_ref[...], hs_ref[...], 1)


def _lora_in(w):
    return jnp.concatenate([w[i] for i in range(w.shape[0])], axis=1).astype(BF16)


def _lora_out(w):
    n, r, _ = w.shape
    rows = jnp.arange(n * r) // r
    return jnp.where((rows[None, :] == jnp.arange(n)[:, None])[:, :, None],
                     jnp.tile(w, (1, n, 1)), 0.0).astype(BF16)


def _rwkv_proj(xc, mod, gn, p, vres, v_first, nl, seq, ctx_len):
    bsz, tc, d = xc.shape
    tt = TOKEN_TILE
    nt = tc // tt
    hb = tt // GRID_W
    nhb = tc // GRID_W
    has_vres = vres is not None

    def full(a):
        nd = a.ndim
        return pl.BlockSpec(a.shape, lambda b, j, _n=nd: (0,) * _n)

    tile = pl.BlockSpec((1, tt, d), lambda b, j: (b, j, 0))
    ins = [xc, xc, xc, mod, gn, p['mix'], p['wr'], p['wk'], p['wv'], p['g1'], p['g2'],
           p['w1'], p['w2'], p['w0'], p['a1'], p['a2'], p['a0'], p['k_k'], p['k_a'], p['r_k'], p['hs'], p['hst']]
    specs = [
        tile,
        pl.BlockSpec((1, GRID_W, d), lambda b, j: (b, jnp.maximum(j * hb - 1, 0), 0)),
        pl.BlockSpec((1, GRID_W, d), lambda b, j: (b, jnp.minimum((j + 1) * hb, nhb - 1), 0)),
        pl.BlockSpec((1, 1, 6, d), lambda b, j: (b, (j >= nl).astype(I32), 0, 0)),
    ] + [full(a) for a in ins[4:]]
    if has_vres:
        ins += [vres[0], vres[1], vres[2], v_first]
        specs += [full(vres[0]), full(vres[1]), full(vres[2]), tile]
    out_sds = jax.ShapeDtypeStruct((bsz, tc, d), F32)
    return pl.pallas_call(
        functools.partial(_rwkv_proj_kernel, nl, seq, ctx_len, has_vres),
        grid=(bsz, nt),
        in_specs=specs,
        out_specs=[tile] * 10 + [pl.BlockSpec((1, tt, HEAD_COLS), lambda b, j: (b, j, 0))],
        out_shape=[out_sds] * 10 + [jax.ShapeDtypeStruct((bsz, tc, HEAD_COLS), F32)],
        scratch_shapes=[pltpu.VMEM((tt + 2 * GRID_W, d), F32), pltpu.VMEM((tt, d), F32)],
        name="rwkv_proj",
        compiler_params=_cparams(("parallel", "arbitrary")),
    )(*ins)


def _wkv_chains(chains, masks):
    nc = len(chains)
    ks = range(nc)
    L = chains[0][0].shape[0]
    r, v, kk, lw, b, kd, s_prev, rev = [[ch[i] for ch in chains] for i in range(8)]
    mk = [masks[1] if rv else masks[0] for rv in rev]
    tri, m0, m1, strict, incl, eye2, bd = [[m[i] for m in mk] for i in range(7)]

    def stack(k, x):
        xb = x.astype(BF16)
        return jnp.concatenate([xb * m0[k], xb * m1[k]], axis=0)

    c = [_dot3_left(tri[k], lw[k]) for k in ks]
    ctot = [c[k][0:1, :] if rev[k] else c[k][L - 1:L, :] for k in ks]
    e_c = [jnp.exp(c[k]) for k in ks]
    e_nc = [jnp.exp(-c[k]) for k in ks]
    e_tc = [jnp.exp(ctot[k] - c[k]) for k in ks]
    ah = [-kk[k] * jnp.exp(c[k] - lw[k]) for k in ks]
    rh = [r[k] * e_c[k] for k in ks]
    lhs = [jnp.concatenate([ah[k], rh[k]], axis=0).astype(BF16) for k in ks]
    rhs = [jnp.concatenate([stack(k, b[k] * e_nc[k]), stack(k, kd[k] * e_nc[k])], axis=0) for k in ks]
    aa = [_dot_nt(lhs[k], rhs[k]) for k in ks]
    a_ab = [jnp.where(strict[k], aa[k][0:L, 0:2 * L], 0.0) for k in ks]
    a_ak = [jnp.where(strict[k], aa[k][0:L, 2 * L:4 * L], 0.0).astype(BF16) for k in ks]
    a_r = [jnp.where(jnp.concatenate([incl[k], incl[k]], axis=1), aa[k][L:2 * L, :], 0.0).astype(BF16)
           for k in ks]

    n_dbl = int(math.log2(L))
    tm = [eye2[k] + a_ab[k] for k in ks]
    pw = [_dot(a_ab[k].astype(BF16), stack(k, a_ab[k])) for k in ks]
    for _ in range(n_dbl - 2):
        z = [_dot(pw[k].astype(BF16), jnp.concatenate([stack(k, tm[k]), stack(k, pw[k])], axis=1)) for k in ks]
        tm = [tm[k] + z[k][:, 0:2 * L] for k in ks]
        pw = [z[k][:, 2 * L:4 * L] for k in ks]
    tm = [tm[k] + _dot(pw[k].astype(BF16), stack(k, tm[k])) for k in ks]

    ss0 = [_dot_nt(lhs[k], s_prev[k].astype(BF16)) for k in ks]
    vs = [stack(k, v[k]) for k in ks]
    wmat = [ss0[k][0:L] + _dot(a_ak[k], vs[k]) for k in ks]
    u = [_dot(tm[k].astype(BF16), stack(k, wmat[k])) for k in ks]
    y = [ss0[k][L:2 * L] + _dot(a_r[k], jnp.concatenate([stack(k, u[k]), vs[k]], axis=0)) for k in ks]
    uv = [jnp.concatenate([u[k], v[k]], axis=0).astype(BF16) for k in ks]
    bk = [jnp.concatenate([b[k] * e_tc[k], kd[k] * e_tc[k]], axis=0).astype(BF16) for k in ks]
    upd = [_dot_tn(uv[k], bk[k]) for k in ks]
    s_new = [s_prev[k] * jnp.exp(ctot[k]) + jnp.where(bd[k], upd[k], 0.0) for k in ks]
    return y, s_new


def _wkv_masks(L, w2, reverse):
    n = w2 // 2
    row = lax.broadcasted_iota(I32, (L, L), 0)
    colm = lax.broadcasted_iota(I32, (L, L), 1)
    tri = jnp.where((colm >= row) if reverse else (colm <= row), 1.0, 0.0).astype(BF16)
    lane = lax.broadcasted_iota(I32, (1, w2), 1)
    m0 = jnp.where(lane < n, 1.0, 0.0).astype(BF16)
    m1 = jnp.where(lane < n, 0.0, 1.0).astype(BF16)
    t_i = lax.broadcasted_iota(I32, (L, 2 * L), 0)
    s_i = lax.broadcasted_iota(I32, (L, 2 * L), 1) % L
    strict = (s_i > t_i) if reverse else (s_i < t_i)
    incl = (s_i >= t_i) if reverse else (s_i <= t_i)
    eye2 = jnp.where(s_i == t_i, 1.0, 0.0)
    ri = lax.broadcasted_iota(I32, (w2, w2), 0)
    ci = lax.broadcasted_iota(I32, (w2, w2), 1)
    bd = (ri < n) == (ci < n)
    return tri, m0, m1, strict, incl, eye2, bd


def _wkv_kernel(rf, vf, kkf, lwf, bf, kdf, rr, vr, kkr, lwr, br, kdr, yf_o, yr_o, s_scr):
    @pl.when(pl.program_id(2) == 0)
    def _init():
        s_scr[...] = jnp.zeros_like(s_scr)

    L = rf.shape[1]
    masks = (_wkv_masks(L, LANES, False), _wkv_masks(L, LANES, True))
    chains = []
    for hp in range(rf.shape[2] // LANES):
        ls = slice(hp * LANES, (hp + 1) * LANES)
        chains.append((rf[0, :, ls], vf[0, :, ls], kkf[0, :, ls], lwf[0, :, ls], bf[0, :, ls],
                       kdf[0, :, ls], s_scr[0, hp], False))
        chains.append((rr[0, :, ls], vr[0, :, ls], kkr[0, :, ls], lwr[0, :, ls], br[0, :, ls],
                       kdr[0, :, ls], s_scr[1, hp], True))
    ys, ss = _wkv_chains(chains, masks)
    for hp in range(rf.shape[2] // LANES):
        ls = slice(hp * LANES, (hp + 1) * LANES)
        yf_o[0, :, ls] = ys[2 * hp]
        yr_o[0, :, ls] = ys[2 * hp + 1]
        s_scr[0, hp] = ss[2 * hp]
        s_scr[1, hp] = ss[2 * hp + 1]


def _wkv(r, v, kk, lw0, b0, kd0, lw1, b1, kd1, seq, ctx_len):
    bsz, tc, d = r.shape
    L = WKV_CHUNK
    nlc = seq // L
    ncc = ctx_len // L
    nch = nlc + ncc
    gp = min(WKV_PAIRS_PER_STEP, d // LANES)
    width = gp * LANES

    def fwd_map(b, h, c):
        return (b, jnp.where(c < ncc, nlc + c, c - ncc), h)

    def rev_map(b, h, c):
        return (b, jnp.where(c < ncc, nlc + ncc - 1 - c, nlc - 1 - (c - ncc)), h)

    fs = pl.BlockSpec((1, L, width), fwd_map)
    rs = pl.BlockSpec((1, L, width), rev_map)
    sds = jax.ShapeDtypeStruct((bsz, tc, d), F32)
    return pl.pallas_call(
        _wkv_kernel,
        grid=(bsz, d // width, nch),
        in_specs=[fs] * 6 + [rs] * 6,
        out_specs=[fs, rs],
        out_shape=[sds, sds],
        scratch_shapes=[pltpu.VMEM((2, gp, LANES, LANES), F32)],
        name="wkv_scan",
        compiler_params=_cparams(("parallel", "parallel", "arbitrary")),
    )(r, v, kk, lw0, b0, kd0, r, v, kk, lw1, b1, kd1)


def _rwkv_out_kernel(head, x_ref, yf_ref, yr_ref, v_ref, g_ref, bs_ref, mod_ref,
                     lnw_ref, lnb_ref, wo_ref, hs_ref, hst_ref, o_ref):
    hs = hs_ref[...]
    hst = hst_ref[...]
    o = yf_ref[0] + yr_ref[0]
    inv_n = 1.0 / head
    mu = _dot_split(_dot_split(o, hs, 2) * inv_n, hst, 2)
    dlt = o - mu
    var = _dot_split(dlt * dlt, hs, 1) * inv_n
    on = dlt * _dot_split(lax.rsqrt(var + GN_EPS), hst, 2)
    on = on * lnw_ref[...] + lnb_ref[...]
    bonus = _dot_split(bs_ref[0], hst, 2) * v_ref[0]
    y = ((on + bonus) * g_ref[0]).astype(BF16)
    gate = mod_ref[0, 0, 2:3, :]
    o_ref[0] = x_ref[0] + gate * _dot(y, wo_ref[...])


def _rwkv_out(xc, yf, yr, v, g, bsum, mod, p, nl, head):
    bsz, tc, d = xc.shape
    tt = TOKEN_TILE
    tile = pl.BlockSpec((1, tt, d), lambda b, j: (b, j, 0))

    def full(a):
        nd = a.ndim
        return pl.BlockSpec(a.shape, lambda b, j, _n=nd: (0,) * _n)

    consts = [p['ln_w'], p['ln_b'], p['wo'], p['hs'], p['hst']]
    return pl.pallas_call(
        functools.partial(_rwkv_out_kernel, head),
        grid=(bsz, tc // tt),
        in_specs=[tile] * 5 + [pl.BlockSpec((1, tt, HEAD_COLS), lambda b, j: (b, j, 0)),
                               pl.BlockSpec((1, 1, 6, d), lambda b, j: (b, (j >= nl).astype(I32), 0, 0))]
        + [full(a) for a in consts],
        out_specs=tile,
        out_shape=jax.ShapeDtypeStruct((bsz, tc, d), F32),
        input_output_aliases={0: 0},
        name="rwkv_out",
        compiler_params=_cparams(("parallel", "arbitrary")),
    )(xc, yf, yr, v, g, bsum, mod, *consts)


def _pool_kernel(nl, seq, ctx_len, x_ref, xp_ref, xn_ref, mod_ref, gn_ref, w_ref, sc_ref, o_ref, ext_scr):
    tt = x_ref.shape[1]
    d = x_ref.shape[2]
    ph = POOL_HALO
    j = pl.program_id(1)
    nct = ctx_len // tt
    shift = mod_ref[0, 0, 0:1, :]
    scale = mod_ref[0, 0, 1:2, :]
    gn = gn_ref[...]
    first = jnp.logical_or(j == 0, j == nl)
    last = jnp.logical_or(j == nl - 1, j == nl + nct - 1)
    hp = _norm_mod(xp_ref[0], gn, shift, scale)
    hn = _norm_mod(xn_ref[0], gn, shift, scale)
    ext_scr[0:ph, :] = jnp.where(first, 0.0, hp)
    ext_scr[ph:ph + tt, :] = _norm_mod(x_ref[0], gn, shift, scale)
    ext_scr[ph + tt:, :] = jnp.where(last, 0.0, hn)

    i = lax.broadcasted_iota(I32, (tt, 1), 0)
    t = jnp.where(j < nl, j * tt + i, (j - nl) * tt + i)
    tseg = jnp.where(j < nl, seq, ctx_len)
    ng = len(POOL_WINDOWS)
    dg = d // ng
    gate = mod_ref[0, 0, 2:3, :]
    for gi, win in enumerate(POOL_WINDOWS):
        half = win // 2
        cs = slice(gi * dg, (gi + 1) * dg)
        acc = ext_scr[ph - half:ph - half + tt, cs]
        for o in range(1, win):
            acc = acc + ext_scr[ph - half + o:ph - half + o + tt, cs]
        cnt = (jnp.minimum(t + half, tseg) - jnp.maximum(t - half, 0)).astype(F32)
        hg = ext_scr[ph:ph + tt, cs]
        dlt = (acc / cnt - hg).astype(BF16)
        y = _dot(dlt, w_ref[gi]) * sc_ref[:, cs]
        o_ref[0, :, cs] = x_ref[0, :, cs] + gate[:, cs] * y


def _pool(xc, mod, gn, w, sc, nl, seq, ctx_len, n_tiles):
    bsz, tc, d = xc.shape
    tt = TOKEN_TILE
    hb = tt // POOL_HALO
    nhb = tc // POOL_HALO
    tile = pl.BlockSpec((1, tt, d), lambda b, j: (b, j, 0))
    return pl.pallas_call(
        functools.partial(_pool_kernel, nl, seq, ctx_len),
        grid=(bsz, n_tiles),
        in_specs=[
            tile,
            pl.BlockSpec((1, POOL_HALO, d), lambda b, j: (b, jnp.maximum(j * hb - 1, 0), 0)),
            pl.BlockSpec((1, POOL_HALO, d), lambda b, j: (b, jnp.minimum((j + 1) * hb, nhb - 1), 0)),
            pl.BlockSpec((1, 1, 6, d), lambda b, j: (b, (j >= nl).astype(I32), 0, 0)),
            pl.BlockSpec(gn.shape, lambda b, j: (0, 0)),
            pl.BlockSpec(w.shape, lambda b, j: (0, 0, 0)),
            pl.BlockSpec(sc.shape, lambda b, j: (0, 0)),
        ],
        out_specs=tile,
        out_shape=jax.ShapeDtypeStruct((bsz, n_tiles * tt, d), F32),
        scratch_shapes=[pltpu.VMEM((tt + 2 * POOL_HALO, d), F32)],
        name="pool_mix",
        compiler_params=_cparams(("parallel", "arbitrary")),
    )(xc, xc, xc, mod, gn, w, sc)


def _router_kernel(x_ref, mod_ref, gn_ref, rth_ref, rtl_ref, h_o, aff_o):
    h = _norm_mod(x_ref[0], gn_ref[...], mod_ref[0, 0, 3:4, :], mod_ref[0, 0, 4:5, :])
    hb = h.astype(BF16)
    h_o[0] = hb
    hl = (h - hb.astype(F32)).astype(BF16)
    logits = _dot_nt(rth_ref[...], hb) + (_dot_nt(rth_ref[...], hl) + _dot_nt(rtl_ref[...], hb))
    m = jnp.max(logits, axis=0, keepdims=True)
    ex = jnp.exp(logits - m)
    aff_o[0] = ex / jnp.sum(ex, axis=0, keepdims=True)


def _router(xc, mod, gn, router_t, tile0, n_tiles):
    bsz, tc, d = xc.shape
    tt = TOKEN_TILE
    e = router_t.shape[0]
    is_ctx = 1 if tile0 > 0 else 0
    rt_hi = router_t.astype(BF16)
    return pl.pallas_call(
        _router_kernel,
        grid=(bsz, n_tiles),
        in_specs=[
            pl.BlockSpec((1, tt, d), lambda b, j: (b, tile0 + j, 0)),
            pl.BlockSpec((1, 1, 6, d), lambda b, j: (b, is_ctx, 0, 0)),
            pl.BlockSpec(gn.shape, lambda b, j: (0, 0)),
            pl.BlockSpec(router_t.shape, lambda b, j: (0, 0)),
            pl.BlockSpec(router_t.shape, lambda b, j: (0, 0)),
        ],
        out_specs=[pl.BlockSpec((1, tt, d), lambda b, j: (b, j, 0)),
                   pl.BlockSpec((1, e, tt), lambda b, j: (b, 0, j))],
        out_shape=[jax.ShapeDtypeStruct((bsz, n_tiles * tt, d), BF16),
                   jax.ShapeDtypeStruct((bsz, e, n_tiles * tt), F32)],
        name="moe_router",
        compiler_params=_cparams(("parallel", "arbitrary")),
    )(xc, mod, gn, rt_hi, (router_t - rt_hi.astype(F32)).astype(BF16))


def _lane_cumsum_excl(x, utri):
    e, t = x.shape
    off = jnp.zeros((e, 1), F32)
    parts = []
    for kb in range(t // LANES):
        blk = x[:, kb * LANES:(kb + 1) * LANES]
        inc = _dot(blk.astype(BF16), utri)
        parts.append(inc - blk + off)
        off = off + inc[:, LANES - 1:LANES]
    return jnp.concatenate(parts, axis=1), off


def _select_kernel(cap, slot_stride, aff_ref, pos_o, post_o, gate_o):
    a = aff_ref[0]
    e, t = a.shape
    bits = pltpu.bitcast(a, I32)

    def body(it, thr):
        cand = thr | lax.shift_left(jnp.int32(1), jnp.int32(29) - it)
        cnt = jnp.sum(jnp.where(bits >= cand, 1.0, 0.0), axis=1, keepdims=True)
        return jnp.where(cnt >= cap, cand, thr)

    thr = lax.fori_loop(0, 30, body, jnp.zeros((e, 1), I32))
    gt = jnp.where(bits > thr, 1.0, 0.0)
    eq = jnp.where(bits == thr, 1.0, 0.0)
    ri = lax.broadcasted_iota(I32, (LANES, LANES), 0)
    ci = lax.broadcasted_iota(I32, (LANES, LANES), 1)
    utri = jnp.where(ri <= ci, 1.0, 0.0).astype(BF16)
    n_gt = jnp.sum(gt, axis=1, keepdims=True)
    eq_rank, _ = _lane_cumsum_excl(eq, utri)
    sel = gt + eq * jnp.where(eq_rank < cap - n_gt, 1.0, 0.0)
    pos, _ = _lane_cumsum_excl(sel, utri)
    base = pl.program_id(0) * slot_stride
    posm = jnp.where(sel > 0.5, pos.astype(I32) + base, -1)
    pos_o[0] = posm
    post_o[0] = jnp.transpose(posm.astype(F32)).astype(I32)
    gate_o[0] = jnp.where(sel > 0.5, a, 0.0)


def _select(aff, cap, slot_stride):
    bsz, e, t = aff.shape
    blk = pl.BlockSpec((1, e, t), lambda b: (b, 0, 0))
    return pl.pallas_call(
        functools.partial(_select_kernel, cap, slot_stride),
        grid=(bsz,),
        in_specs=[blk],
        out_specs=[blk, pl.BlockSpec((1, t, e), lambda b: (b, 0, 0)), blk],
        out_shape=[jax.ShapeDtypeStruct((bsz, e, t), I32),
                   jax.ShapeDtypeStruct((bsz, t, e), I32),
                   jax.ShapeDtypeStruct((bsz, e, t), F32)],
        name="moe_select",
        compiler_params=_cparams(("parallel",)),
    )(aff)


def _ffn_kernel(cap, win, fwin, tk, fk, lo_ref, hi_ref, h_ref, pos_ref, gate_ref, w1_ref, w3_ref, w2_ref,
                ys_o, xin_scr, gsl_scr):
    t = h_ref.shape[1]
    f = w1_ref.shape[3]
    nkt = t // tk
    pos = pos_ref[0, 0]
    g_hi, g_mid, g_lo = _split3(gate_ref[0, 0])
    prow = lax.broadcasted_iota(I32, (SUBLANES, t), 0)
    g8 = jnp.where(prow == 0, g_hi.astype(F32),
                   jnp.where(prow == 1, g_mid.astype(F32),
                             jnp.where(prow == 2, g_lo.astype(F32), 0.0))).astype(BF16)
    xin_scr[...] = jnp.zeros_like(xin_scr)
    gsl_scr[...] = jnp.zeros_like(gsl_scr)
    base = (pl.program_id(1) * pl.num_programs(0) + pl.program_id(0)) * nkt

    def gather(kt, rows, enabled):
        ts = slice(kt * tk, (kt + 1) * tk)
        lo = lo_ref[base + kt]
        start = pl.multiple_of(jnp.minimum((lo // SUBLANES) * SUBLANES, cap - rows), SUBLANES)
        slot = lax.broadcasted_iota(I32, (rows, tk), 0)
        rel = jnp.where(enabled, pos[:, ts] - start, -1)
        onehot = jnp.where(rel == slot, 1.0, 0.0).astype(BF16)
        xin_scr[pl.ds(start, rows), :] += _dot(onehot, h_ref[0, ts, :])
        gsl_scr[pl.ds(start, rows), :] += _dot_nt(onehot, g8[:, ts])

    def fits_fast(kt):
        lo = lo_ref[base + kt]
        start = jnp.minimum((lo // SUBLANES) * SUBLANES, cap - fwin)
        return hi_ref[base + kt] <= start + fwin

    for kt in range(nkt):
        gather(kt, fwin, fits_fast(kt) if fwin < win else True)
    for kt in range(nkt if fwin < win else 0):
        @pl.when(jnp.logical_not(fits_fast(kt)))
        def _wide():
            gather(kt, win, True)
    gate_slot = jnp.sum(gsl_scr[...], axis=1, keepdims=True)
    xb = xin_scr[...].astype(BF16)
    y = jnp.zeros(xin_scr.shape, F32)
    n_chunks = f // fk

    def up(kf):
        fs = slice(kf * fk, (kf + 1) * fk)
        return _dot(xb, w1_ref[0, 0, :, fs]), _dot(xb, w3_ref[0, 0, :, fs])

    nxt = up(0)
    for kf in range(n_chunks):
        h1, h3 = nxt
        if kf + 1 < n_chunks:
            nxt = up(kf + 1)
        hid = (h1 * _sigmoid(h1) * h3).astype(BF16)
        y = y + _dot(hid, w2_ref[0, 0, kf * fk:(kf + 1) * fk, :])
    ys_o[0, 0] = (y * gate_slot).astype(BF16)


def _ffn(h, posm, gate, layer, w1, w3, w2, cap):
    gsz, t, d = h.shape
    _, e, _, f = w1.shape
    tk = GATHER_TILE
    fk = min(f, FFN_CHUNK)
    win = min(cap, tk + SUBLANES)
    fwin = min(win, GATHER_FAST_ROWS)
    assert t % tk == 0 and (cap - win) % SUBLANES == 0 and (cap - fwin) % SUBLANES == 0
    nkt = t // tk
    pr = posm.reshape(gsz, e, nkt, tk)
    first = jnp.min(jnp.where(pr >= 0, pr, cap), axis=3)
    first = jnp.where(first >= cap, 0, first).reshape(-1)
    last = (jnp.max(pr, axis=3) + 1).reshape(-1)
    return pl.pallas_call(
        functools.partial(_ffn_kernel, cap, win, fwin, tk, fk),
        grid_spec=pltpu.PrefetchScalarGridSpec(
            num_scalar_prefetch=2,
            grid=(e, gsz),
            in_specs=[
                pl.BlockSpec((1, t, d), lambda ei, gi, lo, hi: (gi, 0, 0)),
                pl.BlockSpec((1, 1, 1, t), lambda ei, gi, lo, hi: (gi, ei, 0, 0)),
                pl.BlockSpec((1, 1, 1, t), lambda ei, gi, lo, hi: (gi, ei, 0, 0)),
                pl.BlockSpec((1, 1, d, f), lambda ei, gi, lo, hi: (layer, ei, 0, 0)),
                pl.BlockSpec((1, 1, d, f), lambda ei, gi, lo, hi: (layer, ei, 0, 0)),
                pl.BlockSpec((1, 1, f, d), lambda ei, gi, lo, hi: (layer, ei, 0, 0)),
            ],
            out_specs=pl.BlockSpec((1, 1, cap, d), lambda ei, gi, lo, hi: (gi, ei, 0, 0)),
            scratch_shapes=[pltpu.VMEM((cap, d), F32), pltpu.VMEM((cap, SUBLANES), F32)],
        ),
        out_shape=jax.ShapeDtypeStruct((gsz, e, cap, d), BF16),
        name="moe_ffn",
        compiler_params=_cparams(("arbitrary", "arbitrary")),
    )(first, last, h, posm, gate, w1, w3, w2)


def _combine_kernel(wide, pack, fslots, final, lo_ref, hi_ref, x_ref, post_ref, ys_ref, mod_ref, fg_ref,
                    o_ref):
    tt = x_ref.shape[1]
    e = ys_ref.shape[1]
    cap = ys_ref.shape[2]
    base = (pl.program_id(0) * pl.num_programs(1) + pl.program_id(1)) * e
    gate = mod_ref[0, 0, 5:6, :]

    def finish():
        if final:
            xv = o_ref[0]
            ms = jnp.mean(xv * xv, axis=-1, keepdims=True)
            o_ref[0] = xv * lax.rsqrt(ms + RMS_EPS) * fg_ref[...]

    def window_start(ei, rows):
        lo = lo_ref[base + ei]
        return pl.multiple_of(jnp.minimum((lo // BF16_ROWS) * BF16_ROWS, cap - rows), BF16_ROWS)

    def wide_term(ei):
        start = window_start(ei, wide)
        slot = lax.broadcasted_iota(I32, (tt, wide), 1)
        onehot = jnp.where(post_ref[0, :, ei:ei + 1] - start == slot, 1.0, 0.0).astype(BF16)
        return _dot(onehot, ys_ref[0, ei, pl.ds(start, wide), :])

    if not pack:
        acc = wide_term(0)
        for ei in range(1, e):
            acc = acc + wide_term(ei)
        o_ref[0] = x_ref[0] + gate * acc
        finish()
        return

    def fits_fast(ei):
        return hi_ref[base + ei] <= window_start(ei, fslots) + fslots

    lane = lax.broadcasted_iota(I32, (tt, pack * fslots), 1)
    acc = jnp.zeros(x_ref.shape[1:], F32)
    for g in range(e // pack):
        tgt = None
        rows = []
        for q in range(pack):
            ei = g * pack + q
            start = window_start(ei, fslots)
            rel = jnp.where(fits_fast(ei), post_ref[0, :, ei:ei + 1] - start + q * fslots, -1)
            tgt = rel if tgt is None else jnp.where(lane >= q * fslots, rel, tgt)
            rows.append(ys_ref[0, ei, pl.ds(start, fslots), :])
        onehot = jnp.where(tgt == lane, 1.0, 0.0).astype(BF16)
        acc = acc + _dot(onehot, jnp.concatenate(rows, axis=0))
    o_ref[0] = x_ref[0] + gate * acc
    for ei in range(e):
        @pl.when(jnp.logical_not(fits_fast(ei)))
        def _wide():
            o_ref[0] += gate * wide_term(ei)
    finish()


def _combine(xc, post, ys, mod, tile0, n_tiles, shared_slots, final_g=None):
    bsz, tc, d = xc.shape
    tt = TOKEN_TILE
    _, e, cap, _ = ys.shape
    is_ctx = 1 if tile0 > 0 else 0
    ys_map = (lambda b, j, lo, hi: (0, 0, 0, 0)) if shared_slots else (lambda b, j, lo, hi: (b, 0, 0, 0))
    wide = min(cap, tt + BF16_ROWS)
    fslots = COMBINE_FAST_SLOTS
    pack = COMBINE_PACK if (e % COMBINE_PACK == 0 and cap > wide) else 0
    assert (cap - wide) % BF16_ROWS == 0 and (not pack or (cap - fslots) % BF16_ROWS == 0)
    pr = post.reshape(bsz, n_tiles, tt, e)
    first = jnp.min(jnp.where(pr >= 0, pr, cap), axis=2)
    first = jnp.where(first >= cap, 0, first).reshape(-1)
    last = (jnp.max(pr, axis=2) + 1).reshape(-1)
    final = final_g is not None
    fg = final_g if final else jnp.ones((1, d), F32)
    out_tile0 = 0 if final else tile0
    out_rows = n_tiles * tt if final else tc
    return pl.pallas_call(
        functools.partial(_combine_kernel, wide, pack, fslots, final),
        grid_spec=pltpu.PrefetchScalarGridSpec(
            num_scalar_prefetch=2,
            grid=(bsz, n_tiles),
            in_specs=[
                pl.BlockSpec((1, tt, d), lambda b, j, lo, hi: (b, tile0 + j, 0)),
                pl.BlockSpec((1, tt, e), lambda b, j, lo, hi: (b, j, 0)),
                pl.BlockSpec((1, e, cap, d), ys_map),
                pl.BlockSpec((1, 1, 6, d), lambda b, j, lo, hi: (b, is_ctx, 0, 0)),
                pl.BlockSpec((1, d), lambda b, j, lo, hi: (0, 0)),
            ],
            out_specs=pl.BlockSpec((1, tt, d), lambda b, j, lo, hi: (b, out_tile0 + j, 0)),
        ),
        out_shape=jax.ShapeDtypeStruct((bsz, out_rows, d), F32),
        input_output_aliases={} if final else {2: 0},
        name="moe_combine_final" if final else "moe_combine",
        compiler_params=_cparams(("parallel", "arbitrary")),
    )(first, last, xc, post, ys, mod, fg)


def _moe(xc, mod, gn, router_t, layer, w1, w3, w2, nl, nct, seq, ctx_len, do_ctx, final_g):
    assert final_g is None or not do_ctx
    bsz, tc, d = xc.shape
    e = router_t.shape[0]
    h, aff = _router(xc, mod, gn, router_t, 0, nl)
    cap_l = CAPACITY_FACTOR * seq // e
    pos, post, gate = _select(aff, cap_l, 0)
    ys = _ffn(h, pos.reshape(bsz, e, 1, seq), gate.reshape(bsz, e, 1, seq), layer, w1, w3, w2, cap_l)
    xc = _combine(xc, post, ys, mod, 0, nl, False, final_g)
    if do_ctx:
        h_ctx, aff_ctx = _router(xc, mod, gn, router_t, nl, nct)
        cap_c = CAPACITY_FACTOR * ctx_len // e
        pos, post, gate = _select(aff_ctx, cap_c, cap_c)
        pos = jnp.transpose(pos, (1, 0, 2)).reshape(1, e, 1, bsz * ctx_len)
        gate = jnp.transpose(gate, (1, 0, 2)).reshape(1, e, 1, bsz * ctx_len)
        ys = _ffn(h_ctx.reshape(1, bsz * ctx_len, d), pos, gate, layer, w1, w3, w2, bsz * cap_c)
        xc = _combine(xc, post, ys, mod, nl, nct, True)
    return xc


def kernel(x, c, ctx, c_ctx, ada_w, ada_b, norm1_g, norm2_g, rwkv_mix, rwkv_wrkv, rwkv_w0, rwkv_w1, rwkv_w2, rwkv_a0, rwkv_a1, rwkv_a2, rwkv_v0, rwkv_v1, rwkv_v2, rwkv_g1, rwkv_g2, rwkv_kk, rwkv_ka, rwkv_rk, rwkv_lnw, rwkv_lnb, rwkv_wo, pool_w, pool_scale, moe_router, moe_w1, moe_w3, moe_w2, final_g):
    bsz, seq, d = x.shape
    ctx_len = ctx.shape[1]
    depth = ada_w.shape[0]
    n_heads, head = rwkv_rk.shape[1], rwkv_rk.shape[2]
    n_mixers = 2
    tt = TOKEN_TILE
    assert seq % tt == 0 and ctx_len % tt == 0 and tt % GRID_W == 0
    assert head == WKV_CHUNK and 2 * head == LANES and d % LANES == 0
    nl, nct = seq // tt, ctx_len // tt

    xc = jnp.concatenate([x, ctx], axis=1)

    rows = -(-(bsz + 1) // SUBLANES) * SUBLANES
    cond = jnp.zeros((rows, d), F32).at[:bsz].set(c).at[bsz].set(c_ctx)
    m_all = _ada_all(cond, ada_w, ada_b)
    m_lat = m_all[:, :bsz].reshape(depth, bsz, 1, 6, d)
    m_ctx = jnp.broadcast_to(m_all[:, bsz].reshape(depth, 1, 1, 6, d), (depth, bsz, 1, 6, d))
    mods = jnp.concatenate([m_lat, m_ctx], axis=2)

    head_of = jnp.arange(d) // head
    hs = (head_of[:, None] == jnp.arange(HEAD_COLS)[None, :]).astype(BF16)
    hst = jnp.transpose(hs)

    w1_all, w3_all, w2_all = moe_w1.astype(BF16), moe_w3.astype(BF16), moe_w2.astype(BF16)
    v_first = None
    for i in range(depth):
        last = i == depth - 1
        is_rwkv = i % n_mixers == 0
        jn = i // n_mixers
        mod = mods[i]
        gn1 = norm1_g[i].reshape(1, d)
        if is_rwkv:
            p = {
                'mix': rwkv_mix[jn],
                'wr': rwkv_wrkv[jn, 0].astype(BF16), 'wk': rwkv_wrkv[jn, 1].astype(BF16),
                'wv': rwkv_wrkv[jn, 2].astype(BF16),
                'g1': rwkv_g1[jn].astype(BF16), 'g2': rwkv_g2[jn].astype(BF16),
                'w1': _lora_in(rwkv_w1[jn]), 'w2': _lora_out(rwkv_w2[jn]), 'w0': rwkv_w0[jn],
                'a1': _lora_in(rwkv_a1[jn]), 'a2': _lora_out(rwkv_a2[jn]), 'a0': rwkv_a0[jn],
                'k_k': rwkv_kk[jn].reshape(1, d), 'k_a': rwkv_ka[jn].reshape(1, d),
                'r_k': rwkv_rk[jn].reshape(1, d), 'ln_w': rwkv_lnw[jn].reshape(1, d),
                'ln_b': rwkv_lnb[jn].reshape(1, d), 'wo': rwkv_wo[jn].astype(BF16),
                'hs': hs, 'hst': hst,
            }
            vres = None if jn == 0 else (rwkv_v0[jn - 1].reshape(1, d), rwkv_v1[jn - 1].astype(BF16),
                                         rwkv_v2[jn - 1].astype(BF16))
            r, v, g, kk, lw0, lw1, b0, b1, kd0, kd1, bsum = _rwkv_proj(xc, mod, gn1, p, vres, v_first,
                                                                       nl, seq, ctx_len)
            if v_first is None:
                v_first = v
            yf, yr = _wkv(r, v, kk, lw0, b0, kd0, lw1, b1, kd1, seq, ctx_len)
            xc = _rwkv_out(xc, yf, yr, v, g, bsum, mod, p, nl, head)
        else:
            n_tiles = nl if last else nl + nct
            xc = _pool(xc, mod, gn1, pool_w[jn].astype(BF16), pool_scale[jn].reshape(1, d),
                       nl, seq, ctx_len, n_tiles)
        xc = _moe(xc, mod, norm2_g[i].reshape(1, d), jnp.transpose(moe_router[i]),
                  i, w1_all, w3_all, w2_all, nl, nct, seq, ctx_len, not last,
                  final_g.reshape(1, d) if last else None)
    return xc
```

```python
import functools
import math

import jax
import jax.numpy as jnp
from jax import lax
from jax.experimental import pallas as pl
from jax.experimental.pallas import tpu as pltpu

F32 = jnp.float32
BF16 = jnp.bfloat16
I32 = jnp.int32
HIGHEST = lax.Precision.HIGHEST

GRID_W = 64
POOL_WINDOWS = (2, 4, 8, 16)
CAPACITY_FACTOR = 2
RMS_EPS = 1e-6
GN_EPS = 64e-5
EXP_NEG_HALF = math.exp(-0.5)

LANES = 128
SUBLANES = 8
BF16_ROWS = 16
VMEM_LIMIT_BYTES = 56 * 1024 * 1024

KK_NORM_FLOOR = 1e-12

TOKEN_TILE = 256
ADA_COL_TILE = 512
FFN_CHUNK = 512
GATHER_TILE = 256
GATHER_FAST_ROWS = 56
COMBINE_PACK = 4
COMBINE_FAST_SLOTS = 64
WKV_CHUNK = 64
WKV_PAIRS_PER_STEP = 8
POOL_HALO = 8
HEAD_COLS = LANES


def _cparams(sem):
    return pltpu.CompilerParams(dimension_semantics=sem, vmem_limit_bytes=VMEM_LIMIT_BYTES)


def _dot(a, b):
    return jnp.dot(a, b, preferred_element_type=F32)


def _dot_nt(a, b):
    return lax.dot_general(a, b, (((1,), (1,)), ((), ())), preferred_element_type=F32)


def _dot_tn(a, b):
    return lax.dot_general(a, b, (((0,), (0,)), ((), ())), preferred_element_type=F32)


def _split3(x):
    hi = x.astype(BF16)
    r1 = x - hi.astype(F32)
    mid = r1.astype(BF16)
    lo = (r1 - mid.astype(F32)).astype(BF16)
    return hi, mid, lo


def _dot_split(x, m, pieces):
    hi = x.astype(BF16)
    out = _dot(hi, m)
    for _ in range(pieces - 1):
        x = x - hi.astype(F32)
        hi = x.astype(BF16)
        out = out + _dot(hi, m)
    return out


def _dot3_left(m, x):
    w = x.shape[1]
    z = _dot(m, jnp.concatenate(_split3(x), axis=1))
    return z[:, 0:w] + z[:, w:2 * w] + z[:, 2 * w:3 * w]


def _sigmoid(x):
    return 0.5 * jnp.tanh(0.5 * x) + 0.5


def _norm_mod(x, g, shift, scale):
    ms = jnp.mean(x * x, axis=-1, keepdims=True)
    return x * lax.rsqrt(ms + RMS_EPS) * (g * (1.0 + scale)) + shift


def _ada_kernel(c_ref, w_ref, b_ref, o_ref):
    c = c_ref[...]
    s = c * _sigmoid(c)
    o_ref[0] = jnp.dot(s, w_ref[0], precision=HIGHEST, preferred_element_type=F32) + b_ref[0]


def _ada_all(cond, ada_w, ada_b):
    depth, d, n6 = ada_w.shape
    rows = cond.shape[0]
    nt = min(n6, ADA_COL_TILE)
    return pl.pallas_call(
        _ada_kernel,
        grid=(depth, n6 // nt),
        in_specs=[
            pl.BlockSpec((rows, d), lambda i, n: (0, 0)),
            pl.BlockSpec((1, d, nt), lambda i, n: (i, 0, n)),
            pl.BlockSpec((1, 1, nt), lambda i, n: (i, 0, n)),
        ],
        out_specs=pl.BlockSpec((1, rows, nt), lambda i, n: (i, 0, n)),
        out_shape=jax.ShapeDtypeStruct((depth, rows, n6), F32),
        name="ada_mod",
        compiler_params=_cparams(("arbitrary", "arbitrary")),
    )(cond, ada_w, ada_b.reshape(depth, 1, n6))


def _rwkv_proj_kernel(nl, seq, ctx_len, has_vres, *refs):
    if has_vres:
        (x_ref, xp_ref, xn_ref, mod_ref, gn_ref, mix_ref, wr_ref, wk_ref, wv_ref, g1_ref, g2_ref,
         w1_ref, w2_ref, w0_ref, a1_ref, a2_ref, a0_ref, kkw_ref, kaw_ref, rk_ref, hs_ref, hst_ref,
         v0_ref, v1_ref, v2_ref, vf_ref,
         r_o, v_o, g_o, kk_o, lw0_o, lw1_o, b0_o, b1_o, kd0_o, kd1_o, bs_o, ext_scr, sh_scr) = refs
    else:
        (x_ref, xp_ref, xn_ref, mod_ref, gn_ref, mix_ref, wr_ref, wk_ref, wv_ref, g1_ref, g2_ref,
         w1_ref, w2_ref, w0_ref, a1_ref, a2_ref, a0_ref, kkw_ref, kaw_ref, rk_ref, hs_ref, hst_ref,
         r_o, v_o, g_o, kk_o, lw0_o, lw1_o, b0_o, b1_o, kd0_o, kd1_o, bs_o, ext_scr, sh_scr) = refs
    tt = x_ref.shape[1]
    d = x_ref.shape[2]
    hw = GRID_W
    j = pl.program_id(1)
    shift = mod_ref[0, 0, 0:1, :]
    scale = mod_ref[0, 0, 1:2, :]
    gn = gn_ref[...]
    ext_scr[0:hw, :] = _norm_mod(xp_ref[0], gn, shift, scale)
    ext_scr[hw:hw + tt, :] = _norm_mod(x_ref[0], gn, shift, scale)
    ext_scr[hw + tt:, :] = _norm_mod(xn_ref[0], gn, shift, scale)

    i = lax.broadcasted_iota(I32, (tt, 1), 0)
    q = d // 4

    @pl.when(j < nl)
    def _latent_shift():
        t = j * tt + i
        col = i % hw
        sh_scr[:, 0:q] = jnp.where(col != 0, ext_scr[hw - 1:hw - 1 + tt, 0:q], 0.0)
        sh_scr[:, q:2 * q] = jnp.where(col != hw - 1, ext_scr[hw + 1:hw + 1 + tt, q:2 * q], 0.0)
        sh_scr[:, 2 * q:3 * q] = jnp.where(t >= hw, ext_scr[0:tt, 2 * q:3 * q], 0.0)
        sh_scr[:, 3 * q:] = jnp.where(t < seq - hw, ext_scr[2 * hw:2 * hw + tt, 3 * q:], 0.0)

    @pl.when(j >= nl)
    def _context_shift():
        t = (j - nl) * tt + i
        hd = d // 2
        sh_scr[:, 0:hd] = jnp.where(t != 0, ext_scr[hw - 1:hw - 1 + tt, 0:hd], 0.0)
        sh_scr[:, hd:] = jnp.where(t != ctx_len - 1, ext_scr[hw + 1:hw + 1 + tt, hd:], 0.0)

    h = ext_scr[hw:hw + tt, :]
    xx = sh_scr[...] - h

    def mixed(n):
        return (h + xx * mix_ref[n:n + 1, :]).astype(BF16)

    xr, xw, xk, xv, xa, xg = [mixed(n) for n in range(6)]
    r = _dot(xr, wr_ref[...])
    k = _dot(xk, wk_ref[...])
    v = _dot(xv, wv_ref[...])
    if has_vres:
        lor = _dot(_dot(xv, v1_ref[...]).astype(BF16), v2_ref[...])
        v = v + (vf_ref[0] - v) * _sigmoid(v0_ref[...] + lor)
    g = _dot(_sigmoid(_dot(xg, g1_ref[...])).astype(BF16), g2_ref[...])
    r_o[0] = r
    v_o[0] = v
    g_o[0] = g

    kkr = k * kkw_ref[...]
    ss = _dot_split(kkr * kkr, hs_ref[...], 1)
    inv = 1.0 / jnp.maximum(jnp.sqrt(ss), KK_NORM_FLOOR)
    kk = kkr * _dot_split(inv, hst_ref[...], 2)
    kk_o[0] = kk
    kaw = kaw_ref[...]
    kb = None
    tw = jnp.tanh(_dot(xw, w1_ref[...])).astype(BF16)
    ta = _dot(xa, a1_ref[...]).astype(BF16)
    for dr, (lw_o, b_o, kd_o) in enumerate(((lw0_o, b0_o, kd0_o), (lw1_o, b1_o, kd1_o))):
        wpre = w0_ref[dr:dr + 1, :] + _dot(tw, w2_ref[dr])
        lw_o[0] = -EXP_NEG_HALF * _sigmoid(wpre)
        a = _sigmoid(a0_ref[dr:dr + 1, :] + _dot(ta, a2_ref[dr]))
        b_o[0] = kk * a
        kd = k * (1.0 + (a - 1.0) * kaw)
        kd_o[0] = kd
        kb = 0.5 * kd if kb is None else kb + 0.5 * kd
    bs_o[0] = _dot_split(r * kb * rk_ref[...], hs_ref[...], 1)


def _lora_in(w):
    return jnp.concatenate([w[i] for i in range(w.shape[0])], axis=1).astype(BF16)


def _lora_out(w):
    n, r, _ = w.shape
    rows = jnp.arange(n * r) // r
    return jnp.where((rows[None, :] == jnp.arange(n)[:, None])[:, :, None],
                     jnp.tile(w, (1, n, 1)), 0.0).astype(BF16)


def _rwkv_proj(xc, mod, gn, p, vres, v_first, nl, seq, ctx_len):
    bsz, tc, d = xc.shape
    tt = TOKEN_TILE
    nt = tc // tt
    hb = tt // GRID_W
    nhb = tc // GRID_W
    has_vres = vres is not None

    def full(a):
        nd = a.ndim
        return pl.BlockSpec(a.shape, lambda b, j, _n=nd: (0,) * _n)

    tile = pl.BlockSpec((1, tt, d), lambda b, j: (b, j, 0))
    ins = [xc, xc, xc, mod, gn, p['mix'], p['wr'], p['wk'], p['wv'], p['g1'], p['g2'],
           p['w1'], p['w2'], p['w0'], p['a1'], p['a2'], p['a0'], p['k_k'], p['k_a'], p['r_k'], p['hs'], p['hst']]
    specs = [
        tile,
        pl.BlockSpec((1, GRID_W, d), lambda b, j: (b, jnp.maximum(j * hb - 1, 0), 0)),
        pl.BlockSpec((1, GRID_W, d), lambda b, j: (b, jnp.minimum((j + 1) * hb, nhb - 1), 0)),
        pl.BlockSpec((1, 1, 6, d), lambda b, j: (b, (j >= nl).astype(I32), 0, 0)),
    ] + [full(a) for a in ins[4:]]
    if has_vres:
        ins += [vres[0], vres[1], vres[2], v_first]
        specs += [full(vres[0]), full(vres[1]), full(vres[2]), tile]
    out_sds = jax.ShapeDtypeStruct((bsz, tc, d), F32)
    return pl.pallas_call(
        functools.partial(_rwkv_proj_kernel, nl, seq, ctx_len, has_vres),
        grid=(bsz, nt),
        in_specs=specs,
        out_specs=[tile] * 10 + [pl.BlockSpec((1, tt, HEAD_COLS), lambda b, j: (b, j, 0))],
        out_shape=[out_sds] * 10 + [jax.ShapeDtypeStruct((bsz, tc, HEAD_COLS), F32)],
        scratch_shapes=[pltpu.VMEM((tt + 2 * GRID_W, d), F32), pltpu.VMEM((tt, d), F32)],
        name="rwkv_proj",
        compiler_params=_cparams(("parallel", "arbitrary")),
    )(*ins)


def _wkv_chains(chains, masks):
    nc = len(chains)
    ks = range(nc)
    L = chains[0][0].shape[0]
    r, v, kk, lw, b, kd, s_prev, rev = [[ch[i] for ch in chains] for i in range(8)]
    mk = [masks[1] if rv else masks[0] for rv in rev]
    tri, m0, m1, strict, incl, eye2, bd = [[m[i] for m in mk] for i in range(7)]

    def stack(k, x):
        xb = x.astype(BF16)
        return jnp.concatenate([xb * m0[k], xb * m1[k]], axis=0)

    c = [_dot3_left(tri[k], lw[k]) for k in ks]
    ctot = [c[k][0:1, :] if rev[k] else c[k][L - 1:L, :] for k in ks]
    e_c = [jnp.exp(c[k]) for k in ks]
    e_nc = [jnp.exp(-c[k]) for k in ks]
    e_tc = [jnp.exp(ctot[k] - c[k]) for k in ks]
    ah = [-kk[k] * jnp.exp(c[k] - lw[k]) for k in ks]
    rh = [r[k] * e_c[k] for k in ks]
    lhs = [jnp.concatenate([ah[k], rh[k]], axis=0).astype(BF16) for k in ks]
    rhs = [jnp.concatenate([stack(k, b[k] * e_nc[k]), stack(k, kd[k] * e_nc[k])], axis=0) for k in ks]
    aa = [_dot_nt(lhs[k], rhs[k]) for k in ks]
    a_ab = [jnp.where(strict[k], aa[k][0:L, 0:2 * L], 0.0) for k in ks]
    a_ak = [jnp.where(strict[k], aa[k][0:L, 2 * L:4 * L], 0.0).astype(BF16) for k in ks]
    a_r = [jnp.where(jnp.concatenate([incl[k], incl[k]], axis=1), aa[k][L:2 * L, :], 0.0).astype(BF16)
           for k in ks]

    n_dbl = int(math.log2(L))
    tm = [eye2[k] + a_ab[k] for k in ks]
    pw = [_dot(a_ab[k].astype(BF16), stack(k, a_ab[k])) for k in ks]
    for _ in range(n_dbl - 2):
        z = [_dot(pw[k].astype(BF16), jnp.concatenate([stack(k, tm[k]), stack(k, pw[k])], axis=1)) for k in ks]
        tm = [tm[k] + z[k][:, 0:2 * L] for k in ks]
        pw = [z[k][:, 2 * L:4 * L] for k in ks]
    tm = [tm[k] + _dot(pw[k].astype(BF16), stack(k, tm[k])) for k in ks]

    ss0 = [_dot_nt(lhs[k], s_prev[k].astype(BF16)) for k in ks]
    vs = [stack(k, v[k]) for k in ks]
    wmat = [ss0[k][0:L] + _dot(a_ak[k], vs[k]) for k in ks]
    u = [_dot(tm[k].astype(BF16), stack(k, wmat[k])) for k in ks]
    y = [ss0[k][L:2 * L] + _dot(a_r[k], jnp.concatenate([stack(k, u[k]), vs[k]], axis=0)) for k in ks]
    uv = [jnp.concatenate([u[k], v[k]], axis=0).astype(BF16) for k in ks]
    bk = [jnp.concatenate([b[k] * e_tc[k], kd[k] * e_tc[k]], axis=0).astype(BF16) for k in ks]
    upd = [_dot_tn(uv[k], bk[k]) for k in ks]
    s_new = [s_prev[k] * jnp.exp(ctot[k]) + jnp.where(bd[k], upd[k], 0.0) for k in ks]
    return y, s_new


def _wkv_masks(L, w2, reverse):
    n = w2 // 2
    row = lax.broadcasted_iota(I32, (L, L), 0)
    colm = lax.broadcasted_iota(I32, (L, L), 1)
    tri = jnp.where((colm >= row) if reverse else (colm <= row), 1.0, 0.0).astype(BF16)
    lane = lax.broadcasted_iota(I32, (1, w2), 1)
    m0 = jnp.where(lane < n, 1.0, 0.0).astype(BF16)
    m1 = jnp.where(lane < n, 0.0, 1.0).astype(BF16)
    t_i = lax.broadcasted_iota(I32, (L, 2 * L), 0)
    s_i = lax.broadcasted_iota(I32, (L, 2 * L), 1) % L
    strict = (s_i > t_i) if reverse else (s_i < t_i)
    incl = (s_i >= t_i) if reverse else (s_i <= t_i)
    eye2 = jnp.where(s_i == t_i, 1.0, 0.0)
    ri = lax.broadcasted_iota(I32, (w2, w2), 0)
    ci = lax.broadcasted_iota(I32, (w2, w2), 1)
    bd = (ri < n) == (ci < n)
    return tri, m0, m1, strict, incl, eye2, bd


def _wkv_kernel(rf, vf, kkf, lwf, bf, kdf, rr, vr, kkr, lwr, br, kdr, yf_o, yr_o, s_scr):
    @pl.when(pl.program_id(2) == 0)
    def _init():
        s_scr[...] = jnp.zeros_like(s_scr)

    L = rf.shape[1]
    masks = (_wkv_masks(L, LANES, False), _wkv_masks(L, LANES, True))
    chains = []
    for hp in range(rf.shape[2] // LANES):
        ls = slice(hp * LANES, (hp + 1) * LANES)
        chains.append((rf[0, :, ls], vf[0, :, ls], kkf[0, :, ls], lwf[0, :, ls], bf[0, :, ls],
                       kdf[0, :, ls], s_scr[0, hp], False))
        chains.append((rr[0, :, ls], vr[0, :, ls], kkr[0, :, ls], lwr[0, :, ls], br[0, :, ls],
                       kdr[0, :, ls], s_scr[1, hp], True))
    ys, ss = _wkv_chains(chains, masks)
    for hp in range(rf.shape[2] // LANES):
        ls = slice(hp * LANES, (hp + 1) * LANES)
        yf_o[0, :, ls] = ys[2 * hp]
        yr_o[0, :, ls] = ys[2 * hp + 1]
        s_scr[0, hp] = ss[2 * hp]
        s_scr[1, hp] = ss[2 * hp + 1]


def _wkv(r, v, kk, lw0, b0, kd0, lw1, b1, kd1, seq, ctx_len):
    bsz, tc, d = r.shape
    L = WKV_CHUNK
    nlc = seq // L
    ncc = ctx_len // L
    nch = nlc + ncc
    gp = min(WKV_PAIRS_PER_STEP, d // LANES)
    width = gp * LANES

    def fwd_map(b, h, c):
        return (b, jnp.where(c < ncc, nlc + c, c - ncc), h)

    def rev_map(b, h, c):
        return (b, jnp.where(c < ncc, nlc + ncc - 1 - c, nlc - 1 - (c - ncc)), h)

    fs = pl.BlockSpec((1, L, width), fwd_map)
    rs = pl.BlockSpec((1, L, width), rev_map)
    sds = jax.ShapeDtypeStruct((bsz, tc, d), F32)
    return pl.pallas_call(
        _wkv_kernel,
        grid=(bsz, d // width, nch),
        in_specs=[fs] * 6 + [rs] * 6,
        out_specs=[fs, rs],
        out_shape=[sds, sds],
        scratch_shapes=[pltpu.VMEM((2, gp, LANES, LANES), F32)],
        name="wkv_scan",
        compiler_params=_cparams(("parallel", "parallel", "arbitrary")),
    )(r, v, kk, lw0, b0, kd0, r, v, kk, lw1, b1, kd1)


def _rwkv_out_kernel(head, x_ref, yf_ref, yr_ref, v_ref, g_ref, bs_ref, mod_ref,
                     lnw_ref, lnb_ref, wo_ref, hs_ref, hst_ref, o_ref):
    hs = hs_ref[...]
    hst = hst_ref[...]
    o = yf_ref[0] + yr_ref[0]
    inv_n = 1.0 / head
    mu = _dot_split(_dot_split(o, hs, 2) * inv_n, hst, 2)
    dlt = o - mu
    var = _dot_split(dlt * dlt, hs, 1) * inv_n
    on = dlt * _dot_split(lax.rsqrt(var + GN_EPS), hst, 2)
    on = on * lnw_ref[...] + lnb_ref[...]
    bonus = _dot_split(bs_ref[0], hst, 2) * v_ref[0]
    y = ((on + bonus) * g_ref[0]).astype(BF16)
    gate = mod_ref[0, 0, 2:3, :]
    o_ref[0] = x_ref[0] + gate * _dot(y, wo_ref[...])


def _rwkv_out(xc, yf, yr, v, g, bsum, mod, p, nl, head):
    bsz, tc, d = xc.shape
    tt = TOKEN_TILE
    tile = pl.BlockSpec((1, tt, d), lambda b, j: (b, j, 0))

    def full(a):
        nd = a.ndim
        return pl.BlockSpec(a.shape, lambda b, j, _n=nd: (0,) * _n)

    consts = [p['ln_w'], p['ln_b'], p['wo'], p['hs'], p['hst']]
    return pl.pallas_call(
        functools.partial(_rwkv_out_kernel, head),
        grid=(bsz, tc // tt),
        in_specs=[tile] * 5 + [pl.BlockSpec((1, tt, HEAD_COLS), lambda b, j: (b, j, 0)),
                               pl.BlockSpec((1, 1, 6, d), lambda b, j: (b, (j >= nl).astype(I32), 0, 0))]
        + [full(a) for a in consts],
        out_specs=tile,
        out_shape=jax.ShapeDtypeStruct((bsz, tc, d), F32),
        input_output_aliases={0: 0},
        name="rwkv_out",
        compiler_params=_cparams(("parallel", "arbitrary")),
    )(xc, yf, yr, v, g, bsum, mod, *consts)


def _pool_kernel(nl, seq, ctx_len, x_ref, xp_ref, xn_ref, mod_ref, gn_ref, w_ref, sc_ref, o_ref, ext_scr):
    tt = x_ref.shape[1]
    d = x_ref.shape[2]
    ph = POOL_HALO
    j = pl.program_id(1)
    nct = ctx_len // tt
    shift = mod_ref[0, 0, 0:1, :]
    scale = mod_ref[0, 0, 1:2, :]
    gn = gn_ref[...]
    first = jnp.logical_or(j == 0, j == nl)
    last = jnp.logical_or(j == nl - 1, j == nl + nct - 1)
    hp = _norm_mod(xp_ref[0], gn, shift, scale)
    hn = _norm_mod(xn_ref[0], gn, shift, scale)
    ext_scr[0:ph, :] = jnp.where(first, 0.0, hp)
    ext_scr[ph:ph + tt, :] = _norm_mod(x_ref[0], gn, shift, scale)
    ext_scr[ph + tt:, :] = jnp.where(last, 0.0, hn)

    i = lax.broadcasted_iota(I32, (tt, 1), 0)
    t = jnp.where(j < nl, j * tt + i, (j - nl) * tt + i)
    tseg = jnp.where(j < nl, seq, ctx_len)
    ng = len(POOL_WINDOWS)
    dg = d // ng
    gate = mod_ref[0, 0, 2:3, :]
    for gi, win in enumerate(POOL_WINDOWS):
        half = win // 2
        cs = slice(gi * dg, (gi + 1) * dg)
        acc = ext_scr[ph - half:ph - half + tt, cs]
        for o in range(1, win):
            acc = acc + ext_scr[ph - half + o:ph - half + o + tt, cs]
        cnt = (jnp.minimum(t + half, tseg) - jnp.maximum(t - half, 0)).astype(F32)
        hg = ext_scr[ph:ph + tt, cs]
        dlt = (acc / cnt - hg).astype(BF16)
        y = _dot(dlt, w_ref[gi]) * sc_ref[:, cs]
        o_ref[0, :, cs] = x_ref[0, :, cs] + gate[:, cs] * y


def _pool(xc, mod, gn, w, sc, nl, seq, ctx_len, n_tiles):
    bsz, tc, d = xc.shape
    tt = TOKEN_TILE
    hb = tt // POOL_HALO
    nhb = tc // POOL_HALO
    tile = pl.BlockSpec((1, tt, d), lambda b, j: (b, j, 0))
    return pl.pallas_call(
        functools.partial(_pool_kernel, nl, seq, ctx_len),
        grid=(bsz, n_tiles),
        in_specs=[
            tile,
            pl.BlockSpec((1, POOL_HALO, d), lambda b, j: (b, jnp.maximum(j * hb - 1, 0), 0)),
            pl.BlockSpec((1, POOL_HALO, d), lambda b, j: (b, jnp.minimum((j + 1) * hb, nhb - 1), 0)),
            pl.BlockSpec((1, 1, 6, d), lambda b, j: (b, (j >= nl).astype(I32), 0, 0)),
            pl.BlockSpec(gn.shape, lambda b, j: (0, 0)),
            pl.BlockSpec(w.shape, lambda b, j: (0, 0, 0)),
            pl.BlockSpec(sc.shape, lambda b, j: (0, 0)),
        ],
        out_specs=tile,
        out_shape=jax.ShapeDtypeStruct((bsz, n_tiles * tt, d), F32),
        scratch_shapes=[pltpu.VMEM((tt + 2 * POOL_HALO, d), F32)],
        name="pool_mix",
        compiler_params=_cparams(("parallel", "arbitrary")),
    )(xc, xc, xc, mod, gn, w, sc)


def _router_kernel(x_ref, mod_ref, gn_ref, rth_ref, rtl_ref, h_o, aff_o):
    h = _norm_mod(x_ref[0], gn_ref[...], mod_ref[0, 0, 3:4, :], mod_ref[0, 0, 4:5, :])
    hb = h.astype(BF16)
    h_o[0] = hb
    hl = (h - hb.astype(F32)).astype(BF16)
    logits = _dot_nt(rth_ref[...], hb) + (_dot_nt(rth_ref[...], hl) + _dot_nt(rtl_ref[...], hb))
    m = jnp.max(logits, axis=0, keepdims=True)
    ex = jnp.exp(logits - m)
    aff_o[0] = ex / jnp.sum(ex, axis=0, keepdims=True)


def _router(xc, mod, gn, router_t, tile0, n_tiles):
    bsz, tc, d = xc.shape
    tt = TOKEN_TILE
    e = router_t.shape[0]
    is_ctx = 1 if tile0 > 0 else 0
    rt_hi = router_t.astype(BF16)
    return pl.pallas_call(
        _router_kernel,
        grid=(bsz, n_tiles),
        in_specs=[
            pl.BlockSpec((1, tt, d), lambda b, j: (b, tile0 + j, 0)),
            pl.BlockSpec((1, 1, 6, d), lambda b, j: (b, is_ctx, 0, 0)),
            pl.BlockSpec(gn.shape, lambda b, j: (0, 0)),
            pl.BlockSpec(router_t.shape, lambda b, j: (0, 0)),
            pl.BlockSpec(router_t.shape, lambda b, j: (0, 0)),
        ],
        out_specs=[pl.BlockSpec((1, tt, d), lambda b, j: (b, j, 0)),
                   pl.BlockSpec((1, e, tt), lambda b, j: (b, 0, j))],
        out_shape=[jax.ShapeDtypeStruct((bsz, n_tiles * tt, d), BF16),
                   jax.ShapeDtypeStruct((bsz, e, n_tiles * tt), F32)],
        name="moe_router",
        compiler_params=_cparams(("parallel", "arbitrary")),
    )(xc, mod, gn, rt_hi, (router_t - rt_hi.astype(F32)).astype(BF16))


def _lane_cumsum_excl(x, utri):
    e, t = x.shape
    off = jnp.zeros((e, 1), F32)
    parts = []
    for kb in range(t // LANES):
        blk = x[:, kb * LANES:(kb + 1) * LANES]
        inc = _dot(blk.astype(BF16), utri)
        parts.append(inc - blk + off)
        off = off + inc[:, LANES - 1:LANES]
    return jnp.concatenate(parts, axis=1), off


def _select_kernel(cap, slot_stride, aff_ref, pos_o, post_o, gate_o):
    a = aff_ref[0]
    e, t = a.shape
    bits = pltpu.bitcast(a, I32)

    def body(it, thr):
        cand = thr | lax.shift_left(jnp.int32(1), jnp.int32(29) - it)
        cnt = jnp.sum(jnp.where(bits >= cand, 1.0, 0.0), axis=1, keepdims=True)
        return jnp.where(cnt >= cap, cand, thr)

    thr = lax.fori_loop(0, 30, body, jnp.zeros((e, 1), I32))
    gt = jnp.where(bits > thr, 1.0, 0.0)
    eq = jnp.where(bits == thr, 1.0, 0.0)
    ri = lax.broadcasted_iota(I32, (LANES, LANES), 0)
    ci = lax.broadcasted_iota(I32, (LANES, LANES), 1)
    utri = jnp.where(ri <= ci, 1.0, 0.0).astype(BF16)
    n_gt = jnp.sum(gt, axis=1, keepdims=True)
    eq_rank, _ = _lane_cumsum_excl(eq, utri)
    sel = gt + eq * jnp.where(eq_rank < cap - n_gt, 1.0, 0.0)
    pos, _ = _lane_cumsum_excl(sel, utri)
    base = pl.program_id(0) * slot_stride
    posm = jnp.where(sel > 0.5, pos.astype(I32) + base, -1)
    pos_o[0] = posm
    post_o[0] = jnp.transpose(posm.astype(F32)).astype(I32)
    gate_o[0] = jnp.where(sel > 0.5, a, 0.0)


def _select(aff, cap, slot_stride):
    bsz, e, t = aff.shape
    blk = pl.BlockSpec((1, e, t), lambda b: (b, 0, 0))
    return pl.pallas_call(
        functools.partial(_select_kernel, cap, slot_stride),
        grid=(bsz,),
        in_specs=[blk],
        out_specs=[blk, pl.BlockSpec((1, t, e), lambda b: (b, 0, 0)), blk],
        out_shape=[jax.ShapeDtypeStruct((bsz, e, t), I32),
                   jax.ShapeDtypeStruct((bsz, t, e), I32),
                   jax.ShapeDtypeStruct((bsz, e, t), F32)],
        name="moe_select",
        compiler_params=_cparams(("parallel",)),
    )(aff)


def _ffn_kernel(cap, win, fwin, tk, fk, lo_ref, hi_ref, h_ref, pos_ref, gate_ref, w1_ref, w3_ref, w2_ref,
                ys_o, xin_scr, gsl_scr):
    t = h_ref.shape[1]
    f = w1_ref.shape[3]
    nkt = t // tk
    pos = pos_ref[0, 0]
    gate_tok = gate_ref[0, 0]
    xin_scr[...] = jnp.zeros_like(xin_scr)
    gsl_scr[...] = jnp.zeros_like(gsl_scr)
    base = (pl.program_id(1) * pl.num_programs(0) + pl.program_id(0)) * nkt

    def gather(kt, rows, enabled):
        ts = slice(kt * tk, (kt + 1) * tk)
        lo = lo_ref[base + kt]
        start = pl.multiple_of(jnp.minimum((lo // SUBLANES) * SUBLANES, cap - rows), SUBLANES)
        slot = lax.broadcasted_iota(I32, (rows, tk), 0)
        rel = jnp.where(enabled, pos[:, ts] - start, -1)
        hit = rel == slot
        xin_scr[pl.ds(start, rows), :] += _dot(jnp.where(hit, 1.0, 0.0).astype(BF16), h_ref[0, ts, :])
        gsl_scr[pl.ds(start, rows), :] += jnp.sum(jnp.where(hit, gate_tok[:, ts], 0.0), axis=1, keepdims=True)

    def fits_fast(kt):
        lo = lo_ref[base + kt]
        start = jnp.minimum((lo // SUBLANES) * SUBLANES, cap - fwin)
        return hi_ref[base + kt] <= start + fwin

    for kt in range(nkt):
        gather(kt, fwin, fits_fast(kt) if fwin < win else True)
    if fwin < win:
        slow = [jnp.logical_not(fits_fast(kt)) for kt in range(nkt)]

        @pl.when(functools.reduce(jnp.logical_or, slow))
        def _any_wide():
            for kt in range(nkt):
                @pl.when(slow[kt])
                def _wide():
                    gather(kt, win, True)
    gate_slot = gsl_scr[...]
    xb = xin_scr[...].astype(BF16)
    y = jnp.zeros(xin_scr.shape, F32)
    n_chunks = f // fk

    def up(kf):
        fs = slice(kf * fk, (kf + 1) * fk)
        return _dot(xb, w1_ref[0, 0, :, fs]), _dot(xb, w3_ref[0, 0, :, fs])

    nxt = up(0)
    for kf in range(n_chunks):
        h1, h3 = nxt
        if kf + 1 < n_chunks:
            nxt = up(kf + 1)
        hid = (h1 * _sigmoid(h1) * h3).astype(BF16)
        y = y + _dot(hid, w2_ref[0, 0, kf * fk:(kf + 1) * fk, :])
    ys_o[0, 0] = (y * gate_slot).astype(BF16)


def _ffn(h, posm, gate, layer, w1, w3, w2, cap):
    gsz, t, d = h.shape
    _, e, _, f = w1.shape
    tk = GATHER_TILE
    fk = min(f, FFN_CHUNK)
    win = min(cap, tk + SUBLANES)
    fwin = min(win, GATHER_FAST_ROWS)
    assert t % tk == 0 and (cap - win) % SUBLANES == 0 and (cap - fwin) % SUBLANES == 0
    nkt = t // tk
    pr = posm.reshape(gsz, e, nkt, tk)
    first = jnp.min(jnp.where(pr >= 0, pr, cap), axis=3)
    first = jnp.where(first >= cap, 0, first).reshape(-1)
    last = (jnp.max(pr, axis=3) + 1).reshape(-1)
    return pl.pallas_call(
        functools.partial(_ffn_kernel, cap, win, fwin, tk, fk),
        grid_spec=pltpu.PrefetchScalarGridSpec(
            num_scalar_prefetch=2,
            grid=(e, gsz),
            in_specs=[
                pl.BlockSpec((1, t, d), lambda ei, gi, lo, hi: (gi, 0, 0)),
                pl.BlockSpec((1, 1, 1, t), lambda ei, gi, lo, hi: (gi, ei, 0, 0)),
                pl.BlockSpec((1, 1, 1, t), lambda ei, gi, lo, hi: (gi, ei, 0, 0)),
                pl.BlockSpec((1, 1, d, f), lambda ei, gi, lo, hi: (layer, ei, 0, 0)),
                pl.BlockSpec((1, 1, d, f), lambda ei, gi, lo, hi: (layer, ei, 0, 0)),
                pl.BlockSpec((1, 1, f, d), lambda ei, gi, lo, hi: (layer, ei, 0, 0)),
            ],
            out_specs=pl.BlockSpec((1, 1, cap, d), lambda ei, gi, lo, hi: (gi, ei, 0, 0)),
            scratch_shapes=[pltpu.VMEM((cap, d), F32), pltpu.VMEM((cap, 1), F32)],
        ),
        out_shape=jax.ShapeDtypeStruct((gsz, e, cap, d), BF16),
        name="moe_ffn",
        compiler_params=_cparams(("arbitrary", "arbitrary")),
    )(first, last, h, posm, gate, w1, w3, w2)


def _combine_kernel(wide, pack, fslots, final, lo_ref, hi_ref, x_ref, post_ref, ys_ref, mod_ref, fg_ref,
                    o_ref):
    tt = x_ref.shape[1]
    e = ys_ref.shape[1]
    cap = ys_ref.shape[2]
    base = (pl.program_id(0) * pl.num_programs(1) + pl.program_id(1)) * e
    gate = mod_ref[0, 0, 5:6, :]

    def finish():
        if final:
            xv = o_ref[0]
            ms = jnp.mean(xv * xv, axis=-1, keepdims=True)
            o_ref[0] = xv * lax.rsqrt(ms + RMS_EPS) * fg_ref[...]

    def window_start(ei, rows):
        lo = lo_ref[base + ei]
        return pl.multiple_of(jnp.minimum((lo // BF16_ROWS) * BF16_ROWS, cap - rows), BF16_ROWS)

    def wide_term(ei):
        start = window_start(ei, wide)
        slot = lax.broadcasted_iota(I32, (tt, wide), 1)
        onehot = jnp.where(post_ref[0, :, ei:ei + 1] - start == slot, 1.0, 0.0).astype(BF16)
        return _dot(onehot, ys_ref[0, ei, pl.ds(start, wide), :])

    if not pack:
        acc = wide_term(0)
        for ei in range(1, e):
            acc = acc + wide_term(ei)
        o_ref[0] = x_ref[0] + gate * acc
        finish()
        return

    def fits_fast(ei):
        return hi_ref[base + ei] <= window_start(ei, fslots) + fslots

    lane = lax.broadcasted_iota(I32, (tt, pack * fslots), 1)
    acc = jnp.zeros(x_ref.shape[1:], F32)
    for g in range(e // pack):
        tgt = None
        rows = []
        for q in range(pack):
            ei = g * pack + q
            start = window_start(ei, fslots)
            rel = jnp.where(fits_fast(ei), post_ref[0, :, ei:ei + 1] - start + q * fslots, -1)
            tgt = rel if tgt is None else jnp.where(lane >= q * fslots, rel, tgt)
            rows.append(ys_ref[0, ei, pl.ds(start, fslots), :])
        onehot = jnp.where(tgt == lane, 1.0, 0.0).astype(BF16)
        acc = acc + _dot(onehot, jnp.concatenate(rows, axis=0))
    o_ref[0] = x_ref[0] + gate * acc
    slow = [jnp.logical_not(fits_fast(ei)) for ei in range(e)]

    @pl.when(functools.reduce(jnp.logical_or, slow))
    def _any_wide():
        for ei in range(e):
            @pl.when(slow[ei])
            def _wide():
                o_ref[0] += gate * wide_term(ei)
    finish()


def _combine(xc, post, ys, mod, tile0, n_tiles, shared_slots, final_g=None):
    bsz, tc, d = xc.shape
    tt = TOKEN_TILE
    _, e, cap, _ = ys.shape
    is_ctx = 1 if tile0 > 0 else 0
    ys_map = (lambda b, j, lo, hi: (0, 0, 0, 0)) if shared_slots else (lambda b, j, lo, hi: (b, 0, 0, 0))
    wide = min(cap, tt + BF16_ROWS)
    fslots = COMBINE_FAST_SLOTS
    pack = COMBINE_PACK if (e % COMBINE_PACK == 0 and cap > wide) else 0
    assert (cap - wide) % BF16_ROWS == 0 and (not pack or (cap - fslots) % BF16_ROWS == 0)
    pr = post.reshape(bsz, n_tiles, tt, e)
    first = jnp.min(jnp.where(pr >= 0, pr, cap), axis=2)
    first = jnp.where(first >= cap, 0, first).reshape(-1)
    last = (jnp.max(pr, axis=2) + 1).reshape(-1)
    final = final_g is not None
    fg = final_g if final else jnp.ones((1, d), F32)
    out_tile0 = 0 if final else tile0
    out_rows = n_tiles * tt if final else tc
    return pl.pallas_call(
        functools.partial(_combine_kernel, wide, pack, fslots, final),
        grid_spec=pltpu.PrefetchScalarGridSpec(
            num_scalar_prefetch=2,
            grid=(bsz, n_tiles),
            in_specs=[
                pl.BlockSpec((1, tt, d), lambda b, j, lo, hi: (b, tile0 + j, 0)),
                pl.BlockSpec((1, tt, e), lambda b, j, lo, hi: (b, j, 0)),
                pl.BlockSpec((1, e, cap, d), ys_map),
                pl.BlockSpec((1, 1, 6, d), lambda b, j, lo, hi: (b, is_ctx, 0, 0)),
                pl.BlockSpec((1, d), lambda b, j, lo, hi: (0, 0)),
            ],
            out_specs=pl.BlockSpec((1, tt, d), lambda b, j, lo, hi: (b, out_tile0 + j, 0)),
        ),
        out_shape=jax.ShapeDtypeStruct((bsz, out_rows, d), F32),
        input_output_aliases={} if final else {2: 0},
        name="moe_combine_final" if final else "moe_combine",
        compiler_params=_cparams(("parallel", "arbitrary")),
    )(first, last, xc, post, ys, mod, fg)


def _moe(xc, mod, gn, router_t, layer, w1, w3, w2, nl, nct, seq, ctx_len, do_ctx, final_g):
    assert final_g is None or not do_ctx
    bsz, tc, d = xc.shape
    e = router_t.shape[0]
    h, aff = _router(xc, mod, gn, router_t, 0, nl)
    cap_l = CAPACITY_FACTOR * seq // e
    pos, post, gate = _select(aff, cap_l, 0)
    ys = _ffn(h, pos.reshape(bsz, e, 1, seq), gate.reshape(bsz, e, 1, seq), layer, w1, w3, w2, cap_l)
    xc = _combine(xc, post, ys, mod, 0, nl, False, final_g)
    if do_ctx:
        h_ctx, aff_ctx = _router(xc, mod, gn, router_t, nl, nct)
        cap_c = CAPACITY_FACTOR * ctx_len // e
        pos, post, gate = _select(aff_ctx, cap_c, cap_c)
        pos = jnp.transpose(pos, (1, 0, 2)).reshape(1, e, 1, bsz * ctx_len)
        gate = jnp.transpose(gate, (1, 0, 2)).reshape(1, e, 1, bsz * ctx_len)
        ys = _ffn(h_ctx.reshape(1, bsz * ctx_len, d), pos, gate, layer, w1, w3, w2, bsz * cap_c)
        xc = _combine(xc, post, ys, mod, nl, nct, True)
    return xc


def kernel(x, c, ctx, c_ctx, ada_w, ada_b, norm1_g, norm2_g, rwkv_mix, rwkv_wrkv, rwkv_w0, rwkv_w1, rwkv_w2, rwkv_a0, rwkv_a1, rwkv_a2, rwkv_v0, rwkv_v1, rwkv_v2, rwkv_g1, rwkv_g2, rwkv_kk, rwkv_ka, rwkv_rk, rwkv_lnw, rwkv_lnb, rwkv_wo, pool_w, pool_scale, moe_router, moe_w1, moe_w3, moe_w2, final_g):
    bsz, seq, d = x.shape
    ctx_len = ctx.shape[1]
    depth = ada_w.shape[0]
    n_heads, head = rwkv_rk.shape[1], rwkv_rk.shape[2]
    n_mixers = 2
    tt = TOKEN_TILE
    assert seq % tt == 0 and ctx_len % tt == 0 and tt % GRID_W == 0
    assert head == WKV_CHUNK and 2 * head == LANES and d % LANES == 0
    nl, nct = seq // tt, ctx_len // tt

    xc = jnp.concatenate([x, ctx], axis=1)

    rows = -(-(bsz + 1) // SUBLANES) * SUBLANES
    cond = jnp.zeros((rows, d), F32).at[:bsz].set(c).at[bsz].set(c_ctx)
    m_all = _ada_all(cond, ada_w, ada_b)
    m_lat = m_all[:, :bsz].reshape(depth, bsz, 1, 6, d)
    m_ctx = jnp.broadcast_to(m_all[:, bsz].reshape(depth, 1, 1, 6, d), (depth, bsz, 1, 6, d))
    mods = jnp.concatenate([m_lat, m_ctx], axis=2)

    head_of = jnp.arange(d) // head
    hs = (head_of[:, None] == jnp.arange(HEAD_COLS)[None, :]).astype(BF16)
    hst = jnp.transpose(hs)

    w1_all, w3_all, w2_all = moe_w1.astype(BF16), moe_w3.astype(BF16), moe_w2.astype(BF16)
    v_first = None
    for i in range(depth):
        last = i == depth - 1
        is_rwkv = i % n_mixers == 0
        jn = i // n_mixers
        mod = mods[i]
        gn1 = norm1_g[i].reshape(1, d)
        if is_rwkv:
            p = {
                'mix': rwkv_mix[jn],
                'wr': rwkv_wrkv[jn, 0].astype(BF16), 'wk': rwkv_wrkv[jn, 1].astype(BF16),
                'wv': rwkv_wrkv[jn, 2].astype(BF16),
                'g1': rwkv_g1[jn].astype(BF16), 'g2': rwkv_g2[jn].astype(BF16),
                'w1': _lora_in(rwkv_w1[jn]), 'w2': _lora_out(rwkv_w2[jn]), 'w0': rwkv_w0[jn],
                'a1': _lora_in(rwkv_a1[jn]), 'a2': _lora_out(rwkv_a2[jn]), 'a0': rwkv_a0[jn],
                'k_k': rwkv_kk[jn].reshape(1, d), 'k_a': rwkv_ka[jn].reshape(1, d),
                'r_k': rwkv_rk[jn].reshape(1, d), 'ln_w': rwkv_lnw[jn].reshape(1, d),
                'ln_b': rwkv_lnb[jn].reshape(1, d), 'wo': rwkv_wo[jn].astype(BF16),
                'hs': hs, 'hst': hst,
            }
            vres = None if jn == 0 else (rwkv_v0[jn - 1].reshape(1, d), rwkv_v1[jn - 1].astype(BF16),
                                         rwkv_v2[jn - 1].astype(BF16))
            r, v, g, kk, lw0, lw1, b0, b1, kd0, kd1, bsum = _rwkv_proj(xc, mod, gn1, p, vres, v_first,
                                                                       nl, seq, ctx_len)
            if v_first is None:
                v_first = v
            yf, yr = _wkv(r, v, kk, lw0, b0, kd0, lw1, b1, kd1, seq, ctx_len)
            xc = _rwkv_out(xc, yf, yr, v, g, bsum, mod, p, nl, head)
        else:
            n_tiles = nl if last else nl + nct
            xc = _pool(xc, mod, gn1, pool_w[jn].astype(BF16), pool_scale[jn].reshape(1, d),
                       nl, seq, ctx_len, n_tiles)
        xc = _moe(xc, mod, norm2_g[i].reshape(1, d), jnp.transpose(moe_router[i]),
                  i, w1_all, w3_all, w2_all, nl, nct, seq, ctx_len, not last,
                  final_g.reshape(1, d) if last else None)
    return xc
```

```python
import functools
import math

import jax
import jax.numpy as jnp
from jax import lax
from jax.experimental import pallas as pl
from jax.experimental.pallas import tpu as pltpu

F32 = jnp.float32
BF16 = jnp.bfloat16
I32 = jnp.int32
HIGHEST = lax.Precision.HIGHEST

GRID_W = 64
POOL_WINDOWS = (2, 4, 8, 16)
CAPACITY_FACTOR = 2
RMS_EPS = 1e-6
GN_EPS = 64e-5
EXP_NEG_HALF = math.exp(-0.5)

LANES = 128
SUBLANES = 8
BF16_ROWS = 16
VMEM_LIMIT_BYTES = 56 * 1024 * 1024

KK_NORM_FLOOR = 1e-12

TOKEN_TILE = 256
ADA_COL_TILE = 512
FFN_CHUNK = 512
GATHER_TILE = 256
GATHER_FAST_ROWS = 56
COMBINE_PACK = 4
COMBINE_FAST_SLOTS = 64
WKV_CHUNK = 64
WKV_PAIRS_PER_STEP = 8
POOL_HALO = 8
HEAD_COLS = LANES


def _cparams(sem):
    return pltpu.CompilerParams(dimension_semantics=sem, vmem_limit_bytes=VMEM_LIMIT_BYTES)


def _dot(a, b):
    return jnp.dot(a, b, preferred_element_type=F32)


def _dot_nt(a, b):
    return lax.dot_general(a, b, (((1,), (1,)), ((), ())), preferred_element_type=F32)


def _dot_tn(a, b):
    return lax.dot_general(a, b, (((0,), (0,)), ((), ())), preferred_element_type=F32)


def _split3(x):
    hi = x.astype(BF16)
    r1 = x - hi.astype(F32)
    mid = r1.astype(BF16)
    lo = (r1 - mid.astype(F32)).astype(BF16)
    return hi, mid, lo


def _dot_split(x, m, pieces):
    hi = x.astype(BF16)
    out = _dot(hi, m)
    for _ in range(pieces - 1):
        x = x - hi.astype(F32)
        hi = x.astype(BF16)
        out = out + _dot(hi, m)
    return out


def _dot3_left(m, x):
    w = x.shape[1]
    z = _dot(m, jnp.concatenate(_split3(x), axis=1))
    return z[:, 0:w] + z[:, w:2 * w] + z[:, 2 * w:3 * w]


def _sigmoid(x):
    return 0.5 * jnp.tanh(0.5 * x) + 0.5


def _norm_mod(x, g, shift, scale):
    ms = jnp.mean(x * x, axis=-1, keepdims=True)
    return x * lax.rsqrt(ms + RMS_EPS) * (g * (1.0 + scale)) + shift


def _ada_kernel(c_ref, w_ref, b_ref, o_ref):
    c = c_ref[...]
    s = c * _sigmoid(c)
    o_ref[0] = jnp.dot(s, w_ref[0], precision=HIGHEST, preferred_element_type=F32) + b_ref[0]


def _ada_all(cond, ada_w, ada_b):
    depth, d, n6 = ada_w.shape
    rows = cond.shape[0]
    nt = min(n6, ADA_COL_TILE)
    return pl.pallas_call(
        _ada_kernel,
        grid=(depth, n6 // nt),
        in_specs=[
            pl.BlockSpec((rows, d), lambda i, n: (0, 0)),
            pl.BlockSpec((1, d, nt), lambda i, n: (i, 0, n)),
            pl.BlockSpec((1, 1, nt), lambda i, n: (i, 0, n)),
        ],
        out_specs=pl.BlockSpec((1, rows, nt), lambda i, n: (i, 0, n)),
        out_shape=jax.ShapeDtypeStruct((depth, rows, n6), F32),
        name="ada_mod",
        compiler_params=_cparams(("arbitrary", "arbitrary")),
    )(cond, ada_w, ada_b.reshape(depth, 1, n6))


def _rwkv_proj_kernel(nl, seq, ctx_len, has_vres, *refs):
    if has_vres:
        (x_ref, xp_ref, xn_ref, mod_ref, gn_ref, mix_ref, wr_ref, wk_ref, wv_ref, g1_ref, g2_ref,
         w1_ref, w2_ref, w0_ref, a1_ref, a2_ref, a0_ref, kkw_ref, kaw_ref, rk_ref, hs_ref, hst_ref,
         v0_ref, v1_ref, v2_ref, vf_ref,
         r_o, v_o, g_o, kk_o, lw0_o, lw1_o, b0_o, b1_o, kd0_o, kd1_o, bs_o, ext_scr, sh_scr) = refs
    else:
        (x_ref, xp_ref, xn_ref, mod_ref, gn_ref, mix_ref, wr_ref, wk_ref, wv_ref, g1_ref, g2_ref,
         w1_ref, w2_ref, w0_ref, a1_ref, a2_ref, a0_ref, kkw_ref, kaw_ref, rk_ref, hs_ref, hst_ref,
         r_o, v_o, g_o, kk_o, lw0_o, lw1_o, b0_o, b1_o, kd0_o, kd1_o, bs_o, ext_scr, sh_scr) = refs
    tt = x_ref.shape[1]
    d = x_ref.shape[2]
    hw = GRID_W
    j = pl.program_id(1)
    shift = mod_ref[0, 0, 0:1, :]
    scale = mod_ref[0, 0, 1:2, :]
    gn = gn_ref[...]
    ext_scr[0:hw, :] = _norm_mod(xp_ref[0], gn, shift, scale)
    ext_scr[hw:hw + tt, :] = _norm_mod(x_ref[0], gn, shift, scale)
    ext_scr[hw + tt:, :] = _norm_mod(xn_ref[0], gn, shift, scale)

    i = lax.broadcasted_iota(I32, (tt, 1), 0)
    q = d // 4

    @pl.when(j < nl)
    def _latent_shift():
        t = j * tt + i
        col = i % hw
        sh_scr[:, 0:q] = jnp.where(col != 0, ext_scr[hw - 1:hw - 1 + tt, 0:q], 0.0)
        sh_scr[:, q:2 * q] = jnp.where(col != hw - 1, ext_scr[hw + 1:hw + 1 + tt, q:2 * q], 0.0)
        sh_scr[:, 2 * q:3 * q] = jnp.where(t >= hw, ext_scr[0:tt, 2 * q:3 * q], 0.0)
        sh_scr[:, 3 * q:] = jnp.where(t < seq - hw, ext_scr[2 * hw:2 * hw + tt, 3 * q:], 0.0)

    @pl.when(j >= nl)
    def _context_shift():
        t = (j - nl) * tt + i
        hd = d // 2
        sh_scr[:, 0:hd] = jnp.where(t != 0, ext_scr[hw - 1:hw - 1 + tt, 0:hd], 0.0)
        sh_scr[:, hd:] = jnp.where(t != ctx_len - 1, ext_scr[hw + 1:hw + 1 + tt, hd:], 0.0)

    h = ext_scr[hw:hw + tt, :]
    xx = sh_scr[...] - h

    def mixed(n):
        return (h + xx * mix_ref[n:n + 1, :]).astype(BF16)

    xr, xw, xk, xv, xa, xg = [mixed(n) for n in range(6)]
    r = _dot(xr, wr_ref[...])
    k = _dot(xk, wk_ref[...])
    v = _dot(xv, wv_ref[...])
    if has_vres:
        lor = _dot(_dot(xv, v1_ref[...]).astype(BF16), v2_ref[...])
        v = v + (vf_ref[0] - v) * _sigmoid(v0_ref[...] + lor)
    g = _dot(_sigmoid(_dot(xg, g1_ref[...])).astype(BF16), g2_ref[...])
    r_o[0] = r
    v_o[0] = v
    g_o[0] = g

    kkr = k * kkw_ref[...]
    ss = _dot_split(kkr * kkr, hs_ref[...], 1)
    inv = 1.0 / jnp.maximum(jnp.sqrt(ss), KK_NORM_FLOOR)
    kk = kkr * _dot_split(inv, hst_ref[...], 2)
    kk_o[0] = kk
    kaw = kaw_ref[...]
    kb = None
    tw = jnp.tanh(_dot(xw, w1_ref[...])).astype(BF16)
    ta = _dot(xa, a1_ref[...]).astype(BF16)
    for dr, (lw_o, b_o, kd_o) in enumerate(((lw0_o, b0_o, kd0_o), (lw1_o, b1_o, kd1_o))):
        wpre = w0_ref[dr:dr + 1, :] + _dot(tw, w2_ref[dr])
        lw_o[0] = -EXP_NEG_HALF * _sigmoid(wpre)
        a = _sigmoid(a0_ref[dr:dr + 1, :] + _dot(ta, a2_ref[dr]))
        b_o[0] = kk * a
        kd = k * (1.0 + (a - 1.0) * kaw)
        kd_o[0] = kd
        kb = 0.5 * kd if kb is None else kb + 0.5 * kd
    bs_o[0] = _dot_split(r * kb * rk_ref[...], hs_ref[...], 1)


def _lora_in(w):
    return jnp.concatenate([w[i] for i in range(w.shape[0])], axis=1).astype(BF16)


def _lora_out(w):
    n, r, _ = w.shape
    rows = jnp.arange(n * r) // r
    return jnp.where((rows[None, :] == jnp.arange(n)[:, None])[:, :, None],
                     jnp.tile(w, (1, n, 1)), 0.0).astype(BF16)


def _rwkv_proj(xc, mod, gn, p, vres, v_first, nl, seq, ctx_len):
    bsz, tc, d = xc.shape
    tt = TOKEN_TILE
    nt = tc // tt
    hb = tt // GRID_W
    nhb = tc // GRID_W
    has_vres = vres is not None

    def full(a):
        nd = a.ndim
        return pl.BlockSpec(a.shape, lambda b, j, _n=nd: (0,) * _n)

    tile = pl.BlockSpec((1, tt, d), lambda b, j: (b, j, 0))
    ins = [xc, xc, xc, mod, gn, p['mix'], p['wr'], p['wk'], p['wv'], p['g1'], p['g2'],
           p['w1'], p['w2'], p['w0'], p['a1'], p['a2'], p['a0'], p['k_k'], p['k_a'], p['r_k'], p['hs'], p['hst']]
    specs = [
        tile,
        pl.BlockSpec((1, GRID_W, d), lambda b, j: (b, jnp.maximum(j * hb - 1, 0), 0)),
        pl.BlockSpec((1, GRID_W, d), lambda b, j: (b, jnp.minimum((j + 1) * hb, nhb - 1), 0)),
        pl.BlockSpec((1, 1, 6, d), lambda b, j: (b, (j >= nl).astype(I32), 0, 0)),
    ] + [full(a) for a in ins[4:]]
    if has_vres:
        ins += [vres[0], vres[1], vres[2], v_first]
        specs += [full(vres[0]), full(vres[1]), full(vres[2]), tile]
    out_sds = jax.ShapeDtypeStruct((bsz, tc, d), F32)
    return pl.pallas_call(
        functools.partial(_rwkv_proj_kernel, nl, seq, ctx_len, has_vres),
        grid=(bsz, nt),
        in_specs=specs,
        out_specs=[tile] * 10 + [pl.BlockSpec((1, tt, HEAD_COLS), lambda b, j: (b, j, 0))],
        out_shape=[out_sds] * 10 + [jax.ShapeDtypeStruct((bsz, tc, HEAD_COLS), F32)],
        scratch_shapes=[pltpu.VMEM((tt + 2 * GRID_W, d), F32), pltpu.VMEM((tt, d), F32)],
        name="rwkv_proj",
        compiler_params=_cparams(("parallel", "arbitrary")),
    )(*ins)


def _wkv_chains(chains, masks):
    nc = len(chains)
    ks = range(nc)
    L = chains[0][0].shape[0]
    r, v, kk, lw, b, kd, s_prev, rev = [[ch[i] for ch in chains] for i in range(8)]
    mk = [masks[1] if rv else masks[0] for rv in rev]
    tri, m0, m1, strict, incl, eye2, bd = [[m[i] for m in mk] for i in range(7)]

    def stack(k, x):
        xb = x.astype(BF16)
        return jnp.concatenate([xb * m0[k], xb * m1[k]], axis=0)

    c = [_dot3_left(tri[k], lw[k]) for k in ks]
    ctot = [c[k][0:1, :] if rev[k] else c[k][L - 1:L, :] for k in ks]
    e_c = [jnp.exp(c[k]) for k in ks]
    e_nc = [jnp.exp(-c[k]) for k in ks]
    e_tc = [jnp.exp(ctot[k] - c[k]) for k in ks]
    ah = [-kk[k] * jnp.exp(c[k] - lw[k]) for k in ks]
    rh = [r[k] * e_c[k] for k in ks]
    lhs = [jnp.concatenate([ah[k], rh[k]], axis=0).astype(BF16) for k in ks]
    rhs = [jnp.concatenate([stack(k, b[k] * e_nc[k]), stack(k, kd[k] * e_nc[k])], axis=0) for k in ks]
    aa = [_dot_nt(lhs[k], rhs[k]) for k in ks]
    a_ab = [jnp.where(strict[k], aa[k][0:L, 0:2 * L], 0.0) for k in ks]
    a_ak = [jnp.where(strict[k], aa[k][0:L, 2 * L:4 * L], 0.0).astype(BF16) for k in ks]
    a_r = [jnp.where(jnp.concatenate([incl[k], incl[k]], axis=1), aa[k][L:2 * L, :], 0.0).astype(BF16)
           for k in ks]

    n_dbl = int(math.log2(L))
    tm = [eye2[k] + a_ab[k] for k in ks]
    pw = [_dot(a_ab[k].astype(BF16), stack(k, a_ab[k])) for k in ks]
    for _ in range(n_dbl - 2):
        z = [_dot(pw[k].astype(BF16), jnp.concatenate([stack(k, tm[k]), stack(k, pw[k])], axis=1)) for k in ks]
        tm = [tm[k] + z[k][:, 0:2 * L] for k in ks]
        pw = [z[k][:, 2 * L:4 * L] for k in ks]
    tm = [tm[k] + _dot(pw[k].astype(BF16), stack(k, tm[k])) for k in ks]

    ss0 = [_dot_nt(lhs[k], s_prev[k].astype(BF16)) for k in ks]
    vs = [stack(k, v[k]) for k in ks]
    wmat = [ss0[k][0:L] + _dot(a_ak[k], vs[k]) for k in ks]
    u = [_dot(tm[k].astype(BF16), stack(k, wmat[k])) for k in ks]
    y = [ss0[k][L:2 * L] + _dot(a_r[k], jnp.concatenate([stack(k, u[k]), vs[k]], axis=0)) for k in ks]
    uv = [jnp.concatenate([u[k], v[k]], axis=0).astype(BF16) for k in ks]
    bk = [jnp.concatenate([b[k] * e_tc[k], kd[k] * e_tc[k]], axis=0).astype(BF16) for k in ks]
    upd = [_dot_tn(uv[k], bk[k]) for k in ks]
    s_new = [s_prev[k] * jnp.exp(ctot[k]) + jnp.where(bd[k], upd[k], 0.0) for k in ks]
    return y, s_new


def _wkv_masks(L, w2, reverse):
    n = w2 // 2
    row = lax.broadcasted_iota(I32, (L, L), 0)
    colm = lax.broadcasted_iota(I32, (L, L), 1)
    tri = jnp.where((colm >= row) if reverse else (colm <= row), 1.0, 0.0).astype(BF16)
    lane = lax.broadcasted_iota(I32, (1, w2), 1)
    m0 = jnp.where(lane < n, 1.0, 0.0).astype(BF16)
    m1 = jnp.where(lane < n, 0.0, 1.0).astype(BF16)
    t_i = lax.broadcasted_iota(I32, (L, 2 * L), 0)
    s_i = lax.broadcasted_iota(I32, (L, 2 * L), 1) % L
    strict = (s_i > t_i) if reverse else (s_i < t_i)
    incl = (s_i >= t_i) if reverse else (s_i <= t_i)
    eye2 = jnp.where(s_i == t_i, 1.0, 0.0)
    ri = lax.broadcasted_iota(I32, (w2, w2), 0)
    ci = lax.broadcasted_iota(I32, (w2, w2), 1)
    bd = (ri < n) == (ci < n)
    return tri, m0, m1, strict, incl, eye2, bd


def _wkv_kernel(rf, vf, kkf, lwf, bf, kdf, rr, vr, kkr, lwr, br, kdr, yf_o, yr_o, s_scr):
    @pl.when(pl.program_id(2) == 0)
    def _init():
        s_scr[...] = jnp.zeros_like(s_scr)

    L = rf.shape[1]
    masks = (_wkv_masks(L, LANES, False), _wkv_masks(L, LANES, True))
    chains = []
    for hp in range(rf.shape[2] // LANES):
        ls = slice(hp * LANES, (hp + 1) * LANES)
        chains.append((rf[0, :, ls], vf[0, :, ls], kkf[0, :, ls], lwf[0, :, ls], bf[0, :, ls],
                       kdf[0, :, ls], s_scr[0, hp], False))
        chains.append((rr[0, :, ls], vr[0, :, ls], kkr[0, :, ls], lwr[0, :, ls], br[0, :, ls],
                       kdr[0, :, ls], s_scr[1, hp], True))
    ys, ss = _wkv_chains(chains, masks)
    for hp in range(rf.shape[2] // LANES):
        ls = slice(hp * LANES, (hp + 1) * LANES)
        yf_o[0, :, ls] = ys[2 * hp]
        yr_o[0, :, ls] = ys[2 * hp + 1]
        s_scr[0, hp] = ss[2 * hp]
        s_scr[1, hp] = ss[2 * hp + 1]


def _wkv(r, v, kk, lw0, b0, kd0, lw1, b1, kd1, seq, ctx_len):
    bsz, tc, d = r.shape
    L = WKV_CHUNK
    nlc = seq // L
    ncc = ctx_len // L
    nch = nlc + ncc
    gp = min(WKV_PAIRS_PER_STEP, d // LANES)
    width = gp * LANES

    def fwd_map(b, h, c):
        return (b, jnp.where(c < ncc, nlc + c, c - ncc), h)

    def rev_map(b, h, c):
        return (b, jnp.where(c < ncc, nlc + ncc - 1 - c, nlc - 1 - (c - ncc)), h)

    fs = pl.BlockSpec((1, L, width), fwd_map)
    rs = pl.BlockSpec((1, L, width), rev_map)
    sds = jax.ShapeDtypeStruct((bsz, tc, d), F32)
    return pl.pallas_call(
        _wkv_kernel,
        grid=(bsz, d // width, nch),
        in_specs=[fs] * 6 + [rs] * 6,
        out_specs=[fs, rs],
        out_shape=[sds, sds],
        scratch_shapes=[pltpu.VMEM((2, gp, LANES, LANES), F32)],
        name="wkv_scan",
        compiler_params=_cparams(("parallel", "parallel", "arbitrary")),
    )(r, v, kk, lw0, b0, kd0, r, v, kk, lw1, b1, kd1)


def _rwkv_out_kernel(head, x_ref, yf_ref, yr_ref, v_ref, g_ref, bs_ref, mod_ref,
                     lnw_ref, lnb_ref, wo_ref, hs_ref, hst_ref, o_ref):
    hs = hs_ref[...]
    hst = hst_ref[...]
    o = yf_ref[0] + yr_ref[0]
    inv_n = 1.0 / head
    mu = _dot_split(_dot_split(o, hs, 2) * inv_n, hst, 2)
    dlt = o - mu
    var = _dot_split(dlt * dlt, hs, 1) * inv_n
    on = dlt * _dot_split(lax.rsqrt(var + GN_EPS), hst, 2)
    on = on * lnw_ref[...] + lnb_ref[...]
    bonus = _dot_split(bs_ref[0], hst, 2) * v_ref[0]
    y = ((on + bonus) * g_ref[0]).astype(BF16)
    gate = mod_ref[0, 0, 2:3, :]
    o_ref[0] = x_ref[0] + gate * _dot(y, wo_ref[...])


def _rwkv_out(xc, yf, yr, v, g, bsum, mod, p, nl, head):
    bsz, tc, d = xc.shape
    tt = TOKEN_TILE
    tile = pl.BlockSpec((1, tt, d), lambda b, j: (b, j, 0))

    def full(a):
        nd = a.ndim
        return pl.BlockSpec(a.shape, lambda b, j, _n=nd: (0,) * _n)

    consts = [p['ln_w'], p['ln_b'], p['wo'], p['hs'], p['hst']]
    return pl.pallas_call(
        functools.partial(_rwkv_out_kernel, head),
        grid=(bsz, tc // tt),
        in_specs=[tile] * 5 + [pl.BlockSpec((1, tt, HEAD_COLS), lambda b, j: (b, j, 0)),
                               pl.BlockSpec((1, 1, 6, d), lambda b, j: (b, (j >= nl).astype(I32), 0, 0))]
        + [full(a) for a in consts],
        out_specs=tile,
        out_shape=jax.ShapeDtypeStruct((bsz, tc, d), F32),
        input_output_aliases={0: 0},
        name="rwkv_out",
        compiler_params=_cparams(("parallel", "arbitrary")),
    )(xc, yf, yr, v, g, bsum, mod, *consts)


def _pool_kernel(nl, seq, ctx_len, x_ref, xp_ref, xn_ref, mod_ref, gn_ref, w_ref, sc_ref, o_ref, ext_scr):
    tt = x_ref.shape[1]
    d = x_ref.shape[2]
    ph = POOL_HALO
    j = pl.program_id(1)
    nct = ctx_len // tt
    shift = mod_ref[0, 0, 0:1, :]
    scale = mod_ref[0, 0, 1:2, :]
    gn = gn_ref[...]
    first = jnp.logical_or(j == 0, j == nl)
    last = jnp.logical_or(j == nl - 1, j == nl + nct - 1)
    hp = _norm_mod(xp_ref[0], gn, shift, scale)
    hn = _norm_mod(xn_ref[0], gn, shift, scale)
    ext_scr[0:ph, :] = jnp.where(first, 0.0, hp)
    ext_scr[ph:ph + tt, :] = _norm_mod(x_ref[0], gn, shift, scale)
    ext_scr[ph + tt:, :] = jnp.where(last, 0.0, hn)

    i = lax.broadcasted_iota(I32, (tt, 1), 0)
    t = jnp.where(j < nl, j * tt + i, (j - nl) * tt + i)
    tseg = jnp.where(j < nl, seq, ctx_len)
    ng = len(POOL_WINDOWS)
    dg = d // ng
    rows = tt + 2 * ph
    assert all(w & (w - 1) == 0 and w // 2 <= ph for w in POOL_WINDOWS)
    gate = mod_ref[0, 0, 2:3, :]
    for gi, win in enumerate(POOL_WINDOWS):
        half = win // 2
        cs = slice(gi * dg, (gi + 1) * dg)
        run = ext_scr[:, cs]
        span = 1
        while span < win:
            run = run + pltpu.roll(run, rows - span, axis=0)
            span *= 2
        acc = run[ph - half:ph - half + tt]
        cnt = (jnp.minimum(t + half, tseg) - jnp.maximum(t - half, 0)).astype(F32)
        hg = ext_scr[ph:ph + tt, cs]
        dlt = (acc / cnt - hg).astype(BF16)
        y = _dot(dlt, w_ref[gi]) * sc_ref[:, cs]
        o_ref[0, :, cs] = x_ref[0, :, cs] + gate[:, cs] * y


def _pool(xc, mod, gn, w, sc, nl, seq, ctx_len, n_tiles):
    bsz, tc, d = xc.shape
    tt = TOKEN_TILE
    hb = tt // POOL_HALO
    nhb = tc // POOL_HALO
    tile = pl.BlockSpec((1, tt, d), lambda b, j: (b, j, 0))
    return pl.pallas_call(
        functools.partial(_pool_kernel, nl, seq, ctx_len),
        grid=(bsz, n_tiles),
        in_specs=[
            tile,
            pl.BlockSpec((1, POOL_HALO, d), lambda b, j: (b, jnp.maximum(j * hb - 1, 0), 0)),
            pl.BlockSpec((1, POOL_HALO, d), lambda b, j: (b, jnp.minimum((j + 1) * hb, nhb - 1), 0)),
            pl.BlockSpec((1, 1, 6, d), lambda b, j: (b, (j >= nl).astype(I32), 0, 0)),
            pl.BlockSpec(gn.shape, lambda b, j: (0, 0)),
            pl.BlockSpec(w.shape, lambda b, j: (0, 0, 0)),
            pl.BlockSpec(sc.shape, lambda b, j: (0, 0)),
        ],
        out_specs=tile,
        out_shape=jax.ShapeDtypeStruct((bsz, n_tiles * tt, d), F32),
        scratch_shapes=[pltpu.VMEM((tt + 2 * POOL_HALO, d), F32)],
        name="pool_mix",
        compiler_params=_cparams(("parallel", "arbitrary")),
    )(xc, xc, xc, mod, gn, w, sc)


def _router_kernel(x_ref, mod_ref, gn_ref, rth_ref, rtl_ref, h_o, aff_o):
    h = _norm_mod(x_ref[0], gn_ref[...], mod_ref[0, 0, 3:4, :], mod_ref[0, 0, 4:5, :])
    hb = h.astype(BF16)
    h_o[0] = hb
    hl = (h - hb.astype(F32)).astype(BF16)
    logits = _dot_nt(rth_ref[...], hb) + (_dot_nt(rth_ref[...], hl) + _dot_nt(rtl_ref[...], hb))
    m = jnp.max(logits, axis=0, keepdims=True)
    ex = jnp.exp(logits - m)
    aff_o[0] = ex / jnp.sum(ex, axis=0, keepdims=True)


def _router(xc, mod, gn, router_t, tile0, n_tiles):
    bsz, tc, d = xc.shape
    tt = TOKEN_TILE
    e = router_t.shape[0]
    is_ctx = 1 if tile0 > 0 else 0
    rt_hi = router_t.astype(BF16)
    return pl.pallas_call(
        _router_kernel,
        grid=(bsz, n_tiles),
        in_specs=[
            pl.BlockSpec((1, tt, d), lambda b, j: (b, tile0 + j, 0)),
            pl.BlockSpec((1, 1, 6, d), lambda b, j: (b, is_ctx, 0, 0)),
            pl.BlockSpec(gn.shape, lambda b, j: (0, 0)),
            pl.BlockSpec(router_t.shape, lambda b, j: (0, 0)),
            pl.BlockSpec(router_t.shape, lambda b, j: (0, 0)),
        ],
        out_specs=[pl.BlockSpec((1, tt, d), lambda b, j: (b, j, 0)),
                   pl.BlockSpec((1, e, tt), lambda b, j: (b, 0, j))],
        out_shape=[jax.ShapeDtypeStruct((bsz, n_tiles * tt, d), BF16),
                   jax.ShapeDtypeStruct((bsz, e, n_tiles * tt), F32)],
        name="moe_router",
        compiler_params=_cparams(("parallel", "arbitrary")),
    )(xc, mod, gn, rt_hi, (router_t - rt_hi.astype(F32)).astype(BF16))


def _lane_cumsum_excl(x, utri):
    e, t = x.shape
    off = jnp.zeros((e, 1), F32)
    parts = []
    for kb in range(t // LANES):
        blk = x[:, kb * LANES:(kb + 1) * LANES]
        inc = _dot(blk.astype(BF16), utri)
        parts.append(inc - blk + off)
        off = off + inc[:, LANES - 1:LANES]
    return jnp.concatenate(parts, axis=1), off


def _select_kernel(cap, slot_stride, aff_ref, pos_o, post_o, gate_o):
    a = aff_ref[0]
    e, t = a.shape
    bits = pltpu.bitcast(a, I32)

    def body(it, thr):
        cand = thr | lax.shift_left(jnp.int32(1), jnp.int32(29) - it)
        cnt = jnp.sum(jnp.where(bits >= cand, 1.0, 0.0), axis=1, keepdims=True)
        return jnp.where(cnt >= cap, cand, thr)

    thr = lax.fori_loop(0, 30, body, jnp.zeros((e, 1), I32))
    gt = jnp.where(bits > thr, 1.0, 0.0)
    eq = jnp.where(bits == thr, 1.0, 0.0)
    ri = lax.broadcasted_iota(I32, (LANES, LANES), 0)
    ci = lax.broadcasted_iota(I32, (LANES, LANES), 1)
    utri = jnp.where(ri <= ci, 1.0, 0.0).astype(BF16)
    n_gt = jnp.sum(gt, axis=1, keepdims=True)
    eq_rank, _ = _lane_cumsum_excl(eq, utri)
    sel = gt + eq * jnp.where(eq_rank < cap - n_gt, 1.0, 0.0)
    pos, _ = _lane_cumsum_excl(sel, utri)
    base = pl.program_id(0) * slot_stride
    posm = jnp.where(sel > 0.5, pos.astype(I32) + base, -1)
    pos_o[0] = posm
    post_o[0] = jnp.transpose(posm.astype(F32)).astype(I32)
    gate_o[0] = jnp.where(sel > 0.5, a, 0.0)


def _select(aff, cap, slot_stride):
    bsz, e, t = aff.shape
    blk = pl.BlockSpec((1, e, t), lambda b: (b, 0, 0))
    return pl.pallas_call(
        functools.partial(_select_kernel, cap, slot_stride),
        grid=(bsz,),
        in_specs=[blk],
        out_specs=[blk, pl.BlockSpec((1, t, e), lambda b: (b, 0, 0)), blk],
        out_shape=[jax.ShapeDtypeStruct((bsz, e, t), I32),
                   jax.ShapeDtypeStruct((bsz, t, e), I32),
                   jax.ShapeDtypeStruct((bsz, e, t), F32)],
        name="moe_select",
        compiler_params=_cparams(("parallel",)),
    )(aff)


def _ffn_kernel(cap, win, fwin, tk, fk, lo_ref, hi_ref, h_ref, pos_ref, gate_ref, w1_ref, w3_ref, w2_ref,
                ys_o, xin_scr, gsl_scr):
    t = h_ref.shape[1]
    f = w1_ref.shape[3]
    nkt = t // tk
    pos = pos_ref[0, 0]
    gate_tok = gate_ref[0, 0]
    xin_scr[...] = jnp.zeros_like(xin_scr)
    gsl_scr[...] = jnp.zeros_like(gsl_scr)
    base = (pl.program_id(1) * pl.num_programs(0) + pl.program_id(0)) * nkt

    def gather(kt, rows, enabled):
        ts = slice(kt * tk, (kt + 1) * tk)
        lo = lo_ref[base + kt]
        start = pl.multiple_of(jnp.minimum((lo // SUBLANES) * SUBLANES, cap - rows), SUBLANES)
        slot = lax.broadcasted_iota(I32, (rows, tk), 0)
        rel = jnp.where(enabled, pos[:, ts] - start, -1)
        hit = rel == slot
        xin_scr[pl.ds(start, rows), :] += _dot(jnp.where(hit, 1.0, 0.0).astype(BF16), h_ref[0, ts, :])
        gsl_scr[pl.ds(start, rows), :] += jnp.sum(jnp.where(hit, gate_tok[:, ts], 0.0), axis=1, keepdims=True)

    def fits_fast(kt):
        lo = lo_ref[base + kt]
        start = jnp.minimum((lo // SUBLANES) * SUBLANES, cap - fwin)
        return hi_ref[base + kt] <= start + fwin

    for kt in range(nkt):
        gather(kt, fwin, fits_fast(kt) if fwin < win else True)
    if fwin < win:
        slow = [jnp.logical_not(fits_fast(kt)) for kt in range(nkt)]

        @pl.when(functools.reduce(jnp.logical_or, slow))
        def _any_wide():
            for kt in range(nkt):
                @pl.when(slow[kt])
                def _wide():
                    gather(kt, win, True)
    gate_slot = gsl_scr[...]
    xb = xin_scr[...].astype(BF16)
    y = jnp.zeros(xin_scr.shape, F32)
    n_chunks = f // fk

    def up(kf):
        fs = slice(kf * fk, (kf + 1) * fk)
        return _dot(xb, w1_ref[0, 0, :, fs]), _dot(xb, w3_ref[0, 0, :, fs])

    nxt = up(0)
    for kf in range(n_chunks):
        h1, h3 = nxt
        if kf + 1 < n_chunks:
            nxt = up(kf + 1)
        hid = (h1 * _sigmoid(h1) * h3).astype(BF16)
        y = y + _dot(hid, w2_ref[0, 0, kf * fk:(kf + 1) * fk, :])
    ys_o[0, 0] = (y * gate_slot).astype(BF16)


def _ffn(h, posm, gate, layer, w1, w3, w2, cap):
    gsz, t, d = h.shape
    _, e, _, f = w1.shape
    tk = GATHER_TILE
    fk = min(f, FFN_CHUNK)
    win = min(cap, tk + SUBLANES)
    fwin = min(win, GATHER_FAST_ROWS)
    assert t % tk == 0 and (cap - win) % SUBLANES == 0 and (cap - fwin) % SUBLANES == 0
    nkt = t // tk
    pr = posm.reshape(gsz, e, nkt, tk)
    first = jnp.min(jnp.where(pr >= 0, pr, cap), axis=3)
    first = jnp.where(first >= cap, 0, first).reshape(-1)
    last = (jnp.max(pr, axis=3) + 1).reshape(-1)
    return pl.pallas_call(
        functools.partial(_ffn_kernel, cap, win, fwin, tk, fk),
        grid_spec=pltpu.PrefetchScalarGridSpec(
            num_scalar_prefetch=2,
            grid=(e, gsz),
            in_specs=[
                pl.BlockSpec((1, t, d), lambda ei, gi, lo, hi: (gi, 0, 0)),
                pl.BlockSpec((1, 1, 1, t), lambda ei, gi, lo, hi: (gi, ei, 0, 0)),
                pl.BlockSpec((1, 1, 1, t), lambda ei, gi, lo, hi: (gi, ei, 0, 0)),
                pl.BlockSpec((1, 1, d, f), lambda ei, gi, lo, hi: (layer, ei, 0, 0)),
                pl.BlockSpec((1, 1, d, f), lambda ei, gi, lo, hi: (layer, ei, 0, 0)),
                pl.BlockSpec((1, 1, f, d), lambda ei, gi, lo, hi: (layer, ei, 0, 0)),
            ],
            out_specs=pl.BlockSpec((1, 1, cap, d), lambda ei, gi, lo, hi: (gi, ei, 0, 0)),
            scratch_shapes=[pltpu.VMEM((cap, d), F32), pltpu.VMEM((cap, 1), F32)],
        ),
        out_shape=jax.ShapeDtypeStruct((gsz, e, cap, d), BF16),
        name="moe_ffn",
        compiler_params=_cparams(("arbitrary", "arbitrary")),
    )(first, last, h, posm, gate, w1, w3, w2)


def _combine_kernel(wide, pack, fslots, final, lo_ref, hi_ref, x_ref, post_ref, ys_ref, mod_ref, fg_ref,
                    o_ref):
    tt = x_ref.shape[1]
    e = ys_ref.shape[1]
    cap = ys_ref.shape[2]
    base = (pl.program_id(0) * pl.num_programs(1) + pl.program_id(1)) * e
    gate = mod_ref[0, 0, 5:6, :]

    def finish():
        if final:
            xv = o_ref[0]
            ms = jnp.mean(xv * xv, axis=-1, keepdims=True)
            o_ref[0] = xv * lax.rsqrt(ms + RMS_EPS) * fg_ref[...]

    def window_start(ei, rows):
        lo = lo_ref[base + ei]
        return pl.multiple_of(jnp.minimum((lo // BF16_ROWS) * BF16_ROWS, cap - rows), BF16_ROWS)

    def wide_term(ei):
        start = window_start(ei, wide)
        slot = lax.broadcasted_iota(I32, (tt, wide), 1)
        onehot = jnp.where(post_ref[0, :, ei:ei + 1] - start == slot, 1.0, 0.0).astype(BF16)
        return _dot(onehot, ys_ref[0, ei, pl.ds(start, wide), :])

    if not pack:
        acc = wide_term(0)
        for ei in range(1, e):
            acc = acc + wide_term(ei)
        o_ref[0] = x_ref[0] + gate * acc
        finish()
        return

    def fits_fast(ei):
        return hi_ref[base + ei] <= window_start(ei, fslots) + fslots

    lane = lax.broadcasted_iota(I32, (tt, pack * fslots), 1)
    acc = jnp.zeros(x_ref.shape[1:], F32)
    for g in range(e // pack):
        tgt = None
        rows = []
        for q in range(pack):
            ei = g * pack + q
            start = window_start(ei, fslots)
            rel = jnp.where(fits_fast(ei), post_ref[0, :, ei:ei + 1] - start + q * fslots, -1)
            tgt = rel if tgt is None else jnp.where(lane >= q * fslots, rel, tgt)
            rows.append(ys_ref[0, ei, pl.ds(start, fslots), :])
        onehot = jnp.where(tgt == lane, 1.0, 0.0).astype(BF16)
        acc = acc + _dot(onehot, jnp.concatenate(rows, axis=0))
    o_ref[0] = x_ref[0] + gate * acc
    slow = [jnp.logical_not(fits_fast(ei)) for ei in range(e)]

    @pl.when(functools.reduce(jnp.logical_or, slow))
    def _any_wide():
        for ei in range(e):
            @pl.when(slow[ei])
            def _wide():
                o_ref[0] += gate * wide_term(ei)
    finish()


def _combine(xc, post, ys, mod, tile0, n_tiles, shared_slots, final_g=None):
    bsz, tc, d = xc.shape
    tt = TOKEN_TILE
    _, e, cap, _ = ys.shape
    is_ctx = 1 if tile0 > 0 else 0
    ys_map = (lambda b, j, lo, hi: (0, 0, 0, 0)) if shared_slots else (lambda b, j, lo, hi: (b, 0, 0, 0))
    wide = min(cap, tt + BF16_ROWS)
    fslots = COMBINE_FAST_SLOTS
    pack = COMBINE_PACK if (e % COMBINE_PACK == 0 and cap > wide) else 0
    assert (cap - wide) % BF16_ROWS == 0 and (not pack or (cap - fslots) % BF16_ROWS == 0)
    pr = post.reshape(bsz, n_tiles, tt, e)
    first = jnp.min(jnp.where(pr >= 0, pr, cap), axis=2)
    first = jnp.where(first >= cap, 0, first).reshape(-1)
    last = (jnp.max(pr, axis=2) + 1).reshape(-1)
    final = final_g is not None
    fg = final_g if final else jnp.ones((1, d), F32)
    out_tile0 = 0 if final else tile0
    out_rows = n_tiles * tt if final else tc
    return pl.pallas_call(
        functools.partial(_combine_kernel, wide, pack, fslots, final),
        grid_spec=pltpu.PrefetchScalarGridSpec(
            num_scalar_prefetch=2,
            grid=(bsz, n_tiles),
            in_specs=[
                pl.BlockSpec((1, tt, d), lambda b, j, lo, hi: (b, tile0 + j, 0)),
                pl.BlockSpec((1, tt, e), lambda b, j, lo, hi: (b, j, 0)),
                pl.BlockSpec((1, e, cap, d), ys_map),
                pl.BlockSpec((1, 1, 6, d), lambda b, j, lo, hi: (b, is_ctx, 0, 0)),
                pl.BlockSpec((1, d), lambda b, j, lo, hi: (0, 0)),
            ],
            out_specs=pl.BlockSpec((1, tt, d), lambda b, j, lo, hi: (b, out_tile0 + j, 0)),
        ),
        out_shape=jax.ShapeDtypeStruct((bsz, out_rows, d), F32),
        input_output_aliases={} if final else {2: 0},
        name="moe_combine_final" if final else "moe_combine",
        compiler_params=_cparams(("parallel", "arbitrary")),
    )(first, last, xc, post, ys, mod, fg)


def _moe(xc, mod, gn, router_t, layer, w1, w3, w2, nl, nct, seq, ctx_len, do_ctx, final_g):
    assert final_g is None or not do_ctx
    bsz, tc, d = xc.shape
    e = router_t.shape[0]
    h, aff = _router(xc, mod, gn, router_t, 0, nl)
    cap_l = CAPACITY_FACTOR * seq // e
    pos, post, gate = _select(aff, cap_l, 0)
    ys = _ffn(h, pos.reshape(bsz, e, 1, seq), gate.reshape(bsz, e, 1, seq), layer, w1, w3, w2, cap_l)
    xc = _combine(xc, post, ys, mod, 0, nl, False, final_g)
    if do_ctx:
        h_ctx, aff_ctx = _router(xc, mod, gn, router_t, nl, nct)
        cap_c = CAPACITY_FACTOR * ctx_len // e
        pos, post, gate = _select(aff_ctx, cap_c, cap_c)
        pos = jnp.transpose(pos, (1, 0, 2)).reshape(1, e, 1, bsz * ctx_len)
        gate = jnp.transpose(gate, (1, 0, 2)).reshape(1, e, 1, bsz * ctx_len)
        ys = _ffn(h_ctx.reshape(1, bsz * ctx_len, d), pos, gate, layer, w1, w3, w2, bsz * cap_c)
        xc = _combine(xc, post, ys, mod, nl, nct, True)
    return xc


def kernel(x, c, ctx, c_ctx, ada_w, ada_b, norm1_g, norm2_g, rwkv_mix, rwkv_wrkv, rwkv_w0, rwkv_w1, rwkv_w2, rwkv_a0, rwkv_a1, rwkv_a2, rwkv_v0, rwkv_v1, rwkv_v2, rwkv_g1, rwkv_g2, rwkv_kk, rwkv_ka, rwkv_rk, rwkv_lnw, rwkv_lnb, rwkv_wo, pool_w, pool_scale, moe_router, moe_w1, moe_w3, moe_w2, final_g):
    bsz, seq, d = x.shape
    ctx_len = ctx.shape[1]
    depth = ada_w.shape[0]
    n_heads, head = rwkv_rk.shape[1], rwkv_rk.shape[2]
    n_mixers = 2
    tt = TOKEN_TILE
    assert seq % tt == 0 and ctx_len % tt == 0 and tt % GRID_W == 0
    assert head == WKV_CHUNK and 2 * head == LANES and d % LANES == 0
    nl, nct = seq // tt, ctx_len // tt

    xc = jnp.concatenate([x, ctx], axis=1)

    rows = -(-(bsz + 1) // SUBLANES) * SUBLANES
    cond = jnp.zeros((rows, d), F32).at[:bsz].set(c).at[bsz].set(c_ctx)
    m_all = _ada_all(cond, ada_w, ada_b)
    m_lat = m_all[:, :bsz].reshape(depth, bsz, 1, 6, d)
    m_ctx = jnp.broadcast_to(m_all[:, bsz].reshape(depth, 1, 1, 6, d), (depth, bsz, 1, 6, d))
    mods = jnp.concatenate([m_lat, m_ctx], axis=2)

    head_of = jnp.arange(d) // head
    hs = (head_of[:, None] == jnp.arange(HEAD_COLS)[None, :]).astype(BF16)
    hst = jnp.transpose(hs)

    w1_all, w3_all, w2_all = moe_w1.astype(BF16), moe_w3.astype(BF16), moe_w2.astype(BF16)
    v_first = None
    for i in range(depth):
        last = i == depth - 1
        is_rwkv = i % n_mixers == 0
        jn = i // n_mixers
        mod = mods[i]
        gn1 = norm1_g[i].reshape(1, d)
        if is_rwkv:
            p = {
                'mix': rwkv_mix[jn],
                'wr': rwkv_wrkv[jn, 0].astype(BF16), 'wk': rwkv_wrkv[jn, 1].astype(BF16),
                'wv': rwkv_wrkv[jn, 2].astype(BF16),
                'g1': rwkv_g1[jn].astype(BF16), 'g2': rwkv_g2[jn].astype(BF16),
                'w1': _lora_in(rwkv_w1[jn]), 'w2': _lora_out(rwkv_w2[jn]), 'w0': rwkv_w0[jn],
                'a1': _lora_in(rwkv_a1[jn]), 'a2': _lora_out(rwkv_a2[jn]), 'a0': rwkv_a0[jn],
                'k_k': rwkv_kk[jn].reshape(1, d), 'k_a': rwkv_ka[jn].reshape(1, d),
                'r_k': rwkv_rk[jn].reshape(1, d), 'ln_w': rwkv_lnw[jn].reshape(1, d),
                'ln_b': rwkv_lnb[jn].reshape(1, d), 'wo': rwkv_wo[jn].astype(BF16),
                'hs': hs, 'hst': hst,
            }
            vres = None if jn == 0 else (rwkv_v0[jn - 1].reshape(1, d), rwkv_v1[jn - 1].astype(BF16),
                                         rwkv_v2[jn - 1].astype(BF16))
            r, v, g, kk, lw0, lw1, b0, b1, kd0, kd1, bsum = _rwkv_proj(xc, mod, gn1, p, vres, v_first,
                                                                       nl, seq, ctx_len)
            if v_first is None:
                v_first = v
            yf, yr = _wkv(r, v, kk, lw0, b0, kd0, lw1, b1, kd1, seq, ctx_len)
            xc = _rwkv_out(xc, yf, yr, v, g, bsum, mod, p, nl, head)
        else:
            n_tiles = nl if last else nl + nct
            xc = _pool(xc, mod, gn1, pool_w[jn].astype(BF16), pool_scale[jn].reshape(1, d),
                       nl, seq, ctx_len, n_tiles)
        xc = _moe(xc, mod, norm2_g[i].reshape(1, d), jnp.transpose(moe_router[i]),
                  i, w1_all, w3_all, w2_all, nl, nct, seq, ctx_len, not last,
                  final_g.reshape(1, d) if last else None)
    return xc
```
